```python
import math
import jax, jax.numpy as jnp
from jax import lax
import numpy as np

D_MODEL = 2048
BATCH = 4
SEQ = 8192
DEPTH = 2
DEC_BATCH = 2
DEC_SEQ = 8192
PAST_LEN = 128

GRID_W = 64
FNET_GROUPS = 8
FNET_GROUP_DIM = 128
FNET_WIDTH = FNET_GROUPS * FNET_GROUP_DIM
HEAD_DIM = 128
N_Q_HEADS = 8
N_KV_HEADS = 2
ATTN_WIDTH = N_Q_HEADS * HEAD_DIM
KV_WIDTH = N_KV_HEADS * HEAD_DIM
IN0_WIDTH = FNET_WIDTH + ATTN_WIDTH + 2 * KV_WIDTH
MIX0_WIDTH = FNET_WIDTH + ATTN_WIDTH
ROPE_THETA = 10000.0
Q_BLOCK = 128
SSM_EXPAND = 2
SSM_D_INNER = SSM_EXPAND * D_MODEL
SSM_HEAD_DIM = 64
SSM_N_HEADS = SSM_D_INNER // SSM_HEAD_DIM
SSM_N_GROUPS = 8
SSM_D_STATE = 128
SSM_CONV = 3
SSM_CHUNK = 128
SSM_BC_WIDTH = SSM_N_GROUPS * SSM_D_STATE
SSM_CONV_DIM = SSM_D_INNER + 2 * SSM_BC_WIDTH
IN1_WIDTH = SSM_D_INNER + SSM_CONV_DIM + 2 * SSM_N_HEADS
D_FF = 5632
FFN_CONV = 3
EPS = 1e-6

kernel_name = 'hybrid_fourier_attn_ssd_encoder'


def rms_norm(x, g):
    xf = x.astype(jnp.float32)
    y = xf * lax.rsqrt(jnp.mean(xf * xf, axis=-1, keepdims=True) + EPS)
    return (y * g.astype(jnp.float32)).astype(x.dtype)


def centred_dwconv(x, w, b):
    k = w.shape[0]
    half = k // 2
    s = x.shape[1]
    xp = jnp.pad(x, ((0, 0), (half, half), (0, 0)))
    out = b
    for i in range(k):
        out = out + w[i] * xp[:, i:i + s]
    return out


def axial_rope_tables(rows):
    row = jnp.repeat(jnp.arange(rows, dtype=jnp.float32), GRID_W)
    col = jnp.tile(jnp.arange(GRID_W, dtype=jnp.float32), rows)
    axis_dim = HEAD_DIM // 2
    inv_freq = ROPE_THETA ** (-jnp.arange(0, axis_dim, 2, dtype=jnp.float32) / axis_dim)
    ang = jnp.concatenate([row[:, None] * inv_freq, col[:, None] * inv_freq], axis=-1)
    return jnp.cos(ang), jnp.sin(ang)


def apply_rope(x, cos, sin):
    x1, x2 = jnp.split(x, 2, axis=-1)
    c = cos[None, :, None, :]
    s = sin[None, :, None, :]
    return jnp.concatenate([x1 * c - x2 * s, x2 * c + x1 * s], axis=-1).astype(x.dtype)


def block_attention(q, k, v):
    b, s, hq, d = q.shape
    g = hq // N_KV_HEADS
    nb = s // Q_BLOCK
    qb = q.reshape(b, nb, Q_BLOCK, N_KV_HEADS, g, d).transpose(1, 0, 3, 4, 2, 5)
    scale = d ** -0.5

    def one_block(qblk):
        sc = jnp.einsum('bhgqd,bkhd->bhgqk', qblk, k, preferred_element_type=jnp.float32) * scale
        p = jax.nn.softmax(sc, axis=-1)
        return jnp.einsum('bhgqk,bkhd->bhgqd', p.astype(v.dtype), v)

    o = lax.map(one_block, qb)
    return o.transpose(1, 0, 4, 2, 3, 5).reshape(b, s, hq * d)


def fourier_mix(u):
    b, s, _ = u.shape
    ug = u.reshape(b, s, FNET_GROUPS, FNET_GROUP_DIM).astype(jnp.float32)
    f = jnp.fft.fft2(ug, axes=(1, 3), norm='ortho').real
    return f.reshape(b, s, FNET_WIDTH).astype(u.dtype)


def fourier_attention_mixer(h, w_in, q_gain, k_gain, w_out, cos, sin):
    b, s, _ = h.shape
    proj = h @ w_in
    u, q, k, v = jnp.split(proj, [FNET_WIDTH, FNET_WIDTH + ATTN_WIDTH, FNET_WIDTH + ATTN_WIDTH + KV_WIDTH], axis=-1)
    q = apply_rope(rms_norm(q.reshape(b, s, N_Q_HEADS, HEAD_DIM), q_gain), cos, sin)
    k = apply_rope(rms_norm(k.reshape(b, s, N_KV_HEADS, HEAD_DIM), k_gain), cos, sin)
    v = v.reshape(b, s, N_KV_HEADS, HEAD_DIM)
    a = block_attention(q, k, v)
    mixed = jnp.concatenate([fourier_mix(u), a], axis=-1)
    return mixed @ w_out


def ssd_chunked(x, dt, a, bm, cm):
    b, s, h, p = x.shape
    g, n = bm.shape[-2:]
    hpg = h // g
    L = SSM_CHUNK
    nc = s // L
    f32 = jnp.float32
    xc = (x.astype(f32) * dt[..., None]).reshape(b, nc, L, g, hpg, p)
    bc = bm.astype(f32).reshape(b, nc, L, g, n)
    cc = cm.astype(f32).reshape(b, nc, L, g, n)
    acum = jnp.cumsum((dt * a).reshape(b, nc, L, h), axis=2)
    acum_h = acum.transpose(0, 1, 3, 2)
    lower = jnp.tril(jnp.ones((L, L), dtype=bool))
    decay = jnp.exp(jnp.where(lower, acum_h[..., :, None] - acum_h[..., None, :], -jnp.inf))
    cb = jnp.einsum('bclgn,bcsgn->bcgls', cc, bc)
    w = cb[:, :, :, None] * decay.reshape(b, nc, g, hpg, L, L)
    y_diag = jnp.einsum('bcgjls,bcsgjp->bclgjp', w, xc)
    to_end = jnp.exp(acum[:, :, -1:, :] - acum).reshape(b, nc, L, g, hpg)
    states = jnp.einsum('bclgn,bclgjp->bcgjpn', bc, xc * to_end[..., None])
    chunk_decay = jnp.exp(acum[:, :, -1, :]).reshape(b, nc, g, hpg)

    def step(carry, inp):
        st, dec = inp
        return carry * dec[..., None, None] + st, carry

    init = jnp.zeros((b, g, hpg, p, n), f32)
    _, prev = lax.scan(step, init, (jnp.moveaxis(states, 1, 0), jnp.moveaxis(chunk_decay, 1, 0)))
    prev = jnp.moveaxis(prev, 0, 1)
    from_start = jnp.exp(acum).reshape(b, nc, L, g, hpg)
    y_off = jnp.einsum('bclgn,bcgjpn->bclgjp', cc, prev) * from_start[..., None]
    return (y_diag + y_off).reshape(b, s, h, p)


def bidirectional_ssd_mixer(h, w_in, conv_w, conv_b, dt_bias, a_log, d_skip, norm_g, w_out):
    b, s, _ = h.shape
    f32 = jnp.float32
    proj = h @ w_in
    z, xbc, dt_raw = jnp.split(proj, [SSM_D_INNER, SSM_D_INNER + SSM_CONV_DIM], axis=-1)
    xbc = jax.nn.silu(centred_dwconv(xbc, conv_w, conv_b))
    xs, bm, cm = jnp.split(xbc, [SSM_D_INNER, SSM_D_INNER + SSM_BC_WIDTH], axis=-1)
    xs = xs.reshape(b, s, SSM_N_HEADS, SSM_HEAD_DIM)
    bm = bm.reshape(b, s, SSM_N_GROUPS, SSM_D_STATE)
    cm = cm.reshape(b, s, SSM_N_GROUPS, SSM_D_STATE)
    dt = jax.nn.softplus(dt_raw.astype(f32).reshape(b, s, 2, SSM_N_HEADS) + dt_bias.astype(f32))
    a = -jnp.exp(a_log.astype(f32))
    y_f = ssd_chunked(xs, dt[:, :, 0], a[0], bm, cm)
    y_b = jnp.flip(ssd_chunked(jnp.flip(xs, 1), jnp.flip(dt[:, :, 1], 1), a[1], jnp.flip(bm, 1), jnp.flip(cm, 1)), 1)
    y = y_f + y_b + xs.astype(f32) * d_skip.astype(f32)[:, None]
    y = y.reshape(b, s, SSM_D_INNER) * jax.nn.silu(z.astype(f32))
    y = rms_norm(y, norm_g)
    return y.astype(h.dtype) @ w_out


def conv_gated_ffn(h, w_up, conv_w, conv_b, w_down):
    u = centred_dwconv(h @ w_up, conv_w, conv_b)
    gate, val = jnp.split(u, 2, axis=-1)
    return (jax.nn.gelu(gate, approximate=True) * val) @ w_down


def trunk(x, mix_pre_g, mix_post_g, ffn_pre_g, ffn_post_g,
          fa_w_in, fa_q_gain, fa_k_gain, fa_w_out,
          ssd_w_in, ssd_conv_w, ssd_conv_b, ssd_dt_bias, ssd_a_log, ssd_d, ssd_norm_g, ssd_w_out,
          ffn_w_up, ffn_conv_w, ffn_conv_b, ffn_w_down):
    rows = x.shape[1] // GRID_W
    cos, sin = axial_rope_tables(rows)
    for i in range(DEPTH):
        j = i // 2
        h = rms_norm(x, mix_pre_g[i])
        if i % 2 == 0:
            m = fourier_attention_mixer(h, fa_w_in[j], fa_q_gain[j], fa_k_gain[j], fa_w_out[j], cos, sin)
        else:
            m = bidirectional_ssd_mixer(h, ssd_w_in[j], ssd_conv_w[j], ssd_conv_b[j], ssd_dt_bias[j],
                                        ssd_a_log[j], ssd_d[j], ssd_norm_g[j], ssd_w_out[j])
        x = x + rms_norm(m, mix_post_g[i])
        h = rms_norm(x, ffn_pre_g[i])
        f = conv_gated_ffn(h, ffn_w_up[i], ffn_conv_w[i], ffn_conv_b[i], ffn_w_down[i])
        x = x + rms_norm(f, ffn_post_g[i])
    return x


def setup_inputs(seed: int = 0) -> dict:
    key = jax.random.key(seed)
    ks = jax.random.split(key, 26)
    f32 = jnp.float32
    n_even = (DEPTH + 1) // 2
    n_odd = DEPTH // 2

    def dense(k, shape, fan_in):
        return jax.random.normal(k, shape, f32) * fan_in ** -0.5

    def gain(k, shape):
        return 1.0 + 0.02 * jax.random.normal(k, shape, f32)

    dt0 = jnp.exp(jax.random.uniform(ks[14], (n_odd, 2, SSM_N_HEADS), f32, math.log(1e-3), math.log(1e-1)))
    return {
        'x_prompt': jax.random.normal(ks[0], (BATCH, SEQ, D_MODEL), f32),
        'x_sample': jax.random.normal(ks[1], (DEC_BATCH, DEC_SEQ, D_MODEL), f32),
        'mix_pre_g': gain(ks[2], (DEPTH, D_MODEL)),
        'mix_post_g': gain(ks[3], (DEPTH, D_MODEL)),
        'ffn_pre_g': gain(ks[4], (DEPTH, D_MODEL)),
        'ffn_post_g': gain(ks[5], (DEPTH, D_MODEL)),
        'fa_w_in': dense(ks[6], (n_even, D_MODEL, IN0_WIDTH), D_MODEL),
        'fa_q_gain': gain(ks[7], (n_even, HEAD_DIM)),
        'fa_k_gain': gain(ks[8], (n_even, HEAD_DIM)),
        'fa_w_out': dense(ks[9], (n_even, MIX0_WIDTH, D_MODEL), MIX0_WIDTH),
        'ssd_w_in': dense(ks[10], (n_odd, D_MODEL, IN1_WIDTH), D_MODEL),
        'ssd_conv_w': dense(ks[11], (n_odd, SSM_CONV, SSM_CONV_DIM), SSM_CONV),
        'ssd_conv_b': 0.02 * jax.random.normal(ks[12], (n_odd, SSM_CONV_DIM), f32),
        'ssd_dt_bias': dt0 + jnp.log(-jnp.expm1(-dt0)),
        'ssd_a_log': jnp.log(jax.random.uniform(ks[15], (n_odd, 2, SSM_N_HEADS), f32, 1.0, 16.0)),
        'ssd_d': gain(ks[16], (n_odd, SSM_N_HEADS)),
        'ssd_norm_g': gain(ks[17], (n_odd, SSM_D_INNER)),
        'ssd_w_out': dense(ks[18], (n_odd, SSM_D_INNER, D_MODEL), SSM_D_INNER),
        'ffn_w_up': dense(ks[19], (DEPTH, D_MODEL, 2 * D_FF), D_MODEL),
        'ffn_conv_w': dense(ks[20], (DEPTH, FFN_CONV, 2 * D_FF), FFN_CONV),
        'ffn_conv_b': 0.02 * jax.random.normal(ks[21], (DEPTH, 2 * D_FF), f32),
        'ffn_w_down': dense(ks[22], (DEPTH, D_FF, D_MODEL), D_FF),
    }


def reference(x_prompt, x_sample, mix_pre_g, mix_post_g, ffn_pre_g, ffn_post_g,
              fa_w_in, fa_q_gain, fa_k_gain, fa_w_out,
              ssd_w_in, ssd_conv_w, ssd_conv_b, ssd_dt_bias, ssd_a_log, ssd_d, ssd_norm_g, ssd_w_out,
              ffn_w_up, ffn_conv_w, ffn_conv_b, ffn_w_down):
    weights = (mix_pre_g, mix_post_g, ffn_pre_g, ffn_post_g,
               fa_w_in, fa_q_gain, fa_k_gain, fa_w_out,
               ssd_w_in, ssd_conv_w, ssd_conv_b, ssd_dt_bias, ssd_a_log, ssd_d, ssd_norm_g, ssd_w_out,
               ffn_w_up, ffn_conv_w, ffn_conv_b, ffn_w_down)
    y_prompt = trunk(x_prompt, *weights)
    y_sample = trunk(x_sample, *weights)
    return (y_prompt, y_sample)
```

```python
import functools
import math

import jax
import jax.numpy as jnp
import numpy as np
from jax import lax
from jax.experimental import pallas as pl
from jax.experimental.pallas import tpu as pltpu

F32 = jnp.float32
BF16 = jnp.bfloat16

EPS = 1e-6
GRID_W = 64
HEAD_DIM = 128
N_Q_HEADS = 8
N_KV_HEADS = 2
FNET_GROUP_DIM = 128
ROPE_THETA = 10000.0
SSM_HEAD_DIM = 64
SSM_N_GROUPS = 8
SSM_D_STATE = 128
SSM_CHUNK = 128
LANES = 128
HALO = 16
VMEM_LIMIT = 56 * 1024 * 1024

_NT = (((1,), (1,)), ((), ()))


def _params(*sem):
    return pltpu.CompilerParams(dimension_semantics=sem, vmem_limit_bytes=VMEM_LIMIT)


def _resident(shape):
    nd = len(shape)
    return pl.BlockSpec(shape, lambda *_: (0,) * nd, pipeline_mode=pl.Buffered(1))


def _rms(x, g):
    return x * lax.rsqrt(jnp.mean(x * x, axis=-1, keepdims=True) + EPS) * g


def _dot(a, b):
    return jnp.dot(a, b, preferred_element_type=F32)


def _inproj0_kernel(x_ref, g_ref, w_ref, qg_ref, kg_ref, cos_ref, sin_ref,
                    u_ref, q_ref, k_ref, v_ref, *, fw, aw, kvw, scale):
    h = _rms(x_ref[...], g_ref[...]).astype(BF16)
    u_ref[...] = _dot(h, w_ref[:, :fw]).astype(BF16)
    cosf = cos_ref[...]
    sinf = sin_ref[...]

    def norm_rope(t, gain, mult):
        t = _rms(t, gain)
        t = t * cosf + pltpu.roll(t, HEAD_DIM // 2, 1) * sinf
        return (t * mult).astype(BF16)

    q = _dot(h, w_ref[:, fw:fw + aw])
    for hh in range(aw // HEAD_DIM):
        sl = slice(hh * HEAD_DIM, (hh + 1) * HEAD_DIM)
        q_ref[:, sl] = norm_rope(q[:, sl], qg_ref[...], scale)
    k = _dot(h, w_ref[:, fw + aw:fw + aw + kvw])
    for hh in range(kvw // HEAD_DIM):
        sl = slice(hh * HEAD_DIM, (hh + 1) * HEAD_DIM)
        k_ref[:, sl] = norm_rope(k[:, sl], kg_ref[...], 1.0)
    v_ref[...] = _dot(h, w_ref[:, fw + aw + kvw:]).astype(BF16)


def _inproj0(x, g, w, qg, kg, cosf, sinf, *, seq, fw, aw, kvw, tm):
    t, d = x.shape
    n = w.shape[1]
    tps = seq // tm
    kern = functools.partial(_inproj0_kernel, fw=fw, aw=aw, kvw=kvw, scale=HEAD_DIM ** -0.5)
    row = lambda width: pl.BlockSpec((tm, width), lambda i: (i, 0))
    return pl.pallas_call(
        kern,
        grid=(t // tm,),
        in_specs=[row(d), _resident((1, d)), _resident((d, n)),
                  _resident((1, HEAD_DIM)), _resident((1, HEAD_DIM)),
                  pl.BlockSpec((tm, HEAD_DIM), lambda i: (i % tps, 0)),
                  pl.BlockSpec((tm, HEAD_DIM), lambda i: (i % tps, 0))],
        out_specs=[row(fw), row(aw), row(kvw), row(kvw)],
        out_shape=[jax.ShapeDtypeStruct((t, fw), BF16), jax.ShapeDtypeStruct((t, aw), BF16),
                   jax.ShapeDtypeStruct((t, kvw), BF16), jax.ShapeDtypeStruct((t, kvw), BF16)],
        compiler_params=_params("parallel"),
        name="inproj0",
    )(x, g, w, qg, kg, cosf, sinf)


def _attn_kernel(q_ref, k_ref, v_ref, o_ref, m_sc, l_sc, acc_sc, *, group):
    ki = pl.program_id(3)

    @pl.when(ki == 0)
    def _():
        m_sc[...] = jnp.full(m_sc.shape, -jnp.inf, F32)
        l_sc[...] = jnp.zeros(l_sc.shape, F32)
        acc_sc[...] = jnp.zeros(acc_sc.shape, F32)

    k = k_ref[0]
    v = v_ref[0]
    for j in range(group):
        q = q_ref[0, :, j * HEAD_DIM:(j + 1) * HEAD_DIM]
        s = lax.dot_general(q, k, _NT, preferred_element_type=F32)
        m_prev = m_sc[j]
        m_new = jnp.maximum(m_prev, jnp.max(s, axis=-1, keepdims=True))
        alpha = jnp.exp(m_prev - m_new)
        p = jnp.exp(s - m_new)
        l_sc[j] = alpha * l_sc[j] + jnp.sum(p, axis=-1, keepdims=True)
        acc_sc[j] = alpha * acc_sc[j] + _dot(p.astype(BF16), v)
        m_sc[j] = m_new

    @pl.when(ki == pl.num_programs(3) - 1)
    def _():
        for j in range(group):
            o_ref[0, :, j * HEAD_DIM:(j + 1) * HEAD_DIM] = (acc_sc[j] / l_sc[j]).astype(BF16)


def _attention(q, k, v, *, tq, tk):
    b, s, aw = q.shape
    group = aw // HEAD_DIM // N_KV_HEADS
    gw = group * HEAD_DIM
    return pl.pallas_call(
        functools.partial(_attn_kernel, group=group),
        grid=(b, N_KV_HEADS, s // tq, s // tk),
        in_specs=[pl.BlockSpec((1, tq, gw), lambda bi, h, qi, ki: (bi, qi, h)),
                  pl.BlockSpec((1, tk, HEAD_DIM), lambda bi, h, qi, ki: (bi, ki, h)),
                  pl.BlockSpec((1, tk, HEAD_DIM), lambda bi, h, qi, ki: (bi, ki, h))],
        out_specs=pl.BlockSpec((1, tq, gw), lambda bi, h, qi, ki: (bi, qi, h)),
        out_shape=jax.ShapeDtypeStruct((b, s, aw), BF16),
        scratch_shapes=[pltpu.VMEM((group, tq, 1), F32), pltpu.VMEM((group, tq, 1), F32),
                        pltpu.VMEM((group, tq, HEAD_DIM), F32)],
        compiler_params=_params("parallel", "parallel", "parallel", "arbitrary"),
        name="attention",
    )(q, k, v)


def _fourier1_kernel(f_ref, u_ref, y_ref, *, n1):
    y = _dot(f_ref[...], u_ref[0])
    y_ref[0, 0] = y[:n1].astype(BF16)
    y_ref[0, 1] = y[n1:].astype(BF16)


def _fourier2_kernel(m_ref, y_ref, cc_ref, sc_ref, o_ref, *, n2, groups):
    y = y_ref[0].reshape(2 * n2, groups * FNET_GROUP_DIM)
    g = _dot(m_ref[0], y)
    gr = g[:n2].astype(BF16)
    gi = g[n2:].astype(BF16)
    for c in range(groups):
        sl = slice(c * FNET_GROUP_DIM, (c + 1) * FNET_GROUP_DIM)
        o_ref[0, :, sl] = (_dot(gr[:, sl], cc_ref[...]) + _dot(gi[:, sl], sc_ref[...])).astype(BF16)


def _fourier_tables(seq):
    n2 = 128
    n1 = seq // n2
    a1 = 2.0 * np.pi * np.outer(np.arange(n1), np.arange(n1)) / n1
    f1 = np.concatenate([np.cos(a1), -np.sin(a1)], axis=0)
    kk = np.arange(n1)[:, None, None] + n1 * np.arange(n2)[None, :, None]
    th = 2.0 * np.pi * (kk * np.arange(n2)[None, None, :] % seq) / seq
    mc, ms = np.cos(th), np.sin(th)
    m = np.concatenate([np.concatenate([mc, ms], axis=2),
                        np.concatenate([-ms, mc], axis=2)], axis=1)
    ac = 2.0 * np.pi * np.outer(np.arange(FNET_GROUP_DIM), np.arange(FNET_GROUP_DIM)) / FNET_GROUP_DIM
    norm = 1.0 / math.sqrt(seq * FNET_GROUP_DIM)
    return (jnp.asarray(f1, BF16), jnp.asarray(m, BF16),
            jnp.asarray(np.cos(ac) * norm, BF16), jnp.asarray(np.sin(ac) * norm, BF16))


def _fourier_mix(u, *, batch, seq, tn):
    t, width = u.shape
    n2 = 128
    n1 = seq // n2
    groups = width // FNET_GROUP_DIM
    f1, m, cc, sc = _fourier_tables(seq)
    cols = n2 * width
    y = pl.pallas_call(
        functools.partial(_fourier1_kernel, n1=n1),
        grid=(batch, cols // tn),
        in_specs=[_resident((2 * n1, n1)),
                  pl.BlockSpec((1, n1, tn), lambda b, j: (b, 0, j))],
        out_specs=pl.BlockSpec((1, 2, n1, tn), lambda b, j: (b, 0, 0, j)),
        out_shape=jax.ShapeDtypeStruct((batch, 2, n1, cols), BF16),
        compiler_params=_params("parallel", "parallel"),
        name="fourier_seq1",
    )(f1, u.reshape(batch, n1, cols))
    y = y.reshape(batch, 2, seq, width)
    out = pl.pallas_call(
        functools.partial(_fourier2_kernel, n2=n2, groups=groups),
        grid=(n1, batch),
        in_specs=[pl.BlockSpec((1, 2 * n2, 2 * n2), lambda k1, b: (k1, 0, 0)),
                  pl.BlockSpec((1, 2, n2, width), lambda k1, b: (b, 0, k1, 0)),
                  _resident((FNET_GROUP_DIM, FNET_GROUP_DIM)),
                  _resident((FNET_GROUP_DIM, FNET_GROUP_DIM))],
        out_specs=pl.BlockSpec((1, n2, width), lambda k1, b: (b, 0, k1)),
        out_shape=jax.ShapeDtypeStruct((batch, n2, n1 * width), BF16),
        compiler_params=_params("parallel", "parallel"),
        name="fourier_seq2",
    )(m, y, cc, sc)
    return out.reshape(t, width)


def _proj_res_kernel(*refs, n_lhs, has_next):
    lhs = refs[:n_lhs]
    ws = refs[n_lhs:2 * n_lhs]
    x_ref, gp_ref = refs[2 * n_lhs:2 * n_lhs + 2]
    pos = 2 * n_lhs + 2
    gn_ref = refs[pos] if has_next else None
    pos += int(has_next)
    xo_ref = refs[pos]
    ho_ref = refs[pos + 1] if has_next else None
    acc = refs[-1]
    kk = pl.program_id(1)

    @pl.when(kk == 0)
    def _():
        acc[...] = jnp.zeros(acc.shape, F32)

    part = _dot(lhs[0][...], ws[0][...])
    for a, w in zip(lhs[1:], ws[1:]):
        part += _dot(a[...], w[...])
    acc[...] += part

    @pl.when(kk == pl.num_programs(1) - 1)
    def _():
        xn = x_ref[...] + _rms(acc[...], gp_ref[...])
        xo_ref[...] = xn
        if has_next:
            ho_ref[...] = _rms(xn, gn_ref[...]).astype(BF16)


def _proj_res(lhs, ws, x, g_post, g_next, *, tm, tk):
    t, d = x.shape
    n_lhs = len(lhs)
    kdim = lhs[0].shape[1]
    has_next = g_next is not None
    in_specs = [pl.BlockSpec((tm, tk), lambda i, k: (i, k)) for _ in lhs]
    in_specs += [pl.BlockSpec((tk, d), lambda i, k: (k, 0)) for _ in ws]
    in_specs += [pl.BlockSpec((tm, d), lambda i, k: (i, 0)), _resident((1, d))]
    args = list(lhs) + list(ws) + [x, g_post]
    out_specs = [pl.BlockSpec((tm, d), lambda i, k: (i, 0))]
    out_shape = [jax.ShapeDtypeStruct((t, d), F32)]
    if has_next:
        in_specs.append(_resident((1, d)))
        args.append(g_next)
        out_specs.append(pl.BlockSpec((tm, d), lambda i, k: (i, 0)))
        out_shape.append(jax.ShapeDtypeStruct((t, d), BF16))
    res = pl.pallas_call(
        functools.partial(_proj_res_kernel, n_lhs=n_lhs, has_next=has_next),
        grid=(t // tm, kdim // tk),
        in_specs=in_specs, out_specs=out_specs, out_shape=out_shape,
        scratch_shapes=[pltpu.VMEM((tm, d), F32)],
        compiler_params=_params("parallel", "arbitrary"),
        name="proj_res",
    )(*args)
    return res if has_next else (res[0], None)


def _gelu_tanh(x):
    return 0.5 * x * (1.0 + jnp.tanh(math.sqrt(2.0 / math.pi) * (x + 0.044715 * x * x * x)))


def _silu(x):
    return x * (1.0 / (1.0 + jnp.exp(-x)))


def _conv_mm_kernel(*refs, act, tm, tps):
    if act == "none":
        h_ref, w_ref, o_ref = refs
        o_ref[...] = _dot(h_ref[...], w_ref[...]).astype(o_ref.dtype)
        return
    n_w = 2 if act == "glu" else 1
    h_ref, hp_ref, hn_ref = refs[:3]
    w_refs = refs[3:3 + n_w]
    cw_refs = refs[3 + n_w:3 + 2 * n_w]
    cb_refs = refs[3 + 2 * n_w:3 + 3 * n_w]
    o_ref = refs[3 + 3 * n_w]
    hext = refs[-1]
    i = pl.program_id(0)

    @pl.when(pl.program_id(1) == 0)
    def _():
        first = (i % tps) == 0
        last = (i % tps) == tps - 1
        hext[0:HALO] = jnp.where(first, jnp.zeros_like(hp_ref[...]), hp_ref[...])
        hext[HALO:HALO + tm] = h_ref[...]
        hext[HALO + tm:] = jnp.where(last, jnp.zeros_like(hn_ref[...]), hn_ref[...])

    rows = tm + 2 * HALO

    def conv(w_ref, cw_ref, cb_ref):
        u = _dot(hext[...], w_ref[...])
        cw = cw_ref[...]
        c = cb_ref[...] + cw[0:1] * pltpu.roll(u, 1, 0) + cw[1:2] * u + cw[2:3] * pltpu.roll(u, rows - 1, 0)
        return c[HALO:HALO + tm]

    if act == "glu":
        gate = conv(w_refs[0], cw_refs[0], cb_refs[0])
        val = conv(w_refs[1], cw_refs[1], cb_refs[1])
        o_ref[...] = (_gelu_tanh(gate) * val).astype(o_ref.dtype)
    else:
        o_ref[...] = _silu(conv(w_refs[0], cw_refs[0], cb_refs[0])).astype(o_ref.dtype)


def _conv_mm(h, w, conv_w, conv_b, *, act, seq, tm, tn, col0, ncols, out_dtype=BF16):
    t, kdim = h.shape
    nj = ncols // tn
    j0 = col0 // tn
    tps = seq // tm
    if act == "none":
        in_specs = [pl.BlockSpec((tm, kdim), lambda i, j: (i, 0)),
                    pl.BlockSpec((kdim, tn), lambda i, j: (0, j0 + j))]
        args = [h, w]
        scratch = []
    else:
        offs = [j0, j0 + nj] if act == "glu" else [j0]
        hb = tm // HALO
        last_blk = t // HALO - 1
        in_specs = [pl.BlockSpec((tm, kdim), lambda i, j: (i, 0)),
                    pl.BlockSpec((HALO, kdim), lambda i, j: (jnp.maximum(i * hb - 1, 0), 0)),
                    pl.BlockSpec((HALO, kdim), lambda i, j: (jnp.minimum((i + 1) * hb, last_blk), 0))]
        in_specs += [pl.BlockSpec((kdim, tn), lambda i, j, o=o: (0, o + j)) for o in offs]
        cj0 = [0, nj] if act == "glu" else [0]
        in_specs += [pl.BlockSpec((3, tn), lambda i, j, o=o: (0, o + j)) for o in cj0]
        in_specs += [pl.BlockSpec((1, tn), lambda i, j, o=o: (0, o + j)) for o in cj0]
        args = [h, h, h] + [w] * len(offs) + [conv_w] * len(offs) + [conv_b] * len(offs)
        scratch = [pltpu.VMEM((tm + 2 * HALO, kdim), BF16)]
    return pl.pallas_call(
        functools.partial(_conv_mm_kernel, act=act, tm=tm, tps=tps),
        grid=(t // tm, nj),
        in_specs=in_specs,
        out_specs=pl.BlockSpec((tm, tn), lambda i, j: (i, j)),
        out_shape=jax.ShapeDtypeStruct((t, ncols), out_dtype),
        scratch_shapes=scratch,
        compiler_params=_params("parallel", "arbitrary"),
        name="conv_mm_" + act,
    )(*args)


def _dt_kernel(h_ref, w_ref, b_ref, o_ref):
    r = _dot(h_ref[...], w_ref[...]) + b_ref[...]
    o_ref[...] = jnp.maximum(r, 0.0) + jnp.log(1.0 + jnp.exp(-jnp.abs(r)))


def _dt_proj(h, w, b, *, tm):
    t, kdim = h.shape
    n = w.shape[1]
    return pl.pallas_call(
        _dt_kernel,
        grid=(t // tm,),
        in_specs=[pl.BlockSpec((tm, kdim), lambda i: (i, 0)), _resident((kdim, n)), _resident((1, n))],
        out_specs=pl.BlockSpec((tm, n), lambda i: (i, 0)),
        out_shape=jax.ShapeDtypeStruct((t, n), F32),
        compiler_params=_params("parallel"),
        name="ssd_dt",
    )(h, w, b)


def _split3(x):
    hi = x.astype(BF16)
    r = x - hi.astype(F32)
    mid = r.astype(BF16)
    lo = (r - mid.astype(F32)).astype(BF16)
    return hi, mid, lo


def _cumsum_rows(tri, x):
    hi, mid, lo = _split3(x)
    return _dot(tri, hi) + _dot(tri, mid) + _dot(tri, lo)


def _expand(x, e):
    hi = x.astype(BF16)
    lo = (x - hi.astype(F32)).astype(BF16)
    return _dot(hi, e) + _dot(lo, e)


def _ssd_direction(x_ref, b_ref, c_ref, dt_ref, a_ref, tri_ref, e_ref, h_sc, y_ref, *, lane0, hpg, backward):
    L = SSM_CHUNK
    x = x_ref[...]
    bm = b_ref[...]
    cm = c_ref[...]
    dt = dt_ref[...]
    dta = dt * a_ref[...]
    incl = _cumsum_rows(tri_ref[...], dta)
    total = incl[L - 1:L, :]
    if backward:
        pos = incl - dta
        w_state = dt * jnp.exp(pos)
        w_out = jnp.exp(total - pos)
    else:
        pos = incl
        w_state = dt * jnp.exp(total - pos)
        w_out = jnp.exp(pos)
    e = e_ref[...]
    w_state_x = _expand(w_state, e)
    w_out_x = _expand(w_out, e)
    carry = (w_out_x[0:1] if backward else w_out_x[L - 1:L])
    pos_t = pos.T
    dt_t = dt.T

    h_in = h_sc[...]
    y = _dot(cm, h_in.astype(BF16)) * w_out_x
    xs = (x.astype(F32) * w_state_x).astype(BF16)
    b_t = bm.astype(F32).T.astype(BF16)
    h_sc[...] = h_in * carry + _dot(b_t, xs)

    cb = lax.dot_general(cm, bm, _NT, preferred_element_type=F32)
    li = lax.broadcasted_iota(jnp.int32, (L, L), 0)
    si = lax.broadcasted_iota(jnp.int32, (L, L), 1)
    mask = (si >= li) if backward else (li >= si)
    lane = lax.broadcasted_iota(jnp.int32, (L, 2 * SSM_HEAD_DIM), 1)
    for jp in range(hpg // 2):
        xp = x[:, jp * 2 * SSM_HEAD_DIM:(jp + 1) * 2 * SSM_HEAD_DIM]
        outs = []
        for j in (2 * jp, 2 * jp + 1):
            col = pos[:, lane0 + j:lane0 + j + 1]
            row = pos_t[lane0 + j:lane0 + j + 1, :]
            d = (row - col) if backward else (col - row)
            w = cb * jnp.exp(jnp.where(mask, d, -1e30)) * dt_t[lane0 + j:lane0 + j + 1, :]
            outs.append(_dot(w.astype(BF16), xp))
        sl = slice(jp * 2 * SSM_HEAD_DIM, (jp + 1) * 2 * SSM_HEAD_DIM)
        y_ref[:, sl] = (y[:, sl] + jnp.where(lane < SSM_HEAD_DIM, outs[0], outs[1])).astype(y_ref.dtype)


def _ssd_kernel(xf, bf, cf, dtf, xb, bb, cb, dtb, a_ref, tri_ref, ef_ref, eb_ref,
                yf_ref, yb_ref, hf_sc, hb_sc, *, hpg):
    @pl.when(pl.program_id(2) == 0)
    def _():
        hf_sc[...] = jnp.zeros(hf_sc.shape, F32)
        hb_sc[...] = jnp.zeros(hb_sc.shape, F32)

    _ssd_direction(xf, bf, cf, dtf, a_ref, tri_ref, ef_ref, hf_sc, yf_ref, lane0=0, hpg=hpg, backward=False)
    _ssd_direction(xb, bb, cb, dtb, a_ref, tri_ref, eb_ref, hb_sc, yb_ref, lane0=hpg, hpg=hpg, backward=True)


def _ssd(xbc, dt, a_pad, *, batch, seq, d_inner):
    t = xbc.shape[0]
    L = SSM_CHUNK
    nc = seq // L
    gw = d_inner // SSM_N_GROUPS
    hpg = gw // SSM_HEAD_DIM
    xblk = d_inner // gw
    b0 = d_inner // SSM_D_STATE
    c0 = b0 + SSM_N_GROUPS
    tri = jnp.asarray(np.tril(np.ones((L, L))), BF16)
    head_of_lane = np.arange(gw) // SSM_HEAD_DIM
    ef = jnp.asarray(np.arange(LANES)[:, None] == head_of_lane[None, :], BF16)
    eb = jnp.asarray(np.arange(LANES)[:, None] == (head_of_lane[None, :] + hpg), BF16)
    del xblk

    def fwd(col):
        return lambda b, g, c: (b * nc + c, col(g))

    def bwd(col):
        return lambda b, g, c: (b * nc + (nc - 1 - c), col(g))

    def specs(order):
        return [pl.BlockSpec((L, gw), order(lambda g: g)),
                pl.BlockSpec((L, SSM_D_STATE), order(lambda g: b0 + g)),
                pl.BlockSpec((L, SSM_D_STATE), order(lambda g: c0 + g)),
                pl.BlockSpec((L, LANES), order(lambda g: g))]

    in_specs = specs(fwd) + specs(bwd) + [
        pl.BlockSpec((1, LANES), lambda b, g, c: (0, g)),
        _resident((L, L)), _resident((LANES, gw)), _resident((LANES, gw))]
    return pl.pallas_call(
        functools.partial(_ssd_kernel, hpg=hpg),
        grid=(batch, SSM_N_GROUPS, nc),
        in_specs=in_specs,
        out_specs=[pl.BlockSpec((L, gw), fwd(lambda g: g)), pl.BlockSpec((L, gw), bwd(lambda g: g))],
        out_shape=[jax.ShapeDtypeStruct((t, d_inner), BF16), jax.ShapeDtypeStruct((t, d_inner), BF16)],
        scratch_shapes=[pltpu.VMEM((SSM_D_STATE, gw), F32), pltpu.VMEM((SSM_D_STATE, gw), F32)],
        compiler_params=_params("parallel", "parallel", "arbitrary"),
        name="ssd_scan",
    )(xbc, xbc, xbc, dt, xbc, xbc, xbc, dt, a_pad, tri, ef, eb)


def _ssd_out_kernel(yf_ref, yb_ref, xs_ref, z_ref, dsk_ref, ng_ref, w_ref, x_ref, gp_ref, gn_ref,
                    xo_ref, ho_ref, acc, ss, *, d_inner):
    kk = pl.program_id(1)

    @pl.when(kk == 0)
    def _():
        acc[...] = jnp.zeros(acc.shape, F32)
        ss[...] = jnp.zeros(ss.shape, F32)

    y = yf_ref[...].astype(F32) + yb_ref[...].astype(F32) + xs_ref[...].astype(F32) * dsk_ref[...]
    y = y * _silu(z_ref[...].astype(F32))
    ss[...] += jnp.sum(y * y, axis=-1, keepdims=True)
    acc[...] += _dot((y * ng_ref[...]).astype(BF16), w_ref[...])

    @pl.when(kk == pl.num_programs(1) - 1)
    def _():
        m = acc[...] * lax.rsqrt(ss[...] * (1.0 / d_inner) + EPS)
        xn = x_ref[...] + _rms(m, gp_ref[...])
        xo_ref[...] = xn
        ho_ref[...] = _rms(xn, gn_ref[...]).astype(BF16)


def _ssd_out(yf, yb, xbc, z, d_exp, norm_g, w, x, g_post, g_next, *, tm, tk):
    t, d = x.shape
    d_inner = yf.shape[1]
    lhs = pl.BlockSpec((tm, tk), lambda i, k: (i, k))
    vec = pl.BlockSpec((1, tk), lambda i, k: (0, k))
    row = pl.BlockSpec((tm, d), lambda i, k: (i, 0))
    return pl.pallas_call(
        functools.partial(_ssd_out_kernel, d_inner=d_inner),
        grid=(t // tm, d_inner // tk),
        in_specs=[lhs, lhs, lhs, lhs, vec, vec, pl.BlockSpec((tk, d), lambda i, k: (k, 0)),
                  row, _resident((1, d)), _resident((1, d))],
        out_specs=[row, row],
        out_shape=[jax.ShapeDtypeStruct((t, d), F32), jax.ShapeDtypeStruct((t, d), BF16)],
        scratch_shapes=[pltpu.VMEM((tm, d), F32), pltpu.VMEM((tm, 1), F32)],
        compiler_params=_params("parallel", "arbitrary"),
        name="ssd_out",
    )(yf, yb, xbc, z, d_exp, norm_g, w, x, g_post, g_next)


def _rope_tables(seq):
    rows = seq // GRID_W
    row = jnp.repeat(jnp.arange(rows, dtype=F32), GRID_W)
    col = jnp.tile(jnp.arange(GRID_W, dtype=F32), rows)
    axis_dim = HEAD_DIM // 2
    inv_freq = ROPE_THETA ** (-jnp.arange(0, axis_dim, 2, dtype=F32) / axis_dim)
    ang = jnp.concatenate([row[:, None] * inv_freq, col[:, None] * inv_freq], axis=-1)
    cos, sin = jnp.cos(ang), jnp.sin(ang)
    return jnp.concatenate([cos, cos], axis=-1), jnp.concatenate([-sin, sin], axis=-1)


def _pick(n, pref):
    t = min(pref, n)
    while n % t:
        t -= LANES
    return t


def kernel(x_prompt, x_sample, mix_pre_g, mix_post_g, ffn_pre_g, ffn_post_g, fa_w_in, fa_q_gain, fa_k_gain, fa_w_out, ssd_w_in, ssd_conv_w, ssd_conv_b, ssd_dt_bias, ssd_a_log, ssd_d, ssd_norm_g, ssd_w_out, ffn_w_up, ffn_conv_w, ffn_conv_b, ffn_w_down):
    assert x_prompt.shape[1:] == x_sample.shape[1:]
    batch = x_prompt.shape[0] + x_sample.shape[0]
    seq, d = x_prompt.shape[1:]
    t = batch * seq
    depth = mix_pre_g.shape[0]
    aw = N_Q_HEADS * HEAD_DIM
    kvw = N_KV_HEADS * HEAD_DIM
    fw = fa_w_in.shape[2] - aw - 2 * kvw
    assert fa_w_out.shape[1] == fw + aw and fw % FNET_GROUP_DIM == 0
    d_ff = ffn_w_down.shape[1]
    d_inner = ssd_w_out.shape[1]
    n_heads = d_inner // SSM_HEAD_DIM
    hpg = n_heads // SSM_N_GROUPS
    bc_w = SSM_N_GROUPS * SSM_D_STATE
    assert seq % 512 == 0 and hpg % 2 == 0 and 2 * hpg <= LANES

    tm = 512
    x = jnp.concatenate([x_prompt, x_sample], axis=0).reshape(t, d)
    vec = lambda v: v.reshape(1, -1).astype(F32)
    cosf, sinf = _rope_tables(seq)

    h = None
    for i in range(depth):
        j = i // 2
        g_ffn_pre = vec(ffn_pre_g[i])
        if i % 2 == 0:
            u, q, k, v = _inproj0(x, vec(mix_pre_g[i]), fa_w_in[j].astype(BF16), vec(fa_q_gain[j]),
                                  vec(fa_k_gain[j]), cosf, sinf, seq=seq, fw=fw, aw=aw, kvw=kvw, tm=tm)
            a = _attention(q.reshape(batch, seq, aw), k.reshape(batch, seq, kvw), v.reshape(batch, seq, kvw),
                           tq=_pick(seq, 1024), tk=_pick(seq, 1024)).reshape(t, aw)
            f = _fourier_mix(u, batch=batch, seq=seq, tn=_pick(128 * fw, 8192))
            wo = fa_w_out[j].astype(BF16)
            x, h = _proj_res([f, a], [wo[:fw], wo[fw:]], x, vec(mix_post_g[i]), g_ffn_pre, tm=tm, tk=fw)
        else:
            if h is None:
                raise NotImplementedError("odd first layer")
            w_in = ssd_w_in[j].astype(BF16)
            z = _conv_mm(h, w_in, None, None, act="none", seq=seq, tm=tm, tn=_pick(d_inner, 1024),
                         col0=0, ncols=d_inner)
            conv_dim = d_inner + 2 * bc_w
            xbc = _conv_mm(h, w_in, ssd_conv_w[j], vec(ssd_conv_b[j]), act="silu", seq=seq, tm=tm,
                           tn=_pick(conv_dim, 1024), col0=d_inner, ncols=conv_dim)
            w_dt = ssd_w_in[j][:, d_inner + conv_dim:].reshape(d, 2, SSM_N_GROUPS, hpg)
            w_dt = jnp.pad(w_dt.transpose(0, 2, 1, 3).reshape(d, SSM_N_GROUPS, 2 * hpg),
                           ((0, 0), (0, 0), (0, LANES - 2 * hpg))).reshape(d, SSM_N_GROUPS * LANES)

            def regroup(p):
                p = p.astype(F32).reshape(2, SSM_N_GROUPS, hpg).transpose(1, 0, 2).reshape(SSM_N_GROUPS, 2 * hpg)
                return jnp.pad(p, ((0, 0), (0, LANES - 2 * hpg))).reshape(1, SSM_N_GROUPS * LANES)

            dt = _dt_proj(h, w_dt.astype(BF16), regroup(ssd_dt_bias[j]), tm=tm)
            a_pad = regroup(-jnp.exp(ssd_a_log[j].astype(F32)))
            yf, yb = _ssd(xbc, dt, a_pad, batch=batch, seq=seq, d_inner=d_inner)
            d_exp = jnp.repeat(ssd_d[j].astype(F32), SSM_HEAD_DIM).reshape(1, d_inner)
            x, h = _ssd_out(yf, yb, xbc, z, d_exp, vec(ssd_norm_g[j]), ssd_w_out[j].astype(BF16), x,
                            vec(mix_post_g[i]), g_ffn_pre, tm=tm, tk=_pick(d_inner, 1024))
        act = _conv_mm(h, ffn_w_up[i].astype(BF16), ffn_conv_w[i], vec(ffn_conv_b[i]), act="glu", seq=seq,
                       tm=tm, tn=_pick(d_ff, 512), col0=0, ncols=d_ff)
        g_next = vec(mix_pre_g[i + 1]) if i + 1 < depth else None
        x, h = _proj_res([act], [ffn_w_down[i].astype(BF16)], x, vec(ffn_post_g[i]), g_next,
                         tm=tm, tk=_pick(d_ff, 512))
    y = x.reshape(batch, seq, d)
    return (y[:x_prompt.shape[0]], y[x_prompt.shape[0]:])
```

```python
import functools
import math

import jax
import jax.numpy as jnp
import numpy as np
from jax import lax
from jax.experimental import pallas as pl
from jax.experimental.pallas import tpu as pltpu

F32 = jnp.float32
BF16 = jnp.bfloat16

EPS = 1e-6
GRID_W = 64
HEAD_DIM = 128
N_Q_HEADS = 8
N_KV_HEADS = 2
FNET_GROUP_DIM = 128
ROPE_THETA = 10000.0
SSM_HEAD_DIM = 64
SSM_N_GROUPS = 8
SSM_D_STATE = 128
SSM_CHUNK = 128
LANES = 128
HALO = 16
VMEM_LIMIT = 56 * 1024 * 1024

_NT = (((1,), (1,)), ((), ()))


def _params(*sem):
    return pltpu.CompilerParams(dimension_semantics=sem, vmem_limit_bytes=VMEM_LIMIT)


def _resident(shape):
    nd = len(shape)
    return pl.BlockSpec(shape, lambda *_: (0,) * nd, pipeline_mode=pl.Buffered(1))


def _rms(x, g):
    return x * lax.rsqrt(jnp.mean(x * x, axis=-1, keepdims=True) + EPS) * g


def _dot(a, b):
    return jnp.dot(a, b, preferred_element_type=F32)


def _inproj0_kernel(x_ref, g_ref, w_ref, qg_ref, kg_ref, cos_ref, sin_ref,
                    u_ref, q_ref, k_ref, v_ref, *, fw, aw, kvw, scale):
    h = _rms(x_ref[...], g_ref[...]).astype(BF16)
    u_ref[...] = _dot(h, w_ref[:, :fw]).astype(BF16)
    cosf = cos_ref[...]
    sinf = sin_ref[...]

    def norm_rope(t, gain, mult):
        t = _rms(t, gain)
        t = t * cosf + pltpu.roll(t, HEAD_DIM // 2, 1) * sinf
        return (t * mult).astype(BF16)

    q = _dot(h, w_ref[:, fw:fw + aw])
    for hh in range(aw // HEAD_DIM):
        sl = slice(hh * HEAD_DIM, (hh + 1) * HEAD_DIM)
        q_ref[:, sl] = norm_rope(q[:, sl], qg_ref[...], scale)
    k = _dot(h, w_ref[:, fw + aw:fw + aw + kvw])
    for hh in range(kvw // HEAD_DIM):
        sl = slice(hh * HEAD_DIM, (hh + 1) * HEAD_DIM)
        k_ref[:, sl] = norm_rope(k[:, sl], kg_ref[...], 1.0)
    v_ref[...] = _dot(h, w_ref[:, fw + aw + kvw:]).astype(BF16)


def _inproj0(x, g, w, qg, kg, cosf, sinf, *, seq, fw, aw, kvw, tm):
    t, d = x.shape
    n = w.shape[1]
    tps = seq // tm
    kern = functools.partial(_inproj0_kernel, fw=fw, aw=aw, kvw=kvw, scale=HEAD_DIM ** -0.5)
    row = lambda width: pl.BlockSpec((tm, width), lambda i: (i, 0))
    return pl.pallas_call(
        kern,
        grid=(t // tm,),
        in_specs=[row(d), _resident((1, d)), _resident((d, n)),
                  _resident((1, HEAD_DIM)), _resident((1, HEAD_DIM)),
                  pl.BlockSpec((tm, HEAD_DIM), lambda i: (i % tps, 0)),
                  pl.BlockSpec((tm, HEAD_DIM), lambda i: (i % tps, 0))],
        out_specs=[row(fw), row(aw), row(kvw), row(kvw)],
        out_shape=[jax.ShapeDtypeStruct((t, fw), BF16), jax.ShapeDtypeStruct((t, aw), BF16),
                   jax.ShapeDtypeStruct((t, kvw), BF16), jax.ShapeDtypeStruct((t, kvw), BF16)],
        compiler_params=_params("parallel"),
        name="inproj0",
    )(x, g, w, qg, kg, cosf, sinf)


def _attn_kernel(q_ref, k_ref, v_ref, o_ref, m_sc, l_sc, acc_sc, *, group):
    ki = pl.program_id(3)

    @pl.when(ki == 0)
    def _():
        m_sc[...] = jnp.full(m_sc.shape, -jnp.inf, F32)
        l_sc[...] = jnp.zeros(l_sc.shape, F32)
        acc_sc[...] = jnp.zeros(acc_sc.shape, F32)

    k = k_ref[0]
    v = v_ref[0]
    for j in range(group):
        q = q_ref[0, :, j * HEAD_DIM:(j + 1) * HEAD_DIM]
        s = lax.dot_general(q, k, _NT, preferred_element_type=F32)
        m_prev = m_sc[j]
        m_new = jnp.maximum(m_prev, jnp.max(s, axis=-1, keepdims=True))
        alpha = jnp.exp(m_prev - m_new)
        p = jnp.exp(s - m_new)
        l_sc[j] = alpha * l_sc[j] + jnp.sum(p, axis=-1, keepdims=True)
        acc_sc[j] = alpha * acc_sc[j] + _dot(p.astype(BF16), v)
        m_sc[j] = m_new

    @pl.when(ki == pl.num_programs(3) - 1)
    def _():
        for j in range(group):
            o_ref[0, :, j * HEAD_DIM:(j + 1) * HEAD_DIM] = (acc_sc[j] / l_sc[j]).astype(BF16)


def _attention(q, k, v, *, tq, tk):
    b, s, aw = q.shape
    group = aw // HEAD_DIM // N_KV_HEADS
    gw = group * HEAD_DIM
    return pl.pallas_call(
        functools.partial(_attn_kernel, group=group),
        grid=(b, N_KV_HEADS, s // tq, s // tk),
        in_specs=[pl.BlockSpec((1, tq, gw), lambda bi, h, qi, ki: (bi, qi, h)),
                  pl.BlockSpec((1, tk, HEAD_DIM), lambda bi, h, qi, ki: (bi, ki, h)),
                  pl.BlockSpec((1, tk, HEAD_DIM), lambda bi, h, qi, ki: (bi, ki, h))],
        out_specs=pl.BlockSpec((1, tq, gw), lambda bi, h, qi, ki: (bi, qi, h)),
        out_shape=jax.ShapeDtypeStruct((b, s, aw), BF16),
        scratch_shapes=[pltpu.VMEM((group, tq, 1), F32), pltpu.VMEM((group, tq, 1), F32),
                        pltpu.VMEM((group, tq, HEAD_DIM), F32)],
        compiler_params=_params("parallel", "parallel", "parallel", "arbitrary"),
        name="attention",
    )(q, k, v)


def _fourier1_kernel(f_ref, u_ref, y_ref, *, n1):
    y = _dot(f_ref[...], u_ref[0])
    y_ref[0, 0] = y[:n1].astype(BF16)
    y_ref[0, 1] = y[n1:].astype(BF16)


def _fourier2_kernel(m_ref, y_ref, cc_ref, sc_ref, o_ref, *, n2, groups):
    y = y_ref[0].reshape(2 * n2, groups * FNET_GROUP_DIM)
    g = _dot(m_ref[0], y)
    gr = g[:n2].astype(BF16)
    gi = g[n2:].astype(BF16)
    for c in range(groups):
        sl = slice(c * FNET_GROUP_DIM, (c + 1) * FNET_GROUP_DIM)
        o_ref[0, :, sl] = (_dot(gr[:, sl], cc_ref[...]) + _dot(gi[:, sl], sc_ref[...])).astype(BF16)


def _fourier_tables(seq):
    n2 = 128
    n1 = seq // n2
    a1 = 2.0 * np.pi * np.outer(np.arange(n1), np.arange(n1)) / n1
    f1 = np.concatenate([np.cos(a1), -np.sin(a1)], axis=0)
    kk = np.arange(n1)[:, None, None] + n1 * np.arange(n2)[None, :, None]
    th = 2.0 * np.pi * (kk * np.arange(n2)[None, None, :] % seq) / seq
    mc, ms = np.cos(th), np.sin(th)
    m = np.concatenate([np.concatenate([mc, ms], axis=2),
                        np.concatenate([-ms, mc], axis=2)], axis=1)
    ac = 2.0 * np.pi * np.outer(np.arange(FNET_GROUP_DIM), np.arange(FNET_GROUP_DIM)) / FNET_GROUP_DIM
    norm = 1.0 / math.sqrt(seq * FNET_GROUP_DIM)
    return (jnp.asarray(f1, BF16), jnp.asarray(m, BF16),
            jnp.asarray(np.cos(ac) * norm, BF16), jnp.asarray(np.sin(ac) * norm, BF16))


def _fourier_mix(u, *, batch, seq, tn):
    t, width = u.shape
    n2 = 128
    n1 = seq // n2
    groups = width // FNET_GROUP_DIM
    f1, m, cc, sc = _fourier_tables(seq)
    cols = n2 * width
    y = pl.pallas_call(
        functools.partial(_fourier1_kernel, n1=n1),
        grid=(batch, cols // tn),
        in_specs=[_resident((2 * n1, n1)),
                  pl.BlockSpec((1, n1, tn), lambda b, j: (b, 0, j))],
        out_specs=pl.BlockSpec((1, 2, n1, tn), lambda b, j: (b, 0, 0, j)),
        out_shape=jax.ShapeDtypeStruct((batch, 2, n1, cols), BF16),
        compiler_params=_params("parallel", "parallel"),
        name="fourier_seq1",
    )(f1, u.reshape(batch, n1, cols))
    y = y.reshape(batch, 2, seq, width)
    out = pl.pallas_call(
        functools.partial(_fourier2_kernel, n2=n2, groups=groups),
        grid=(n1, batch),
        in_specs=[pl.BlockSpec((1, 2 * n2, 2 * n2), lambda k1, b: (k1, 0, 0)),
                  pl.BlockSpec((1, 2, n2, width), lambda k1, b: (b, 0, k1, 0)),
                  _resident((FNET_GROUP_DIM, FNET_GROUP_DIM)),
                  _resident((FNET_GROUP_DIM, FNET_GROUP_DIM))],
        out_specs=pl.BlockSpec((1, n2, width), lambda k1, b: (b, 0, k1)),
        out_shape=jax.ShapeDtypeStruct((batch, n2, n1 * width), BF16),
        compiler_params=_params("parallel", "parallel"),
        name="fourier_seq2",
    )(m, y, cc, sc)
    return out.reshape(t, width)


def _proj_res_kernel(*refs, n_lhs, has_next):
    lhs = refs[:n_lhs]
    ws = refs[n_lhs:2 * n_lhs]
    x_ref, gp_ref = refs[2 * n_lhs:2 * n_lhs + 2]
    pos = 2 * n_lhs + 2
    gn_ref = refs[pos] if has_next else None
    pos += int(has_next)
    xo_ref = refs[pos]
    ho_ref = refs[pos + 1] if has_next else None
    acc = refs[-1]
    kk = pl.program_id(1)

    @pl.when(kk == 0)
    def _():
        acc[...] = jnp.zeros(acc.shape, F32)

    part = _dot(lhs[0][...], ws[0][...])
    for a, w in zip(lhs[1:], ws[1:]):
        part += _dot(a[...], w[...])
    acc[...] += part

    @pl.when(kk == pl.num_programs(1) - 1)
    def _():
        xn = x_ref[...] + _rms(acc[...], gp_ref[...])
        xo_ref[...] = xn
        if has_next:
            ho_ref[...] = _rms(xn, gn_ref[...]).astype(BF16)


def _proj_res(lhs, ws, x, g_post, g_next, *, tm, tk):
    t, d = x.shape
    n_lhs = len(lhs)
    kdim = lhs[0].shape[1]
    has_next = g_next is not None
    in_specs = [pl.BlockSpec((tm, tk), lambda i, k: (i, k)) for _ in lhs]
    in_specs += [pl.BlockSpec((tk, d), lambda i, k: (k, 0)) for _ in ws]
    in_specs += [pl.BlockSpec((tm, d), lambda i, k: (i, 0)), _resident((1, d))]
    args = list(lhs) + list(ws) + [x, g_post]
    out_specs = [pl.BlockSpec((tm, d), lambda i, k: (i, 0))]
    out_shape = [jax.ShapeDtypeStruct((t, d), F32)]
    if has_next:
        in_specs.append(_resident((1, d)))
        args.append(g_next)
        out_specs.append(pl.BlockSpec((tm, d), lambda i, k: (i, 0)))
        out_shape.append(jax.ShapeDtypeStruct((t, d), BF16))
    res = pl.pallas_call(
        functools.partial(_proj_res_kernel, n_lhs=n_lhs, has_next=has_next),
        grid=(t // tm, kdim // tk),
        in_specs=in_specs, out_specs=out_specs, out_shape=out_shape,
        scratch_shapes=[pltpu.VMEM((tm, d), F32)],
        compiler_params=_params("parallel", "arbitrary"),
        name="proj_res",
    )(*args)
    return res if has_next else (res[0], None)


def _gelu_tanh(x):
    return 0.5 * x * (1.0 + jnp.tanh(math.sqrt(2.0 / math.pi) * (x + 0.044715 * x * x * x)))


def _silu(x):
    return x * (1.0 / (1.0 + jnp.exp(-x)))


def _assemble_halo(hext, h_ref, hp_ref, hn_ref, *, tm, tps):
    i = pl.program_id(0)
    first = (i % tps) == 0
    last = (i % tps) == tps - 1
    hext[0:HALO] = jnp.where(first, jnp.zeros_like(hp_ref[...]), hp_ref[...])
    hext[HALO:HALO + tm] = h_ref[...]
    hext[HALO + tm:] = jnp.where(last, jnp.zeros_like(hn_ref[...]), hn_ref[...])


def _conv_rows(u, cw, cb, *, tm):
    rows = tm + 2 * HALO
    c = cb + cw[0:1] * pltpu.roll(u, 1, 0) + cw[1:2] * u + cw[2:3] * pltpu.roll(u, rows - 1, 0)
    return c[HALO:HALO + tm]


def _halo_specs(t, tm, kdim):
    hb = tm // HALO
    last_blk = t // HALO - 1
    return [pl.BlockSpec((tm, kdim), lambda i, j: (i, 0)),
            pl.BlockSpec((HALO, kdim), lambda i, j: (jnp.maximum(i * hb - 1, 0), 0)),
            pl.BlockSpec((HALO, kdim), lambda i, j: (jnp.minimum((i + 1) * hb, last_blk), 0))]


def _conv_mm_kernel(*refs, act, tm, tps):
    if act == "none":
        h_ref, w_ref, o_ref = refs
        o_ref[...] = _dot(h_ref[...], w_ref[...]).astype(o_ref.dtype)
        return
    h_ref, hp_ref, hn_ref, w_ref, cw_ref, cb_ref, o_ref, hext = refs

    @pl.when(pl.program_id(1) == 0)
    def _():
        _assemble_halo(hext, h_ref, hp_ref, hn_ref, tm=tm, tps=tps)

    u = _dot(hext[...], w_ref[...])
    o_ref[...] = _silu(_conv_rows(u, cw_ref[...], cb_ref[...], tm=tm)).astype(o_ref.dtype)


def _conv_mm(h, w, conv_w, conv_b, *, act, seq, tm, tn, col0, ncols):
    t, kdim = h.shape
    nj = ncols // tn
    j0 = col0 // tn
    if act == "none":
        in_specs = [pl.BlockSpec((tm, kdim), lambda i, j: (i, 0)),
                    pl.BlockSpec((kdim, tn), lambda i, j: (0, j0 + j))]
        args = [h, w]
        scratch = []
    else:
        in_specs = _halo_specs(t, tm, kdim) + [
            pl.BlockSpec((kdim, tn), lambda i, j: (0, j0 + j)),
            pl.BlockSpec((3, tn), lambda i, j: (0, j)),
            pl.BlockSpec((1, tn), lambda i, j: (0, j))]
        args = [h, h, h, w, conv_w, conv_b]
        scratch = [pltpu.VMEM((tm + 2 * HALO, kdim), BF16)]
    return pl.pallas_call(
        functools.partial(_conv_mm_kernel, act=act, tm=tm, tps=seq // tm),
        grid=(t // tm, nj),
        in_specs=in_specs,
        out_specs=pl.BlockSpec((tm, tn), lambda i, j: (i, j)),
        out_shape=jax.ShapeDtypeStruct((t, ncols), BF16),
        scratch_shapes=scratch,
        compiler_params=_params("parallel", "arbitrary"),
        name="conv_mm_" + act,
    )(*args)


def _ffn_kernel(*refs, tm, tps, nj, has_next):
    (h_ref, hp_ref, hn_ref, wg_ref, wv_ref, cwg_ref, cwv_ref, cbg_ref, cbv_ref,
     wd_ref, wdl_ref, x_ref, gp_ref) = refs[:13]
    pos = 13
    gn_ref = refs[pos] if has_next else None
    pos += int(has_next)
    xo_ref = refs[pos]
    ho_ref = refs[pos + 1] if has_next else None
    hext, act_a, act_b, acc = refs[-4:]
    j = pl.program_id(1)

    @pl.when(j == 0)
    def _():
        _assemble_halo(hext, h_ref, hp_ref, hn_ref, tm=tm, tps=tps)
        act_b[...] = jnp.zeros(act_b.shape, BF16)
        acc[...] = jnp.zeros(acc.shape, F32)

    def step(act_prev, act_cur):
        hx = hext[...]
        ug = _dot(hx, wg_ref[...])
        uv = _dot(hx, wv_ref[...])
        acc[...] += _dot(act_prev[...], wd_ref[...])
        gate = _conv_rows(ug, cwg_ref[...], cbg_ref[...], tm=tm)
        val = _conv_rows(uv, cwv_ref[...], cbv_ref[...], tm=tm)
        act_cur[...] = (_gelu_tanh(gate) * val).astype(BF16)

    @pl.when(j % 2 == 0)
    def _():
        step(act_b, act_a)

    @pl.when(j % 2 == 1)
    def _():
        step(act_a, act_b)

    @pl.when(j == nj - 1)
    def _():
        act_last = act_a if (nj - 1) % 2 == 0 else act_b
        m = acc[...] + _dot(act_last[...], wdl_ref[...])
        xn = x_ref[...] + _rms(m, gp_ref[...])
        xo_ref[...] = xn
        if has_next:
            ho_ref[...] = _rms(xn, gn_ref[...]).astype(BF16)


def _ffn(h, w_up, conv_w, conv_b, w_down, x, g_post, g_next, *, seq, tm, tn):
    t, d = x.shape
    d_ff = w_down.shape[0]
    nj = d_ff // tn
    has_next = g_next is not None
    row = pl.BlockSpec((tm, d), lambda i, j: (i, 0))
    in_specs = _halo_specs(t, tm, d) + [
        pl.BlockSpec((d, tn), lambda i, j: (0, j)),
        pl.BlockSpec((d, tn), lambda i, j: (0, nj + j)),
        pl.BlockSpec((3, tn), lambda i, j: (0, j)),
        pl.BlockSpec((3, tn), lambda i, j: (0, nj + j)),
        pl.BlockSpec((1, tn), lambda i, j: (0, j)),
        pl.BlockSpec((1, tn), lambda i, j: (0, nj + j)),
        pl.BlockSpec((tn, d), lambda i, j: (jnp.maximum(j - 1, 0), 0)),
        pl.BlockSpec((tn, d), lambda i, j: (nj - 1, 0), pipeline_mode=pl.Buffered(1)),
        row, _resident((1, d))]
    args = [h, h, h, w_up, w_up, conv_w, conv_w, conv_b, conv_b, w_down, w_down, x, g_post]
    out_specs = [row]
    out_shape = [jax.ShapeDtypeStruct((t, d), F32)]
    if has_next:
        in_specs.append(_resident((1, d)))
        args.append(g_next)
        out_specs.append(row)
        out_shape.append(jax.ShapeDtypeStruct((t, d), BF16))
    res = pl.pallas_call(
        functools.partial(_ffn_kernel, tm=tm, tps=seq // tm, nj=nj, has_next=has_next),
        grid=(t // tm, nj),
        in_specs=in_specs, out_specs=out_specs, out_shape=out_shape,
        scratch_shapes=[pltpu.VMEM((tm + 2 * HALO, d), BF16), pltpu.VMEM((tm, tn), BF16),
                        pltpu.VMEM((tm, tn), BF16), pltpu.VMEM((tm, d), F32)],
        compiler_params=_params("parallel", "arbitrary"),
        name="ffn",
    )(*args)
    return res if has_next else (res[0], None)


def _dt_kernel(h_ref, w_ref, b_ref, o_ref):
    r = _dot(h_ref[...], w_ref[...]) + b_ref[...]
    o_ref[...] = jnp.maximum(r, 0.0) + jnp.log(1.0 + jnp.exp(-jnp.abs(r)))


def _dt_proj(h, w, b, *, tm):
    t, kdim = h.shape
    n = w.shape[1]
    return pl.pallas_call(
        _dt_kernel,
        grid=(t // tm,),
        in_specs=[pl.BlockSpec((tm, kdim), lambda i: (i, 0)), _resident((kdim, n)), _resident((1, n))],
        out_specs=pl.BlockSpec((tm, n), lambda i: (i, 0)),
        out_shape=jax.ShapeDtypeStruct((t, n), F32),
        compiler_params=_params("parallel"),
        name="ssd_dt",
    )(h, w, b)


def _split3(x):
    hi = x.astype(BF16)
    r = x - hi.astype(F32)
    mid = r.astype(BF16)
    lo = (r - mid.astype(F32)).astype(BF16)
    return hi, mid, lo


def _cumsum_rows(tri, x):
    hi, mid, lo = _split3(x)
    return _dot(tri, hi) + _dot(tri, mid) + _dot(tri, lo)


def _expand(x, e):
    hi = x.astype(BF16)
    lo = (x - hi.astype(F32)).astype(BF16)
    return _dot(hi, e) + _dot(lo, e)


def _ssd_direction(x_ref, b_ref, c_ref, dt_ref, a_ref, tri_ref, e_ref, h_sc, y_ref, *, lane0, hpg, backward):
    L = SSM_CHUNK
    x = x_ref[...]
    bm = b_ref[...]
    cm = c_ref[...]
    dt = dt_ref[...]
    dta = dt * a_ref[...]
    incl = _cumsum_rows(tri_ref[...], dta)
    total = incl[L - 1:L, :]
    if backward:
        pos = incl - dta
        w_state = dt * jnp.exp(pos)
        w_out = jnp.exp(total - pos)
    else:
        pos = incl
        w_state = dt * jnp.exp(total - pos)
        w_out = jnp.exp(pos)
    e = e_ref[...]
    w_state_x = _expand(w_state, e)
    w_out_x = _expand(w_out, e)
    carry = (w_out_x[0:1] if backward else w_out_x[L - 1:L])
    pos_t = pos.T
    dt_t = dt.T

    h_in = h_sc[...]
    y = _dot(cm, h_in.astype(BF16)) * w_out_x
    xs = (x.astype(F32) * w_state_x).astype(BF16)
    b_t = bm.astype(F32).T.astype(BF16)
    h_sc[...] = h_in * carry + _dot(b_t, xs)

    cb = lax.dot_general(cm, bm, _NT, preferred_element_type=F32)
    li = lax.broadcasted_iota(jnp.int32, (L, L), 0)
    si = lax.broadcasted_iota(jnp.int32, (L, L), 1)
    mask = (si >= li) if backward else (li >= si)
    lane = lax.broadcasted_iota(jnp.int32, (L, 2 * SSM_HEAD_DIM), 1)
    for jp in range(hpg // 2):
        xp = x[:, jp * 2 * SSM_HEAD_DIM:(jp + 1) * 2 * SSM_HEAD_DIM]
        outs = []
        for j in (2 * jp, 2 * jp + 1):
            col = pos[:, lane0 + j:lane0 + j + 1]
            row = pos_t[lane0 + j:lane0 + j + 1, :]
            d = (row - col) if backward else (col - row)
            w = cb * jnp.exp(jnp.where(mask, d, -1e30)) * dt_t[lane0 + j:lane0 + j + 1, :]
            outs.append(_dot(w.astype(BF16), xp))
        sl = slice(jp * 2 * SSM_HEAD_DIM, (jp + 1) * 2 * SSM_HEAD_DIM)
        y_ref[:, sl] = (y[:, sl] + jnp.where(lane < SSM_HEAD_DIM, outs[0], outs[1])).astype(y_ref.dtype)


def _ssd_kernel(xf, bf, cf, dtf, xb, bb, cb, dtb, a_ref, tri_ref, ef_ref, eb_ref,
                yf_ref, yb_ref, hf_sc, hb_sc, *, hpg):
    @pl.when(pl.program_id(2) == 0)
    def _():
        hf_sc[...] = jnp.zeros(hf_sc.shape, F32)
        hb_sc[...] = jnp.zeros(hb_sc.shape, F32)

    _ssd_direction(xf, bf, cf, dtf, a_ref, tri_ref, ef_ref, hf_sc, yf_ref, lane0=0, hpg=hpg, backward=False)
    _ssd_direction(xb, bb, cb, dtb, a_ref, tri_ref, eb_ref, hb_sc, yb_ref, lane0=hpg, hpg=hpg, backward=True)


def _ssd(xbc, dt, a_pad, *, batch, seq, d_inner):
    t = xbc.shape[0]
    L = SSM_CHUNK
    nc = seq // L
    gw = d_inner // SSM_N_GROUPS
    hpg = gw // SSM_HEAD_DIM
    xblk = d_inner // gw
    b0 = d_inner // SSM_D_STATE
    c0 = b0 + SSM_N_GROUPS
    tri = jnp.asarray(np.tril(np.ones((L, L))), BF16)
    head_of_lane = np.arange(gw) // SSM_HEAD_DIM
    ef = jnp.asarray(np.arange(LANES)[:, None] == head_of_lane[None, :], BF16)
    eb = jnp.asarray(np.arange(LANES)[:, None] == (head_of_lane[None, :] + hpg), BF16)
    del xblk

    def fwd(col):
        return lambda b, g, c: (b * nc + c, col(g))

    def bwd(col):
        return lambda b, g, c: (b * nc + (nc - 1 - c), col(g))

    def specs(order):
        return [pl.BlockSpec((L, gw), order(lambda g: g)),
                pl.BlockSpec((L, SSM_D_STATE), order(lambda g: b0 + g)),
                pl.BlockSpec((L, SSM_D_STATE), order(lambda g: c0 + g)),
                pl.BlockSpec((L, LANES), order(lambda g: g))]

    in_specs = specs(fwd) + specs(bwd) + [
        pl.BlockSpec((1, LANES), lambda b, g, c: (0, g)),
        _resident((L, L)), _resident((LANES, gw)), _resident((LANES, gw))]
    return pl.pallas_call(
        functools.partial(_ssd_kernel, hpg=hpg),
        grid=(batch, SSM_N_GROUPS, nc),
        in_specs=in_specs,
        out_specs=[pl.BlockSpec((L, gw), fwd(lambda g: g)), pl.BlockSpec((L, gw), bwd(lambda g: g))],
        out_shape=[jax.ShapeDtypeStruct((t, d_inner), BF16), jax.ShapeDtypeStruct((t, d_inner), BF16)],
        scratch_shapes=[pltpu.VMEM((SSM_D_STATE, gw), F32), pltpu.VMEM((SSM_D_STATE, gw), F32)],
        compiler_params=_params("parallel", "parallel", "arbitrary"),
        name="ssd_scan",
    )(xbc, xbc, xbc, dt, xbc, xbc, xbc, dt, a_pad, tri, ef, eb)


def _ssd_out_kernel(yf_ref, yb_ref, xs_ref, z_ref, dsk_ref, ng_ref, w_ref, x_ref, gp_ref, gn_ref,
                    xo_ref, ho_ref, acc, ss, *, d_inner):
    kk = pl.program_id(1)

    @pl.when(kk == 0)
    def _():
        acc[...] = jnp.zeros(acc.shape, F32)
        ss[...] = jnp.zeros(ss.shape, F32)

    y = yf_ref[...].astype(F32) + yb_ref[...].astype(F32) + xs_ref[...].astype(F32) * dsk_ref[...]
    y = y * _silu(z_ref[...].astype(F32))
    ss[...] += jnp.sum(y * y, axis=-1, keepdims=True)
    acc[...] += _dot((y * ng_ref[...]).astype(BF16), w_ref[...])

    @pl.when(kk == pl.num_programs(1) - 1)
    def _():
        m = acc[...] * lax.rsqrt(ss[...] * (1.0 / d_inner) + EPS)
        xn = x_ref[...] + _rms(m, gp_ref[...])
        xo_ref[...] = xn
        ho_ref[...] = _rms(xn, gn_ref[...]).astype(BF16)


def _ssd_out(yf, yb, xbc, z, d_exp, norm_g, w, x, g_post, g_next, *, tm, tk):
    t, d = x.shape
    d_inner = yf.shape[1]
    lhs = pl.BlockSpec((tm, tk), lambda i, k: (i, k))
    vec = pl.BlockSpec((1, tk), lambda i, k: (0, k))
    row = pl.BlockSpec((tm, d), lambda i, k: (i, 0))
    return pl.pallas_call(
        functools.partial(_ssd_out_kernel, d_inner=d_inner),
        grid=(t // tm, d_inner // tk),
        in_specs=[lhs, lhs, lhs, lhs, vec, vec, pl.BlockSpec((tk, d), lambda i, k: (k, 0)),
                  row, _resident((1, d)), _resident((1, d))],
        out_specs=[row, row],
        out_shape=[jax.ShapeDtypeStruct((t, d), F32), jax.ShapeDtypeStruct((t, d), BF16)],
        scratch_shapes=[pltpu.VMEM((tm, d), F32), pltpu.VMEM((tm, 1), F32)],
        compiler_params=_params("parallel", "arbitrary"),
        name="ssd_out",
    )(yf, yb, xbc, z, d_exp, norm_g, w, x, g_post, g_next)


def _rope_tables(seq):
    rows = seq // GRID_W
    row = jnp.repeat(jnp.arange(rows, dtype=F32), GRID_W)
    col = jnp.tile(jnp.arange(GRID_W, dtype=F32), rows)
    axis_dim = HEAD_DIM // 2
    inv_freq = ROPE_THETA ** (-jnp.arange(0, axis_dim, 2, dtype=F32) / axis_dim)
    ang = jnp.concatenate([row[:, None] * inv_freq, col[:, None] * inv_freq], axis=-1)
    cos, sin = jnp.cos(ang), jnp.sin(ang)
    return jnp.concatenate([cos, cos], axis=-1), jnp.concatenate([-sin, sin], axis=-1)


def _pick(n, pref):
    t = min(pref, n)
    while n % t:
        t -= LANES
    return t


def kernel(x_prompt, x_sample, mix_pre_g, mix_post_g, ffn_pre_g, ffn_post_g, fa_w_in, fa_q_gain, fa_k_gain, fa_w_out, ssd_w_in, ssd_conv_w, ssd_conv_b, ssd_dt_bias, ssd_a_log, ssd_d, ssd_norm_g, ssd_w_out, ffn_w_up, ffn_conv_w, ffn_conv_b, ffn_w_down):
    depth = mix_pre_g.shape[0]
    d = x_prompt.shape[2]
    aw = N_Q_HEADS * HEAD_DIM
    kvw = N_KV_HEADS * HEAD_DIM
    fw = fa_w_in.shape[2] - aw - 2 * kvw
    assert fa_w_out.shape[1] == fw + aw and fw % FNET_GROUP_DIM == 0
    d_ff = ffn_w_down.shape[1]
    d_inner = ssd_w_out.shape[1]
    n_heads = d_inner // SSM_HEAD_DIM
    hpg = n_heads // SSM_N_GROUPS
    bc_w = SSM_N_GROUPS * SSM_D_STATE
    conv_dim = d_inner + 2 * bc_w
    assert hpg % 2 == 0 and 2 * hpg <= LANES and depth % 2 == 0
    tm = 512
    vec = lambda v: v.reshape(1, -1).astype(F32)

    def regroup(p):
        p = p.astype(F32).reshape(2, SSM_N_GROUPS, hpg).transpose(1, 0, 2).reshape(SSM_N_GROUPS, 2 * hpg)
        return jnp.pad(p, ((0, 0), (0, LANES - 2 * hpg))).reshape(1, SSM_N_GROUPS * LANES)

    layers = []
    for i in range(depth):
        j = i // 2
        lw = dict(ffn_pre=vec(ffn_pre_g[i]), ffn_post=vec(ffn_post_g[i]), mix_pre=vec(mix_pre_g[i]),
                  mix_post=vec(mix_post_g[i]), w_up=ffn_w_up[i].astype(BF16), w_down=ffn_w_down[i].astype(BF16),
                  ffn_cw=ffn_conv_w[i].astype(F32), ffn_cb=vec(ffn_conv_b[i]))
        if i % 2 == 0:
            wo = fa_w_out[j].astype(BF16)
            lw.update(w_in=fa_w_in[j].astype(BF16), qg=vec(fa_q_gain[j]), kg=vec(fa_k_gain[j]),
                      wo_f=wo[:fw], wo_a=wo[fw:])
        else:
            w_dt = ssd_w_in[j][:, d_inner + conv_dim:].reshape(d, 2, SSM_N_GROUPS, hpg)
            w_dt = jnp.pad(w_dt.transpose(0, 2, 1, 3).reshape(d, SSM_N_GROUPS, 2 * hpg),
                           ((0, 0), (0, 0), (0, LANES - 2 * hpg))).reshape(d, SSM_N_GROUPS * LANES)
            lw.update(w_in=ssd_w_in[j].astype(BF16), w_dt=w_dt.astype(BF16), dt_bias=regroup(ssd_dt_bias[j]),
                      a_pad=regroup(-jnp.exp(ssd_a_log[j].astype(F32))), cw=ssd_conv_w[j].astype(F32),
                      cb=vec(ssd_conv_b[j]), d_exp=jnp.repeat(ssd_d[j].astype(F32), SSM_HEAD_DIM).reshape(1, d_inner),
                      norm_g=vec(ssd_norm_g[j]), w_out=ssd_w_out[j].astype(BF16))
        layers.append(lw)

    def trunk(xin):
        batch, seq, _ = xin.shape
        assert seq % tm == 0
        t = batch * seq
        x = xin.reshape(t, d)
        cosf, sinf = _rope_tables(seq)
        h = None
        for i, lw in enumerate(layers):
            if i % 2 == 0:
                u, q, k, v = _inproj0(x, lw["mix_pre"], lw["w_in"], lw["qg"], lw["kg"], cosf, sinf,
                                      seq=seq, fw=fw, aw=aw, kvw=kvw, tm=tm)
                a = _attention(q.reshape(batch, seq, aw), k.reshape(batch, seq, kvw), v.reshape(batch, seq, kvw),
                               tq=_pick(seq, 1024), tk=_pick(seq, 1024)).reshape(t, aw)
                f = _fourier_mix(u, batch=batch, seq=seq, tn=_pick(128 * fw, 8192))
                x, h = _proj_res([f, a], [lw["wo_f"], lw["wo_a"]], x, lw["mix_post"], lw["ffn_pre"], tm=tm, tk=fw)
            else:
                z = _conv_mm(h, lw["w_in"], None, None, act="none", seq=seq, tm=tm, tn=_pick(d_inner, 1024),
                             col0=0, ncols=d_inner)
                xbc = _conv_mm(h, lw["w_in"], lw["cw"], lw["cb"], act="silu", seq=seq, tm=tm,
                               tn=_pick(conv_dim, 1024), col0=d_inner, ncols=conv_dim)
                dt = _dt_proj(h, lw["w_dt"], lw["dt_bias"], tm=tm)
                yf, yb = _ssd(xbc, dt, lw["a_pad"], batch=batch, seq=seq, d_inner=d_inner)
                x, h = _ssd_out(yf, yb, xbc, z, lw["d_exp"], lw["norm_g"], lw["w_out"], x,
                                lw["mix_post"], lw["ffn_pre"], tm=tm, tk=_pick(d_inner, 1024))
            g_next = layers[i + 1]["mix_pre"] if i + 1 < depth else None
            x, h = _ffn(h, lw["w_up"], lw["ffn_cw"], lw["ffn_cb"], lw["w_down"], x, lw["ffn_post"], g_next,
                        seq=seq, tm=tm, tn=_pick(d_ff, 512))
        return x.reshape(batch, seq, d)

    return (trunk(x_prompt), trunk(x_sample))
```

```python
import functools
import math

import jax
import jax.numpy as jnp
import numpy as np
from jax import lax
from jax.experimental import pallas as pl
from jax.experimental.pallas import tpu as pltpu

F32 = jnp.float32
BF16 = jnp.bfloat16

EPS = 1e-6
GRID_W = 64
HEAD_DIM = 128
N_Q_HEADS = 8
N_KV_HEADS = 2
FNET_GROUP_DIM = 128
ROPE_THETA = 10000.0
SSM_HEAD_DIM = 64
SSM_N_GROUPS = 8
SSM_D_STATE = 128
SSM_CHUNK = 128
LANES = 128
HALO = 16
VMEM_LIMIT = 56 * 1024 * 1024

_NT = (((1,), (1,)), ((), ()))


def _params(*sem):
    return pltpu.CompilerParams(dimension_semantics=sem, vmem_limit_bytes=VMEM_LIMIT)


def _resident(shape):
    nd = len(shape)
    return pl.BlockSpec(shape, lambda *_: (0,) * nd, pipeline_mode=pl.Buffered(1))


def _rms(x, g):
    return x * lax.rsqrt(jnp.mean(x * x, axis=-1, keepdims=True) + EPS) * g


def _dot(a, b):
    return jnp.dot(a, b, preferred_element_type=F32)


def _inproj0_kernel(x_ref, g_ref, w_ref, qg_ref, kg_ref, cos_ref, sin_ref,
                    u_ref, q_ref, k_ref, v_ref, *, fw, aw, kvw, scale):
    h = _rms(x_ref[...], g_ref[...]).astype(BF16)
    u_ref[...] = _dot(h, w_ref[:, :fw]).astype(BF16)
    cosf = cos_ref[...]
    sinf = sin_ref[...]

    def norm_rope(t, gain, mult):
        t = _rms(t, gain)
        t = t * cosf + pltpu.roll(t, HEAD_DIM // 2, 1) * sinf
        return (t * mult).astype(BF16)

    q = _dot(h, w_ref[:, fw:fw + aw])
    for hh in range(aw // HEAD_DIM):
        sl = slice(hh * HEAD_DIM, (hh + 1) * HEAD_DIM)
        q_ref[:, sl] = norm_rope(q[:, sl], qg_ref[...], scale)
    k = _dot(h, w_ref[:, fw + aw:fw + aw + kvw])
    for hh in range(kvw // HEAD_DIM):
        sl = slice(hh * HEAD_DIM, (hh + 1) * HEAD_DIM)
        k_ref[:, sl] = norm_rope(k[:, sl], kg_ref[...], 1.0)
    v_ref[...] = _dot(h, w_ref[:, fw + aw + kvw:]).astype(BF16)


def _inproj0(x, g, w, qg, kg, cosf, sinf, *, seq, fw, aw, kvw, tm):
    t, d = x.shape
    n = w.shape[1]
    tps = seq // tm
    kern = functools.partial(_inproj0_kernel, fw=fw, aw=aw, kvw=kvw, scale=HEAD_DIM ** -0.5)
    row = lambda width: pl.BlockSpec((tm, width), lambda i: (i, 0))
    return pl.pallas_call(
        kern,
        grid=(t // tm,),
        in_specs=[row(d), _resident((1, d)), _resident((d, n)),
                  _resident((1, HEAD_DIM)), _resident((1, HEAD_DIM)),
                  pl.BlockSpec((tm, HEAD_DIM), lambda i: (i % tps, 0)),
                  pl.BlockSpec((tm, HEAD_DIM), lambda i: (i % tps, 0))],
        out_specs=[row(fw), row(aw), row(kvw), row(kvw)],
        out_shape=[jax.ShapeDtypeStruct((t, fw), BF16), jax.ShapeDtypeStruct((t, aw), BF16),
                   jax.ShapeDtypeStruct((t, kvw), BF16), jax.ShapeDtypeStruct((t, kvw), BF16)],
        compiler_params=_params("parallel"),
        name="inproj0",
    )(x, g, w, qg, kg, cosf, sinf)


def _attn_kernel(q_ref, k_ref, v_ref, o_ref, m_sc, l_sc, acc_sc, *, group):
    ki = pl.program_id(3)

    @pl.when(ki == 0)
    def _():
        m_sc[...] = jnp.full(m_sc.shape, -jnp.inf, F32)
        l_sc[...] = jnp.zeros(l_sc.shape, F32)
        acc_sc[...] = jnp.zeros(acc_sc.shape, F32)

    k = k_ref[0]
    v = v_ref[0]
    for j in range(group):
        q = q_ref[0, :, j * HEAD_DIM:(j + 1) * HEAD_DIM]
        s = lax.dot_general(q, k, _NT, preferred_element_type=F32)
        m_prev = m_sc[j]
        m_new = jnp.maximum(m_prev, jnp.max(s, axis=-1, keepdims=True))
        alpha = jnp.exp(m_prev - m_new)
        p = jnp.exp(s - m_new)
        l_sc[j] = alpha * l_sc[j] + jnp.sum(p, axis=-1, keepdims=True)
        acc_sc[j] = alpha * acc_sc[j] + _dot(p.astype(BF16), v)
        m_sc[j] = m_new

    @pl.when(ki == pl.num_programs(3) - 1)
    def _():
        for j in range(group):
            o_ref[0, :, j * HEAD_DIM:(j + 1) * HEAD_DIM] = (acc_sc[j] / l_sc[j]).astype(BF16)


def _attention(q, k, v, *, tq, tk):
    b, s, aw = q.shape
    group = aw // HEAD_DIM // N_KV_HEADS
    gw = group * HEAD_DIM
    return pl.pallas_call(
        functools.partial(_attn_kernel, group=group),
        grid=(b, N_KV_HEADS, s // tq, s // tk),
        in_specs=[pl.BlockSpec((1, tq, gw), lambda bi, h, qi, ki: (bi, qi, h)),
                  pl.BlockSpec((1, tk, HEAD_DIM), lambda bi, h, qi, ki: (bi, ki, h)),
                  pl.BlockSpec((1, tk, HEAD_DIM), lambda bi, h, qi, ki: (bi, ki, h))],
        out_specs=pl.BlockSpec((1, tq, gw), lambda bi, h, qi, ki: (bi, qi, h)),
        out_shape=jax.ShapeDtypeStruct((b, s, aw), BF16),
        scratch_shapes=[pltpu.VMEM((group, tq, 1), F32), pltpu.VMEM((group, tq, 1), F32),
                        pltpu.VMEM((group, tq, HEAD_DIM), F32)],
        compiler_params=_params("parallel", "parallel", "parallel", "arbitrary"),
        name="attention",
    )(q, k, v)


def _fourier1_kernel(f_ref, u_ref, y_ref, *, n1):
    y = _dot(f_ref[...], u_ref[0])
    y_ref[0, 0] = y[:n1].astype(BF16)
    y_ref[0, 1] = y[n1:].astype(BF16)


def _fourier2_kernel(m_ref, y_ref, cc_ref, sc_ref, o_ref, *, n2, groups):
    y = y_ref[0].reshape(2 * n2, groups * FNET_GROUP_DIM)
    g = _dot(m_ref[0], y)
    gr = g[:n2].astype(BF16)
    gi = g[n2:].astype(BF16)
    for c in range(groups):
        sl = slice(c * FNET_GROUP_DIM, (c + 1) * FNET_GROUP_DIM)
        o_ref[0, :, sl] = (_dot(gr[:, sl], cc_ref[...]) + _dot(gi[:, sl], sc_ref[...])).astype(BF16)


def _fourier_tables(seq):
    n2 = 128
    n1 = seq // n2
    a1 = 2.0 * np.pi * np.outer(np.arange(n1), np.arange(n1)) / n1
    f1 = np.concatenate([np.cos(a1), -np.sin(a1)], axis=0)
    kk = np.arange(n1)[:, None, None] + n1 * np.arange(n2)[None, :, None]
    th = 2.0 * np.pi * (kk * np.arange(n2)[None, None, :] % seq) / seq
    mc, ms = np.cos(th), np.sin(th)
    m = np.concatenate([np.concatenate([mc, ms], axis=2),
                        np.concatenate([-ms, mc], axis=2)], axis=1)
    ac = 2.0 * np.pi * np.outer(np.arange(FNET_GROUP_DIM), np.arange(FNET_GROUP_DIM)) / FNET_GROUP_DIM
    norm = 1.0 / math.sqrt(seq * FNET_GROUP_DIM)
    return (jnp.asarray(f1, BF16), jnp.asarray(m, BF16),
            jnp.asarray(np.cos(ac) * norm, BF16), jnp.asarray(np.sin(ac) * norm, BF16))


def _fourier_mix(u, *, batch, seq, tn):
    t, width = u.shape
    n2 = 128
    n1 = seq // n2
    groups = width // FNET_GROUP_DIM
    f1, m, cc, sc = _fourier_tables(seq)
    cols = n2 * width
    y = pl.pallas_call(
        functools.partial(_fourier1_kernel, n1=n1),
        grid=(batch, cols // tn),
        in_specs=[_resident((2 * n1, n1)),
                  pl.BlockSpec((1, n1, tn), lambda b, j: (b, 0, j))],
        out_specs=pl.BlockSpec((1, 2, n1, tn), lambda b, j: (b, 0, 0, j)),
        out_shape=jax.ShapeDtypeStruct((batch, 2, n1, cols), BF16),
        compiler_params=_params("parallel", "parallel"),
        name="fourier_seq1",
    )(f1, u.reshape(batch, n1, cols))
    y = y.reshape(batch, 2, seq, width)
    out = pl.pallas_call(
        functools.partial(_fourier2_kernel, n2=n2, groups=groups),
        grid=(n1, batch),
        in_specs=[pl.BlockSpec((1, 2 * n2, 2 * n2), lambda k1, b: (k1, 0, 0)),
                  pl.BlockSpec((1, 2, n2, width), lambda k1, b: (b, 0, k1, 0)),
                  _resident((FNET_GROUP_DIM, FNET_GROUP_DIM)),
                  _resident((FNET_GROUP_DIM, FNET_GROUP_DIM))],
        out_specs=pl.BlockSpec((1, n2, width), lambda k1, b: (b, 0, k1)),
        out_shape=jax.ShapeDtypeStruct((batch, n2, n1 * width), BF16),
        compiler_params=_params("parallel", "parallel"),
        name="fourier_seq2",
    )(m, y, cc, sc)
    return out.reshape(t, width)


def _proj_res_kernel(*refs, n_lhs, has_next):
    lhs = refs[:n_lhs]
    ws = refs[n_lhs:2 * n_lhs]
    x_ref, gp_ref = refs[2 * n_lhs:2 * n_lhs + 2]
    pos = 2 * n_lhs + 2
    gn_ref = refs[pos] if has_next else None
    pos += int(has_next)
    xo_ref = refs[pos]
    ho_ref = refs[pos + 1] if has_next else None
    acc = refs[-1]
    kk = pl.program_id(1)

    @pl.when(kk == 0)
    def _():
        acc[...] = jnp.zeros(acc.shape, F32)

    part = _dot(lhs[0][...], ws[0][...])
    for a, w in zip(lhs[1:], ws[1:]):
        part += _dot(a[...], w[...])
    acc[...] += part

    @pl.when(kk == pl.num_programs(1) - 1)
    def _():
        xn = x_ref[...] + _rms(acc[...], gp_ref[...])
        xo_ref[...] = xn
        if has_next:
            ho_ref[...] = _rms(xn, gn_ref[...]).astype(BF16)


def _proj_res(lhs, ws, x, g_post, g_next, *, tm, tk):
    t, d = x.shape
    n_lhs = len(lhs)
    kdim = lhs[0].shape[1]
    has_next = g_next is not None
    in_specs = [pl.BlockSpec((tm, tk), lambda i, k: (i, k)) for _ in lhs]
    in_specs += [pl.BlockSpec((tk, d), lambda i, k: (k, 0)) for _ in ws]
    in_specs += [pl.BlockSpec((tm, d), lambda i, k: (i, 0)), _resident((1, d))]
    args = list(lhs) + list(ws) + [x, g_post]
    out_specs = [pl.BlockSpec((tm, d), lambda i, k: (i, 0))]
    out_shape = [jax.ShapeDtypeStruct((t, d), F32)]
    if has_next:
        in_specs.append(_resident((1, d)))
        args.append(g_next)
        out_specs.append(pl.BlockSpec((tm, d), lambda i, k: (i, 0)))
        out_shape.append(jax.ShapeDtypeStruct((t, d), BF16))
    res = pl.pallas_call(
        functools.partial(_proj_res_kernel, n_lhs=n_lhs, has_next=has_next),
        grid=(t // tm, kdim // tk),
        in_specs=in_specs, out_specs=out_specs, out_shape=out_shape,
        scratch_shapes=[pltpu.VMEM((tm, d), F32)],
        compiler_params=_params("parallel", "arbitrary"),
        name="proj_res",
    )(*args)
    return res if has_next else (res[0], None)


def _gelu_tanh(x):
    return 0.5 * x * (1.0 + jnp.tanh(math.sqrt(2.0 / math.pi) * (x + 0.044715 * x * x * x)))


def _silu(x):
    return x * (1.0 / (1.0 + jnp.exp(-x)))


def _assemble_halo(hext, h_ref, hp_ref, hn_ref, *, tm, tps):
    i = pl.program_id(0)
    first = (i % tps) == 0
    last = (i % tps) == tps - 1
    hext[0:HALO] = jnp.where(first, jnp.zeros_like(hp_ref[...]), hp_ref[...])
    hext[HALO:HALO + tm] = h_ref[...]
    hext[HALO + tm:] = jnp.where(last, jnp.zeros_like(hn_ref[...]), hn_ref[...])


def _conv_rows(u, cw, cb, *, tm):
    rows = tm + 2 * HALO
    c = cb + cw[0:1] * pltpu.roll(u, 1, 0) + cw[1:2] * u + cw[2:3] * pltpu.roll(u, rows - 1, 0)
    return c[HALO:HALO + tm]


def _halo_specs(t, tm, kdim):
    hb = tm // HALO
    last_blk = t // HALO - 1
    return [pl.BlockSpec((tm, kdim), lambda i, j: (i, 0)),
            pl.BlockSpec((HALO, kdim), lambda i, j: (jnp.maximum(i * hb - 1, 0), 0)),
            pl.BlockSpec((HALO, kdim), lambda i, j: (jnp.minimum((i + 1) * hb, last_blk), 0))]


def _conv_mm_kernel(*refs, act, tm, tps):
    if act == "none":
        h_ref, w_ref, o_ref = refs
        o_ref[...] = _dot(h_ref[...], w_ref[...]).astype(o_ref.dtype)
        return
    h_ref, hp_ref, hn_ref, w_ref, cw_ref, cb_ref, o_ref, hext = refs

    @pl.when(pl.program_id(1) == 0)
    def _():
        _assemble_halo(hext, h_ref, hp_ref, hn_ref, tm=tm, tps=tps)

    u = _dot(hext[...], w_ref[...])
    o_ref[...] = _silu(_conv_rows(u, cw_ref[...], cb_ref[...], tm=tm)).astype(o_ref.dtype)


def _conv_mm(h, w, conv_w, conv_b, *, act, seq, tm, tn, col0, ncols):
    t, kdim = h.shape
    nj = ncols // tn
    j0 = col0 // tn
    if act == "none":
        in_specs = [pl.BlockSpec((tm, kdim), lambda i, j: (i, 0)),
                    pl.BlockSpec((kdim, tn), lambda i, j: (0, j0 + j))]
        args = [h, w]
        scratch = []
    else:
        in_specs = _halo_specs(t, tm, kdim) + [
            pl.BlockSpec((kdim, tn), lambda i, j: (0, j0 + j)),
            pl.BlockSpec((3, tn), lambda i, j: (0, j)),
            pl.BlockSpec((1, tn), lambda i, j: (0, j))]
        args = [h, h, h, w, conv_w, conv_b]
        scratch = [pltpu.VMEM((tm + 2 * HALO, kdim), BF16)]
    return pl.pallas_call(
        functools.partial(_conv_mm_kernel, act=act, tm=tm, tps=seq // tm),
        grid=(t // tm, nj),
        in_specs=in_specs,
        out_specs=pl.BlockSpec((tm, tn), lambda i, j: (i, j)),
        out_shape=jax.ShapeDtypeStruct((t, ncols), BF16),
        scratch_shapes=scratch,
        compiler_params=_params("parallel", "arbitrary"),
        name="conv_mm_" + act,
    )(*args)


def _ffn_kernel(*refs, tm, tps, nj, has_next):
    (h_ref, hp_ref, hn_ref, wg_ref, wv_ref, cwg_ref, cwv_ref, cbg_ref, cbv_ref,
     wd_ref, wdl_ref, x_ref, gp_ref) = refs[:13]
    pos = 13
    gn_ref = refs[pos] if has_next else None
    pos += int(has_next)
    xo_ref = refs[pos]
    ho_ref = refs[pos + 1] if has_next else None
    hext, act_a, act_b, acc = refs[-4:]
    j = pl.program_id(1)

    @pl.when(j == 0)
    def _():
        _assemble_halo(hext, h_ref, hp_ref, hn_ref, tm=tm, tps=tps)
        act_b[...] = jnp.zeros(act_b.shape, BF16)
        acc[...] = jnp.zeros(acc.shape, F32)

    def step(act_prev, act_cur):
        hx = hext[...]
        ug = _dot(hx, wg_ref[...])
        uv = _dot(hx, wv_ref[...])
        acc[...] += _dot(act_prev[...], wd_ref[...])
        gate = _conv_rows(ug, cwg_ref[...], cbg_ref[...], tm=tm)
        val = _conv_rows(uv, cwv_ref[...], cbv_ref[...], tm=tm)
        act_cur[...] = (_gelu_tanh(gate) * val).astype(BF16)

    @pl.when(j % 2 == 0)
    def _():
        step(act_b, act_a)

    @pl.when(j % 2 == 1)
    def _():
        step(act_a, act_b)

    @pl.when(j == nj - 1)
    def _():
        act_last = act_a if (nj - 1) % 2 == 0 else act_b
        m = acc[...] + _dot(act_last[...], wdl_ref[...])
        xn = x_ref[...] + _rms(m, gp_ref[...])
        xo_ref[...] = xn
        if has_next:
            ho_ref[...] = _rms(xn, gn_ref[...]).astype(BF16)


def _ffn(h, w_up, conv_w, conv_b, w_down, x, g_post, g_next, *, seq, tm, tn):
    t, d = x.shape
    d_ff = w_down.shape[0]
    nj = d_ff // tn
    has_next = g_next is not None
    row = pl.BlockSpec((tm, d), lambda i, j: (i, 0))
    in_specs = _halo_specs(t, tm, d) + [
        pl.BlockSpec((d, tn), lambda i, j: (0, j)),
        pl.BlockSpec((d, tn), lambda i, j: (0, nj + j)),
        pl.BlockSpec((3, tn), lambda i, j: (0, j)),
        pl.BlockSpec((3, tn), lambda i, j: (0, nj + j)),
        pl.BlockSpec((1, tn), lambda i, j: (0, j)),
        pl.BlockSpec((1, tn), lambda i, j: (0, nj + j)),
        pl.BlockSpec((tn, d), lambda i, j: (jnp.maximum(j - 1, 0), 0)),
        pl.BlockSpec((tn, d), lambda i, j: (nj - 1, 0), pipeline_mode=pl.Buffered(1)),
        row, _resident((1, d))]
    args = [h, h, h, w_up, w_up, conv_w, conv_w, conv_b, conv_b, w_down, w_down, x, g_post]
    out_specs = [row]
    out_shape = [jax.ShapeDtypeStruct((t, d), F32)]
    if has_next:
        in_specs.append(_resident((1, d)))
        args.append(g_next)
        out_specs.append(row)
        out_shape.append(jax.ShapeDtypeStruct((t, d), BF16))
    res = pl.pallas_call(
        functools.partial(_ffn_kernel, tm=tm, tps=seq // tm, nj=nj, has_next=has_next),
        grid=(t // tm, nj),
        in_specs=in_specs, out_specs=out_specs, out_shape=out_shape,
        scratch_shapes=[pltpu.VMEM((tm + 2 * HALO, d), BF16), pltpu.VMEM((tm, tn), BF16),
                        pltpu.VMEM((tm, tn), BF16), pltpu.VMEM((tm, d), F32)],
        compiler_params=_params("parallel", "arbitrary"),
        name="ffn",
    )(*args)
    return res if has_next else (res[0], None)


def _dt_kernel(h_ref, w_ref, b_ref, o_ref):
    r = _dot(h_ref[...], w_ref[...]) + b_ref[...]
    o_ref[...] = jnp.maximum(r, 0.0) + jnp.log(1.0 + jnp.exp(-jnp.abs(r)))


def _dt_proj(h, w, b, *, tm):
    t, kdim = h.shape
    n = w.shape[1]
    return pl.pallas_call(
        _dt_kernel,
        grid=(t // tm,),
        in_specs=[pl.BlockSpec((tm, kdim), lambda i: (i, 0)), _resident((kdim, n)), _resident((1, n))],
        out_specs=pl.BlockSpec((tm, n), lambda i: (i, 0)),
        out_shape=jax.ShapeDtypeStruct((t, n), F32),
        compiler_params=_params("parallel"),
        name="ssd_dt",
    )(h, w, b)


def _split3(x):
    hi = x.astype(BF16)
    r = x - hi.astype(F32)
    mid = r.astype(BF16)
    lo = (r - mid.astype(F32)).astype(BF16)
    return hi, mid, lo


def _cumsum_rows(tri, x):
    hi, mid, lo = _split3(x)
    return _dot(tri, hi) + _dot(tri, mid) + _dot(tri, lo)


def _expand(x, e):
    return _dot(x.astype(BF16), e)


class _Chunk:
    pass


def _ssd_positions(c, x_ref, b_ref, c_ref, dt_ref, a_ref, tri_ref):
    rows = slice(c.r0, c.r0 + SSM_CHUNK)
    c.x = x_ref[rows, :]
    c.bm = b_ref[rows, :]
    c.cm = c_ref[rows, :]
    c.dt = dt_ref[rows, :]
    c.dta = c.dt * a_ref[...]
    c.incl = _cumsum_rows(tri_ref[...], c.dta)


def _ssd_weights(c, e_ref):
    L = SSM_CHUNK
    total = c.incl[L - 1:L, :]
    if c.backward:
        c.pos = c.incl - c.dta
        w_state = c.dt * jnp.exp(c.pos)
        w_out = jnp.exp(total - c.pos)
    else:
        c.pos = c.incl
        w_state = c.dt * jnp.exp(total - c.pos)
        w_out = jnp.exp(c.pos)
    e = e_ref[...]
    c.w_out_x = _expand(w_out, e)
    c.carry = c.w_out_x[0:1] if c.backward else c.w_out_x[L - 1:L]
    c.xs = (c.x.astype(F32) * _expand(w_state, e)).astype(BF16)
    c.b_t = c.bm.astype(F32).T.astype(BF16)
    c.cb = lax.dot_general(c.cm, c.bm, _NT, preferred_element_type=F32)
    c.pos_t = c.pos.T
    c.dt_t = c.dt.T


def _ssd_state(c, h_in):
    c.y = _dot(c.cm, h_in.astype(BF16)) * c.w_out_x
    return h_in * c.carry + _dot(c.b_t, c.xs)


def _ssd_diag(c, y_ref, *, hpg):
    L = SSM_CHUNK
    rows = slice(c.r0, c.r0 + L)
    li = lax.broadcasted_iota(jnp.int32, (L, L), 0)
    si = lax.broadcasted_iota(jnp.int32, (L, L), 1)
    mask = (si >= li) if c.backward else (li >= si)
    lane = lax.broadcasted_iota(jnp.int32, (L, 2 * SSM_HEAD_DIM), 1)
    for jp in range(hpg // 2):
        sl = slice(jp * 2 * SSM_HEAD_DIM, (jp + 1) * 2 * SSM_HEAD_DIM)
        xp = c.x[:, sl]
        outs = []
        for j in (c.lane0 + 2 * jp, c.lane0 + 2 * jp + 1):
            col = c.pos[:, j:j + 1]
            row = c.pos_t[j:j + 1, :]
            d = (row - col) if c.backward else (col - row)
            w = c.cb * jnp.exp(jnp.where(mask, d, -1e30)) * c.dt_t[j:j + 1, :]
            outs.append(_dot(w.astype(BF16), xp))
        y_ref[rows, sl] = (c.y[:, sl] + jnp.where(lane < SSM_HEAD_DIM, outs[0], outs[1])).astype(y_ref.dtype)


def _ssd_kernel(xf, bf, cf, dtf, xb, bb, cb, dtb, a_ref, tri_ref, ef_ref, eb_ref,
                yf_ref, yb_ref, hf_sc, hb_sc, *, hpg, cps):
    @pl.when(pl.program_id(2) == 0)
    def _():
        hf_sc[...] = jnp.zeros(hf_sc.shape, F32)
        hb_sc[...] = jnp.zeros(hb_sc.shape, F32)

    fwd, bwd = [], []
    for k in range(cps):
        f, b = _Chunk(), _Chunk()
        f.r0, f.lane0, f.backward = k * SSM_CHUNK, 0, False
        b.r0, b.lane0, b.backward = (cps - 1 - k) * SSM_CHUNK, hpg, True
        fwd.append(f)
        bwd.append(b)
    for f, b in zip(fwd, bwd):
        _ssd_positions(f, xf, bf, cf, dtf, a_ref, tri_ref)
        _ssd_positions(b, xb, bb, cb, dtb, a_ref, tri_ref)
    for f, b in zip(fwd, bwd):
        _ssd_weights(f, ef_ref)
        _ssd_weights(b, eb_ref)
    hf = hf_sc[...]
    hb = hb_sc[...]
    for f, b in zip(fwd, bwd):
        hf = _ssd_state(f, hf)
        hb = _ssd_state(b, hb)
    hf_sc[...] = hf
    hb_sc[...] = hb
    for f, b in zip(fwd, bwd):
        _ssd_diag(f, yf_ref, hpg=hpg)
        _ssd_diag(b, yb_ref, hpg=hpg)


def _ssd(xbc, dt, a_pad, *, batch, seq, d_inner, cps):
    t = xbc.shape[0]
    L = SSM_CHUNK
    rows = cps * L
    nc = seq // rows
    gw = d_inner // SSM_N_GROUPS
    hpg = gw // SSM_HEAD_DIM
    b0 = d_inner // SSM_D_STATE
    c0 = b0 + SSM_N_GROUPS
    tri = jnp.asarray(np.tril(np.ones((L, L))), BF16)
    head_of_lane = np.arange(gw) // SSM_HEAD_DIM
    ef = jnp.asarray(np.arange(LANES)[:, None] == head_of_lane[None, :], BF16)
    eb = jnp.asarray(np.arange(LANES)[:, None] == (head_of_lane[None, :] + hpg), BF16)

    def fwd(col):
        return lambda b, g, c: (b * nc + c, col(g))

    def bwd(col):
        return lambda b, g, c: (b * nc + (nc - 1 - c), col(g))

    def specs(order):
        return [pl.BlockSpec((rows, gw), order(lambda g: g)),
                pl.BlockSpec((rows, SSM_D_STATE), order(lambda g: b0 + g)),
                pl.BlockSpec((rows, SSM_D_STATE), order(lambda g: c0 + g)),
                pl.BlockSpec((rows, LANES), order(lambda g: g))]

    in_specs = specs(fwd) + specs(bwd) + [
        pl.BlockSpec((1, LANES), lambda b, g, c: (0, g)),
        _resident((L, L)), _resident((LANES, gw)), _resident((LANES, gw))]
    return pl.pallas_call(
        functools.partial(_ssd_kernel, hpg=hpg, cps=cps),
        grid=(batch, SSM_N_GROUPS, nc),
        in_specs=in_specs,
        out_specs=[pl.BlockSpec((rows, gw), fwd(lambda g: g)), pl.BlockSpec((rows, gw), bwd(lambda g: g))],
        out_shape=[jax.ShapeDtypeStruct((t, d_inner), BF16), jax.ShapeDtypeStruct((t, d_inner), BF16)],
        scratch_shapes=[pltpu.VMEM((SSM_D_STATE, gw), F32), pltpu.VMEM((SSM_D_STATE, gw), F32)],
        compiler_params=_params("parallel", "parallel", "arbitrary"),
        name="ssd_scan",
    )(xbc, xbc, xbc, dt, xbc, xbc, xbc, dt, a_pad, tri, ef, eb)


def _ssd_out_kernel(yf_ref, yb_ref, xs_ref, z_ref, dsk_ref, ng_ref, w_ref, x_ref, gp_ref, gn_ref,
                    xo_ref, ho_ref, acc, ss, *, d_inner):
    kk = pl.program_id(1)

    @pl.when(kk == 0)
    def _():
        acc[...] = jnp.zeros(acc.shape, F32)
        ss[...] = jnp.zeros(ss.shape, F32)

    y = yf_ref[...].astype(F32) + yb_ref[...].astype(F32) + xs_ref[...].astype(F32) * dsk_ref[...]
    y = y * _silu(z_ref[...].astype(F32))
    ss[...] += jnp.sum(y * y, axis=-1, keepdims=True)
    acc[...] += _dot((y * ng_ref[...]).astype(BF16), w_ref[...])

    @pl.when(kk == pl.num_programs(1) - 1)
    def _():
        m = acc[...] * lax.rsqrt(ss[...] * (1.0 / d_inner) + EPS)
        xn = x_ref[...] + _rms(m, gp_ref[...])
        xo_ref[...] = xn
        ho_ref[...] = _rms(xn, gn_ref[...]).astype(BF16)


def _ssd_out(yf, yb, xbc, z, d_exp, norm_g, w, x, g_post, g_next, *, tm, tk):
    t, d = x.shape
    d_inner = yf.shape[1]
    lhs = pl.BlockSpec((tm, tk), lambda i, k: (i, k))
    vec = pl.BlockSpec((1, tk), lambda i, k: (0, k))
    row = pl.BlockSpec((tm, d), lambda i, k: (i, 0))
    return pl.pallas_call(
        functools.partial(_ssd_out_kernel, d_inner=d_inner),
        grid=(t // tm, d_inner // tk),
        in_specs=[lhs, lhs, lhs, lhs, vec, vec, pl.BlockSpec((tk, d), lambda i, k: (k, 0)),
                  row, _resident((1, d)), _resident((1, d))],
        out_specs=[row, row],
        out_shape=[jax.ShapeDtypeStruct((t, d), F32), jax.ShapeDtypeStruct((t, d), BF16)],
        scratch_shapes=[pltpu.VMEM((tm, d), F32), pltpu.VMEM((tm, 1), F32)],
        compiler_params=_params("parallel", "arbitrary"),
        name="ssd_out",
    )(yf, yb, xbc, z, d_exp, norm_g, w, x, g_post, g_next)


def _rope_tables(seq):
    rows = seq // GRID_W
    row = jnp.repeat(jnp.arange(rows, dtype=F32), GRID_W)
    col = jnp.tile(jnp.arange(GRID_W, dtype=F32), rows)
    axis_dim = HEAD_DIM // 2
    inv_freq = ROPE_THETA ** (-jnp.arange(0, axis_dim, 2, dtype=F32) / axis_dim)
    ang = jnp.concatenate([row[:, None] * inv_freq, col[:, None] * inv_freq], axis=-1)
    cos, sin = jnp.cos(ang), jnp.sin(ang)
    return jnp.concatenate([cos, cos], axis=-1), jnp.concatenate([-sin, sin], axis=-1)


def _pick(n, pref):
    t = min(pref, n)
    while n % t:
        t -= LANES
    return t


def kernel(x_prompt, x_sample, mix_pre_g, mix_post_g, ffn_pre_g, ffn_post_g, fa_w_in, fa_q_gain, fa_k_gain, fa_w_out, ssd_w_in, ssd_conv_w, ssd_conv_b, ssd_dt_bias, ssd_a_log, ssd_d, ssd_norm_g, ssd_w_out, ffn_w_up, ffn_conv_w, ffn_conv_b, ffn_w_down):
    depth = mix_pre_g.shape[0]
    d = x_prompt.shape[2]
    aw = N_Q_HEADS * HEAD_DIM
    kvw = N_KV_HEADS * HEAD_DIM
    fw = fa_w_in.shape[2] - aw - 2 * kvw
    assert fa_w_out.shape[1] == fw + aw and fw % FNET_GROUP_DIM == 0
    d_ff = ffn_w_down.shape[1]
    d_inner = ssd_w_out.shape[1]
    n_heads = d_inner // SSM_HEAD_DIM
    hpg = n_heads // SSM_N_GROUPS
    bc_w = SSM_N_GROUPS * SSM_D_STATE
    conv_dim = d_inner + 2 * bc_w
    assert hpg % 2 == 0 and 2 * hpg <= LANES and depth % 2 == 0
    tm = 512
    vec = lambda v: v.reshape(1, -1).astype(F32)

    def regroup(p):
        p = p.astype(F32).reshape(2, SSM_N_GROUPS, hpg).transpose(1, 0, 2).reshape(SSM_N_GROUPS, 2 * hpg)
        return jnp.pad(p, ((0, 0), (0, LANES - 2 * hpg))).reshape(1, SSM_N_GROUPS * LANES)

    layers = []
    for i in range(depth):
        j = i // 2
        lw = dict(ffn_pre=vec(ffn_pre_g[i]), ffn_post=vec(ffn_post_g[i]), mix_pre=vec(mix_pre_g[i]),
                  mix_post=vec(mix_post_g[i]), w_up=ffn_w_up[i].astype(BF16), w_down=ffn_w_down[i].astype(BF16),
                  ffn_cw=ffn_conv_w[i].astype(F32), ffn_cb=vec(ffn_conv_b[i]))
        if i % 2 == 0:
            wo = fa_w_out[j].astype(BF16)
            lw.update(w_in=fa_w_in[j].astype(BF16), qg=vec(fa_q_gain[j]), kg=vec(fa_k_gain[j]),
                      wo_f=wo[:fw], wo_a=wo[fw:])
        else:
            w_dt = ssd_w_in[j][:, d_inner + conv_dim:].reshape(d, 2, SSM_N_GROUPS, hpg)
            w_dt = jnp.pad(w_dt.transpose(0, 2, 1, 3).reshape(d, SSM_N_GROUPS, 2 * hpg),
                           ((0, 0), (0, 0), (0, LANES - 2 * hpg))).reshape(d, SSM_N_GROUPS * LANES)
            lw.update(w_in=ssd_w_in[j].astype(BF16), w_dt=w_dt.astype(BF16), dt_bias=regroup(ssd_dt_bias[j]),
                      a_pad=regroup(-jnp.exp(ssd_a_log[j].astype(F32))), cw=ssd_conv_w[j].astype(F32),
                      cb=vec(ssd_conv_b[j]), d_exp=jnp.repeat(ssd_d[j].astype(F32), SSM_HEAD_DIM).reshape(1, d_inner),
                      norm_g=vec(ssd_norm_g[j]), w_out=ssd_w_out[j].astype(BF16))
        layers.append(lw)

    def trunk(xin):
        batch, seq, _ = xin.shape
        assert seq % tm == 0
        t = batch * seq
        x = xin.reshape(t, d)
        cosf, sinf = _rope_tables(seq)
        h = None
        for i, lw in enumerate(layers):
            if i % 2 == 0:
                u, q, k, v = _inproj0(x, lw["mix_pre"], lw["w_in"], lw["qg"], lw["kg"], cosf, sinf,
                                      seq=seq, fw=fw, aw=aw, kvw=kvw, tm=tm)
                a = _attention(q.reshape(batch, seq, aw), k.reshape(batch, seq, kvw), v.reshape(batch, seq, kvw),
                               tq=_pick(seq, 1024), tk=_pick(seq, 1024)).reshape(t, aw)
                f = _fourier_mix(u, batch=batch, seq=seq, tn=_pick(128 * fw, 8192))
                x, h = _proj_res([f, a], [lw["wo_f"], lw["wo_a"]], x, lw["mix_post"], lw["ffn_pre"], tm=tm, tk=fw)
            else:
                z = _conv_mm(h, lw["w_in"], None, None, act="none", seq=seq, tm=tm, tn=_pick(d_inner, 1024),
                             col0=0, ncols=d_inner)
                xbc = _conv_mm(h, lw["w_in"], lw["cw"], lw["cb"], act="silu", seq=seq, tm=tm,
                               tn=_pick(conv_dim, 1024), col0=d_inner, ncols=conv_dim)
                dt = _dt_proj(h, lw["w_dt"], lw["dt_bias"], tm=tm)
                yf, yb = _ssd(xbc, dt, lw["a_pad"], batch=batch, seq=seq, d_inner=d_inner, cps=8)
                x, h = _ssd_out(yf, yb, xbc, z, lw["d_exp"], lw["norm_g"], lw["w_out"], x,
                                lw["mix_post"], lw["ffn_pre"], tm=tm, tk=_pick(d_inner, 1024))
            g_next = layers[i + 1]["mix_pre"] if i + 1 < depth else None
            x, h = _ffn(h, lw["w_up"], lw["ffn_cw"], lw["ffn_cb"], lw["w_down"], x, lw["ffn_post"], g_next,
                        seq=seq, tm=tm, tn=_pick(d_ff, 512))
        return x.reshape(batch, seq, d)

    return (trunk(x_prompt), trunk(x_sample))
```

```python
import functools
import math

import jax
import jax.numpy as jnp
import numpy as np
from jax import lax
from jax.experimental import pallas as pl
from jax.experimental.pallas import tpu as pltpu

F32 = jnp.float32
BF16 = jnp.bfloat16

EPS = 1e-6
GRID_W = 64
HEAD_DIM = 128
N_Q_HEADS = 8
N_KV_HEADS = 2
FNET_GROUP_DIM = 128
ROPE_THETA = 10000.0
SSM_HEAD_DIM = 64
SSM_N_GROUPS = 8
SSM_D_STATE = 128
SSM_CHUNK = 128
LANES = 128
HALO = 16
VMEM_LIMIT = 56 * 1024 * 1024

_NT = (((1,), (1,)), ((), ()))


def _params(*sem):
    return pltpu.CompilerParams(dimension_semantics=sem, vmem_limit_bytes=VMEM_LIMIT)


def _resident(shape):
    nd = len(shape)
    return pl.BlockSpec(shape, lambda *_: (0,) * nd, pipeline_mode=pl.Buffered(1))


def _rms(x, g):
    return x * lax.rsqrt(jnp.mean(x * x, axis=-1, keepdims=True) + EPS) * g


def _dot(a, b):
    return jnp.dot(a, b, preferred_element_type=F32)


def _inproj0_kernel(x_ref, g_ref, w_ref, qg_ref, kg_ref, cos_ref, sin_ref,
                    u_ref, q_ref, k_ref, v_ref, *, fw, aw, kvw, scale):
    h = _rms(x_ref[...], g_ref[...]).astype(BF16)
    u_ref[...] = _dot(h, w_ref[:, :fw]).astype(BF16)
    cosf = cos_ref[...]
    sinf = sin_ref[...]

    def norm_rope(t, gain, mult):
        t = _rms(t, gain)
        t = t * cosf + pltpu.roll(t, HEAD_DIM // 2, 1) * sinf
        return (t * mult).astype(BF16)

    q = _dot(h, w_ref[:, fw:fw + aw])
    for hh in range(aw // HEAD_DIM):
        sl = slice(hh * HEAD_DIM, (hh + 1) * HEAD_DIM)
        q_ref[:, sl] = norm_rope(q[:, sl], qg_ref[...], scale)
    k = _dot(h, w_ref[:, fw + aw:fw + aw + kvw])
    for hh in range(kvw // HEAD_DIM):
        sl = slice(hh * HEAD_DIM, (hh + 1) * HEAD_DIM)
        k_ref[:, sl] = norm_rope(k[:, sl], kg_ref[...], 1.0)
    v_ref[...] = _dot(h, w_ref[:, fw + aw + kvw:]).astype(BF16)


def _inproj0(x, g, w, qg, kg, cosf, sinf, *, seq, fw, aw, kvw, tm):
    t, d = x.shape
    n = w.shape[1]
    tps = seq // tm
    kern = functools.partial(_inproj0_kernel, fw=fw, aw=aw, kvw=kvw, scale=math.log2(math.e) * HEAD_DIM ** -0.5)
    row = lambda width: pl.BlockSpec((tm, width), lambda i: (i, 0))
    return pl.pallas_call(
        kern,
        grid=(t // tm,),
        in_specs=[row(d), _resident((1, d)), _resident((d, n)),
                  _resident((1, HEAD_DIM)), _resident((1, HEAD_DIM)),
                  pl.BlockSpec((tm, HEAD_DIM), lambda i: (i % tps, 0)),
                  pl.BlockSpec((tm, HEAD_DIM), lambda i: (i % tps, 0))],
        out_specs=[row(fw), row(aw), row(kvw), row(kvw)],
        out_shape=[jax.ShapeDtypeStruct((t, fw), BF16), jax.ShapeDtypeStruct((t, aw), BF16),
                   jax.ShapeDtypeStruct((t, kvw), BF16), jax.ShapeDtypeStruct((t, kvw), BF16)],
        compiler_params=_params("parallel"),
        name="inproj0",
    )(x, g, w, qg, kg, cosf, sinf)


def _attn_kernel(q_ref, k_ref, v_ref, o_ref, m_sc, acc_sc, *, group):
    ki = pl.program_id(3)

    @pl.when(ki == 0)
    def _():
        m_sc[...] = jnp.full(m_sc.shape, -jnp.inf, F32)
        acc_sc[...] = jnp.zeros(acc_sc.shape, F32)

    k = k_ref[0]
    v = v_ref[0]
    v_ext = jnp.concatenate([v, jnp.ones_like(v)], axis=1)
    heads = range(group)
    s = [lax.dot_general(q_ref[0, :, j * HEAD_DIM:(j + 1) * HEAD_DIM], k, _NT,
                         preferred_element_type=F32).astype(BF16) for j in heads]
    p, alpha = [], []
    for j in heads:
        m_prev = m_sc[j]
        m_new = jnp.maximum(m_prev, jnp.max(s[j], axis=-1, keepdims=True).astype(F32))
        alpha.append(jnp.exp2(m_prev - m_new))
        p.append(jnp.exp2(s[j] - m_new.astype(BF16)))
        m_sc[j] = m_new
    for j in heads:
        acc_sc[j] = alpha[j] * acc_sc[j] + _dot(p[j], v_ext)

    @pl.when(ki == pl.num_programs(3) - 1)
    def _():
        for j in heads:
            acc = acc_sc[j]
            o_ref[0, :, j * HEAD_DIM:(j + 1) * HEAD_DIM] = (acc[:, :HEAD_DIM] / acc[:, HEAD_DIM:]).astype(BF16)


def _attention(q, k, v, *, tq, tk):
    b, s, aw = q.shape
    group = aw // HEAD_DIM // N_KV_HEADS
    gw = group * HEAD_DIM
    return pl.pallas_call(
        functools.partial(_attn_kernel, group=group),
        grid=(b, N_KV_HEADS, s // tq, s // tk),
        in_specs=[pl.BlockSpec((1, tq, gw), lambda bi, h, qi, ki: (bi, qi, h)),
                  pl.BlockSpec((1, tk, HEAD_DIM), lambda bi, h, qi, ki: (bi, ki, h)),
                  pl.BlockSpec((1, tk, HEAD_DIM), lambda bi, h, qi, ki: (bi, ki, h))],
        out_specs=pl.BlockSpec((1, tq, gw), lambda bi, h, qi, ki: (bi, qi, h)),
        out_shape=jax.ShapeDtypeStruct((b, s, aw), BF16),
        scratch_shapes=[pltpu.VMEM((group, tq, 1), F32), pltpu.VMEM((group, tq, 2 * HEAD_DIM), F32)],
        compiler_params=_params("parallel", "parallel", "parallel", "arbitrary"),
        name="attention",
    )(q, k, v)


def _fourier1_kernel(f_ref, u_ref, y_ref, *, n1):
    y = _dot(f_ref[...], u_ref[0])
    y_ref[0, 0] = y[:n1].astype(BF16)
    y_ref[0, 1] = y[n1:].astype(BF16)


def _fourier2_kernel(m_ref, y_ref, cc_ref, sc_ref, o_ref, *, n2, groups):
    y = y_ref[0].reshape(2 * n2, groups * FNET_GROUP_DIM)
    g = _dot(m_ref[0], y)
    gr = g[:n2].astype(BF16)
    gi = g[n2:].astype(BF16)
    for c in range(groups):
        sl = slice(c * FNET_GROUP_DIM, (c + 1) * FNET_GROUP_DIM)
        o_ref[0, :, sl] = (_dot(gr[:, sl], cc_ref[...]) + _dot(gi[:, sl], sc_ref[...])).astype(BF16)


def _fourier_tables(seq):
    n2 = 128
    n1 = seq // n2
    a1 = 2.0 * np.pi * np.outer(np.arange(n1), np.arange(n1)) / n1
    f1 = np.concatenate([np.cos(a1), -np.sin(a1)], axis=0)
    kk = np.arange(n1)[:, None, None] + n1 * np.arange(n2)[None, :, None]
    th = 2.0 * np.pi * (kk * np.arange(n2)[None, None, :] % seq) / seq
    mc, ms = np.cos(th), np.sin(th)
    m = np.concatenate([np.concatenate([mc, ms], axis=2),
                        np.concatenate([-ms, mc], axis=2)], axis=1)
    ac = 2.0 * np.pi * np.outer(np.arange(FNET_GROUP_DIM), np.arange(FNET_GROUP_DIM)) / FNET_GROUP_DIM
    norm = 1.0 / math.sqrt(seq * FNET_GROUP_DIM)
    return (jnp.asarray(f1, BF16), jnp.asarray(m, BF16),
            jnp.asarray(np.cos(ac) * norm, BF16), jnp.asarray(np.sin(ac) * norm, BF16))


def _fourier_mix(u, *, batch, seq, tn):
    t, width = u.shape
    n2 = 128
    n1 = seq // n2
    groups = width // FNET_GROUP_DIM
    f1, m, cc, sc = _fourier_tables(seq)
    cols = n2 * width
    y = pl.pallas_call(
        functools.partial(_fourier1_kernel, n1=n1),
        grid=(batch, cols // tn),
        in_specs=[_resident((2 * n1, n1)),
                  pl.BlockSpec((1, n1, tn), lambda b, j: (b, 0, j))],
        out_specs=pl.BlockSpec((1, 2, n1, tn), lambda b, j: (b, 0, 0, j)),
        out_shape=jax.ShapeDtypeStruct((batch, 2, n1, cols), BF16),
        compiler_params=_params("parallel", "parallel"),
        name="fourier_seq1",
    )(f1, u.reshape(batch, n1, cols))
    y = y.reshape(batch, 2, seq, width)
    out = pl.pallas_call(
        functools.partial(_fourier2_kernel, n2=n2, groups=groups),
        grid=(n1, batch),
        in_specs=[pl.BlockSpec((1, 2 * n2, 2 * n2), lambda k1, b: (k1, 0, 0)),
                  pl.BlockSpec((1, 2, n2, width), lambda k1, b: (b, 0, k1, 0)),
                  _resident((FNET_GROUP_DIM, FNET_GROUP_DIM)),
                  _resident((FNET_GROUP_DIM, FNET_GROUP_DIM))],
        out_specs=pl.BlockSpec((1, n2, width), lambda k1, b: (b, 0, k1)),
        out_shape=jax.ShapeDtypeStruct((batch, n2, n1 * width), BF16),
        compiler_params=_params("parallel", "parallel"),
        name="fourier_seq2",
    )(m, y, cc, sc)
    return out.reshape(t, width)


def _proj_res_kernel(*refs, n_lhs, has_next):
    lhs = refs[:n_lhs]
    ws = refs[n_lhs:2 * n_lhs]
    x_ref, gp_ref = refs[2 * n_lhs:2 * n_lhs + 2]
    pos = 2 * n_lhs + 2
    gn_ref = refs[pos] if has_next else None
    pos += int(has_next)
    xo_ref = refs[pos]
    ho_ref = refs[pos + 1] if has_next else None
    acc = refs[-1]
    kk = pl.program_id(1)

    @pl.when(kk == 0)
    def _():
        acc[...] = jnp.zeros(acc.shape, F32)

    part = _dot(lhs[0][...], ws[0][...])
    for a, w in zip(lhs[1:], ws[1:]):
        part += _dot(a[...], w[...])
    acc[...] += part

    @pl.when(kk == pl.num_programs(1) - 1)
    def _():
        xn = x_ref[...] + _rms(acc[...], gp_ref[...])
        xo_ref[...] = xn
        if has_next:
            ho_ref[...] = _rms(xn, gn_ref[...]).astype(BF16)


def _proj_res(lhs, ws, x, g_post, g_next, *, tm, tk):
    t, d = x.shape
    n_lhs = len(lhs)
    kdim = lhs[0].shape[1]
    has_next = g_next is not None
    in_specs = [pl.BlockSpec((tm, tk), lambda i, k: (i, k)) for _ in lhs]
    in_specs += [pl.BlockSpec((tk, d), lambda i, k: (k, 0)) for _ in ws]
    in_specs += [pl.BlockSpec((tm, d), lambda i, k: (i, 0)), _resident((1, d))]
    args = list(lhs) + list(ws) + [x, g_post]
    out_specs = [pl.BlockSpec((tm, d), lambda i, k: (i, 0))]
    out_shape = [jax.ShapeDtypeStruct((t, d), F32)]
    if has_next:
        in_specs.append(_resident((1, d)))
        args.append(g_next)
        out_specs.append(pl.BlockSpec((tm, d), lambda i, k: (i, 0)))
        out_shape.append(jax.ShapeDtypeStruct((t, d), BF16))
    res = pl.pallas_call(
        functools.partial(_proj_res_kernel, n_lhs=n_lhs, has_next=has_next),
        grid=(t // tm, kdim // tk),
        in_specs=in_specs, out_specs=out_specs, out_shape=out_shape,
        scratch_shapes=[pltpu.VMEM((tm, d), F32)],
        compiler_params=_params("parallel", "arbitrary"),
        name="proj_res",
    )(*args)
    return res if has_next else (res[0], None)


def _gelu_tanh(x):
    return 0.5 * x * (1.0 + jnp.tanh(math.sqrt(2.0 / math.pi) * (x + 0.044715 * x * x * x)))


def _silu(x):
    return x * (1.0 / (1.0 + jnp.exp(-x)))


def _assemble_halo(hext, h_ref, hp_ref, hn_ref, *, tm, tps):
    i = pl.program_id(0)
    first = (i % tps) == 0
    last = (i % tps) == tps - 1
    hext[0:HALO] = jnp.where(first, jnp.zeros_like(hp_ref[...]), hp_ref[...])
    hext[HALO:HALO + tm] = h_ref[...]
    hext[HALO + tm:] = jnp.where(last, jnp.zeros_like(hn_ref[...]), hn_ref[...])


def _conv_rows(u, cw, cb, *, tm):
    rows = tm + 2 * HALO
    c = cb + cw[0:1] * pltpu.roll(u, 1, 0) + cw[1:2] * u + cw[2:3] * pltpu.roll(u, rows - 1, 0)
    return c[HALO:HALO + tm]


def _halo_specs(t, tm, kdim):
    hb = tm // HALO
    last_blk = t // HALO - 1
    return [pl.BlockSpec((tm, kdim), lambda i, j: (i, 0)),
            pl.BlockSpec((HALO, kdim), lambda i, j: (jnp.maximum(i * hb - 1, 0), 0)),
            pl.BlockSpec((HALO, kdim), lambda i, j: (jnp.minimum((i + 1) * hb, last_blk), 0))]


def _conv_mm_kernel(*refs, act, tm, tps):
    if act == "none":
        h_ref, w_ref, o_ref = refs
        o_ref[...] = _dot(h_ref[...], w_ref[...]).astype(o_ref.dtype)
        return
    h_ref, hp_ref, hn_ref, w_ref, cw_ref, cb_ref, o_ref, hext = refs

    @pl.when(pl.program_id(1) == 0)
    def _():
        _assemble_halo(hext, h_ref, hp_ref, hn_ref, tm=tm, tps=tps)

    u = _dot(hext[...], w_ref[...])
    o_ref[...] = _silu(_conv_rows(u, cw_ref[...], cb_ref[...], tm=tm)).astype(o_ref.dtype)


def _conv_mm(h, w, conv_w, conv_b, *, act, seq, tm, tn, col0, ncols):
    t, kdim = h.shape
    nj = ncols // tn
    j0 = col0 // tn
    if act == "none":
        in_specs = [pl.BlockSpec((tm, kdim), lambda i, j: (i, 0)),
                    pl.BlockSpec((kdim, tn), lambda i, j: (0, j0 + j))]
        args = [h, w]
        scratch = []
    else:
        in_specs = _halo_specs(t, tm, kdim) + [
            pl.BlockSpec((kdim, tn), lambda i, j: (0, j0 + j)),
            pl.BlockSpec((3, tn), lambda i, j: (0, j)),
            pl.BlockSpec((1, tn), lambda i, j: (0, j))]
        args = [h, h, h, w, conv_w, conv_b]
        scratch = [pltpu.VMEM((tm + 2 * HALO, kdim), BF16)]
    return pl.pallas_call(
        functools.partial(_conv_mm_kernel, act=act, tm=tm, tps=seq // tm),
        grid=(t // tm, nj),
        in_specs=in_specs,
        out_specs=pl.BlockSpec((tm, tn), lambda i, j: (i, j)),
        out_shape=jax.ShapeDtypeStruct((t, ncols), BF16),
        scratch_shapes=scratch,
        compiler_params=_params("parallel", "arbitrary"),
        name="conv_mm_" + act,
    )(*args)


def _ffn_kernel(*refs, tm, tps, nj, has_next):
    (h_ref, hp_ref, hn_ref, wg_ref, wv_ref, cwg_ref, cwv_ref, cbg_ref, cbv_ref,
     wd_ref, wdl_ref, x_ref, gp_ref) = refs[:13]
    pos = 13
    gn_ref = refs[pos] if has_next else None
    pos += int(has_next)
    xo_ref = refs[pos]
    ho_ref = refs[pos + 1] if has_next else None
    hext, act_a, act_b, acc = refs[-4:]
    j = pl.program_id(1)

    @pl.when(j == 0)
    def _():
        _assemble_halo(hext, h_ref, hp_ref, hn_ref, tm=tm, tps=tps)
        act_b[...] = jnp.zeros(act_b.shape, BF16)
        acc[...] = jnp.zeros(acc.shape, F32)

    def step(act_prev, act_cur):
        hx = hext[...]
        ug = _dot(hx, wg_ref[...])
        uv = _dot(hx, wv_ref[...])
        acc[...] += _dot(act_prev[...], wd_ref[...])
        gate = _conv_rows(ug, cwg_ref[...], cbg_ref[...], tm=tm)
        val = _conv_rows(uv, cwv_ref[...], cbv_ref[...], tm=tm)
        act_cur[...] = (_gelu_tanh(gate) * val).astype(BF16)

    @pl.when(j % 2 == 0)
    def _():
        step(act_b, act_a)

    @pl.when(j % 2 == 1)
    def _():
        step(act_a, act_b)

    @pl.when(j == nj - 1)
    def _():
        act_last = act_a if (nj - 1) % 2 == 0 else act_b
        m = acc[...] + _dot(act_last[...], wdl_ref[...])
        xn = x_ref[...] + _rms(m, gp_ref[...])
        xo_ref[...] = xn
        if has_next:
            ho_ref[...] = _rms(xn, gn_ref[...]).astype(BF16)


def _ffn(h, w_up, conv_w, conv_b, w_down, x, g_post, g_next, *, seq, tm, tn):
    t, d = x.shape
    d_ff = w_down.shape[0]
    nj = d_ff // tn
    has_next = g_next is not None
    row = pl.BlockSpec((tm, d), lambda i, j: (i, 0))
    in_specs = _halo_specs(t, tm, d) + [
        pl.BlockSpec((d, tn), lambda i, j: (0, j)),
        pl.BlockSpec((d, tn), lambda i, j: (0, nj + j)),
        pl.BlockSpec((3, tn), lambda i, j: (0, j)),
        pl.BlockSpec((3, tn), lambda i, j: (0, nj + j)),
        pl.BlockSpec((1, tn), lambda i, j: (0, j)),
        pl.BlockSpec((1, tn), lambda i, j: (0, nj + j)),
        pl.BlockSpec((tn, d), lambda i, j: (jnp.maximum(j - 1, 0), 0)),
        pl.BlockSpec((tn, d), lambda i, j: (nj - 1, 0), pipeline_mode=pl.Buffered(1)),
        row, _resident((1, d))]
    args = [h, h, h, w_up, w_up, conv_w, conv_w, conv_b, conv_b, w_down, w_down, x, g_post]
    out_specs = [row]
    out_shape = [jax.ShapeDtypeStruct((t, d), F32)]
    if has_next:
        in_specs.append(_resident((1, d)))
        args.append(g_next)
        out_specs.append(row)
        out_shape.append(jax.ShapeDtypeStruct((t, d), BF16))
    res = pl.pallas_call(
        functools.partial(_ffn_kernel, tm=tm, tps=seq // tm, nj=nj, has_next=has_next),
        grid=(t // tm, nj),
        in_specs=in_specs, out_specs=out_specs, out_shape=out_shape,
        scratch_shapes=[pltpu.VMEM((tm + 2 * HALO, d), BF16), pltpu.VMEM((tm, tn), BF16),
                        pltpu.VMEM((tm, tn), BF16), pltpu.VMEM((tm, d), F32)],
        compiler_params=_params("parallel", "arbitrary"),
        name="ffn",
    )(*args)
    return res if has_next else (res[0], None)


def _dt_kernel(h_ref, w_ref, b_ref, o_ref):
    r = _dot(h_ref[...], w_ref[...]) + b_ref[...]
    o_ref[...] = jnp.maximum(r, 0.0) + jnp.log(1.0 + jnp.exp(-jnp.abs(r)))


def _dt_proj(h, w, b, *, tm):
    t, kdim = h.shape
    n = w.shape[1]
    return pl.pallas_call(
        _dt_kernel,
        grid=(t // tm,),
        in_specs=[pl.BlockSpec((tm, kdim), lambda i: (i, 0)), _resident((kdim, n)), _resident((1, n))],
        out_specs=pl.BlockSpec((tm, n), lambda i: (i, 0)),
        out_shape=jax.ShapeDtypeStruct((t, n), F32),
        compiler_params=_params("parallel"),
        name="ssd_dt",
    )(h, w, b)


def _split3(x):
    hi = x.astype(BF16)
    r = x - hi.astype(F32)
    mid = r.astype(BF16)
    lo = (r - mid.astype(F32)).astype(BF16)
    return hi, mid, lo


def _cumsum_rows(tri, x):
    hi, mid, lo = _split3(x)
    return _dot(tri, hi) + _dot(tri, mid) + _dot(tri, lo)


def _expand(x, e):
    return _dot(x.astype(BF16), e)


class _Chunk:
    pass


def _ssd_positions(c, x_ref, b_ref, c_ref, dt_ref, a_ref, tri_ref):
    rows = slice(c.r0, c.r0 + SSM_CHUNK)
    c.x = x_ref[rows, :]
    c.bm = b_ref[rows, :]
    c.cm = c_ref[rows, :]
    c.dt = dt_ref[rows, :]
    c.dta = c.dt * a_ref[...]
    c.incl = _cumsum_rows(tri_ref[...], c.dta)


def _ssd_weights(c, e_ref):
    L = SSM_CHUNK
    total = c.incl[L - 1:L, :]
    if c.backward:
        c.pos = c.incl - c.dta
        w_state = c.dt * jnp.exp(c.pos)
        w_out = jnp.exp(total - c.pos)
    else:
        c.pos = c.incl
        w_state = c.dt * jnp.exp(total - c.pos)
        w_out = jnp.exp(c.pos)
    e = e_ref[...]
    c.w_out_x = _expand(w_out, e)
    c.carry = c.w_out_x[0:1] if c.backward else c.w_out_x[L - 1:L]
    c.xs = (c.x.astype(F32) * _expand(w_state, e)).astype(BF16)
    c.b_t = c.bm.astype(F32).T.astype(BF16)
    c.cb = lax.dot_general(c.cm, c.bm, _NT, preferred_element_type=F32)
    c.pos_t = c.pos.T
    c.dt_t = c.dt.T


def _ssd_state(c, h_in):
    c.y = _dot(c.cm, h_in.astype(BF16)) * c.w_out_x
    return h_in * c.carry + _dot(c.b_t, c.xs)


def _ssd_diag(c, y_ref, *, hpg):
    L = SSM_CHUNK
    rows = slice(c.r0, c.r0 + L)
    li = lax.broadcasted_iota(jnp.int32, (L, L), 0)
    si = lax.broadcasted_iota(jnp.int32, (L, L), 1)
    mask = (si >= li) if c.backward else (li >= si)
    lane = lax.broadcasted_iota(jnp.int32, (L, 2 * SSM_HEAD_DIM), 1)
    for jp in range(hpg // 2):
        sl = slice(jp * 2 * SSM_HEAD_DIM, (jp + 1) * 2 * SSM_HEAD_DIM)
        xp = c.x[:, sl]
        outs = []
        for j in (c.lane0 + 2 * jp, c.lane0 + 2 * jp + 1):
            col = c.pos[:, j:j + 1]
            row = c.pos_t[j:j + 1, :]
            d = (row - col) if c.backward else (col - row)
            w = c.cb * jnp.exp(jnp.where(mask, d, -1e30)) * c.dt_t[j:j + 1, :]
            outs.append(_dot(w.astype(BF16), xp))
        y_ref[rows, sl] = (c.y[:, sl] + jnp.where(lane < SSM_HEAD_DIM, outs[0], outs[1])).astype(y_ref.dtype)


def _ssd_kernel(xf, bf, cf, dtf, xb, bb, cb, dtb, a_ref, tri_ref, ef_ref, eb_ref,
                yf_ref, yb_ref, hf_sc, hb_sc, *, hpg, cps):
    @pl.when(pl.program_id(2) == 0)
    def _():
        hf_sc[...] = jnp.zeros(hf_sc.shape, F32)
        hb_sc[...] = jnp.zeros(hb_sc.shape, F32)

    fwd, bwd = [], []
    for k in range(cps):
        f, b = _Chunk(), _Chunk()
        f.r0, f.lane0, f.backward = k * SSM_CHUNK, 0, False
        b.r0, b.lane0, b.backward = (cps - 1 - k) * SSM_CHUNK, hpg, True
        fwd.append(f)
        bwd.append(b)
    for f, b in zip(fwd, bwd):
        _ssd_positions(f, xf, bf, cf, dtf, a_ref, tri_ref)
        _ssd_positions(b, xb, bb, cb, dtb, a_ref, tri_ref)
    for f, b in zip(fwd, bwd):
        _ssd_weights(f, ef_ref)
        _ssd_weights(b, eb_ref)
    hf = hf_sc[...]
    hb = hb_sc[...]
    for f, b in zip(fwd, bwd):
        hf = _ssd_state(f, hf)
        hb = _ssd_state(b, hb)
    hf_sc[...] = hf
    hb_sc[...] = hb
    for f, b in zip(fwd, bwd):
        _ssd_diag(f, yf_ref, hpg=hpg)
        _ssd_diag(b, yb_ref, hpg=hpg)


def _ssd(xbc, dt, a_pad, *, batch, seq, d_inner, cps):
    t = xbc.shape[0]
    L = SSM_CHUNK
    rows = cps * L
    nc = seq // rows
    gw = d_inner // SSM_N_GROUPS
    hpg = gw // SSM_HEAD_DIM
    b0 = d_inner // SSM_D_STATE
    c0 = b0 + SSM_N_GROUPS
    tri = jnp.asarray(np.tril(np.ones((L, L))), BF16)
    head_of_lane = np.arange(gw) // SSM_HEAD_DIM
    ef = jnp.asarray(np.arange(LANES)[:, None] == head_of_lane[None, :], BF16)
    eb = jnp.asarray(np.arange(LANES)[:, None] == (head_of_lane[None, :] + hpg), BF16)

    def fwd(col):
        return lambda b, g, c: (b * nc + c, col(g))

    def bwd(col):
        return lambda b, g, c: (b * nc + (nc - 1 - c), col(g))

    def specs(order):
        return [pl.BlockSpec((rows, gw), order(lambda g: g)),
                pl.BlockSpec((rows, SSM_D_STATE), order(lambda g: b0 + g)),
                pl.BlockSpec((rows, SSM_D_STATE), order(lambda g: c0 + g)),
                pl.BlockSpec((rows, LANES), order(lambda g: g))]

    in_specs = specs(fwd) + specs(bwd) + [
        pl.BlockSpec((1, LANES), lambda b, g, c: (0, g)),
        _resident((L, L)), _resident((LANES, gw)), _resident((LANES, gw))]
    return pl.pallas_call(
        functools.partial(_ssd_kernel, hpg=hpg, cps=cps),
        grid=(batch, SSM_N_GROUPS, nc),
        in_specs=in_specs,
        out_specs=[pl.BlockSpec((rows, gw), fwd(lambda g: g)), pl.BlockSpec((rows, gw), bwd(lambda g: g))],
        out_shape=[jax.ShapeDtypeStruct((t, d_inner), BF16), jax.ShapeDtypeStruct((t, d_inner), BF16)],
        scratch_shapes=[pltpu.VMEM((SSM_D_STATE, gw), F32), pltpu.VMEM((SSM_D_STATE, gw), F32)],
        compiler_params=_params("parallel", "parallel", "arbitrary"),
        name="ssd_scan",
    )(xbc, xbc, xbc, dt, xbc, xbc, xbc, dt, a_pad, tri, ef, eb)


def _ssd_out_kernel(yf_ref, yb_ref, xs_ref, z_ref, dsk_ref, ng_ref, w_ref, x_ref, gp_ref, gn_ref,
                    xo_ref, ho_ref, acc, ss, *, d_inner):
    kk = pl.program_id(1)

    @pl.when(kk == 0)
    def _():
        acc[...] = jnp.zeros(acc.shape, F32)
        ss[...] = jnp.zeros(ss.shape, F32)

    y = yf_ref[...].astype(F32) + yb_ref[...].astype(F32) + xs_ref[...].astype(F32) * dsk_ref[...]
    y = y * _silu(z_ref[...].astype(F32))
    ss[...] += jnp.sum(y * y, axis=-1, keepdims=True)
    acc[...] += _dot((y * ng_ref[...]).astype(BF16), w_ref[...])

    @pl.when(kk == pl.num_programs(1) - 1)
    def _():
        m = acc[...] * lax.rsqrt(ss[...] * (1.0 / d_inner) + EPS)
        xn = x_ref[...] + _rms(m, gp_ref[...])
        xo_ref[...] = xn
        ho_ref[...] = _rms(xn, gn_ref[...]).astype(BF16)


def _ssd_out(yf, yb, xbc, z, d_exp, norm_g, w, x, g_post, g_next, *, tm, tk):
    t, d = x.shape
    d_inner = yf.shape[1]
    lhs = pl.BlockSpec((tm, tk), lambda i, k: (i, k))
    vec = pl.BlockSpec((1, tk), lambda i, k: (0, k))
    row = pl.BlockSpec((tm, d), lambda i, k: (i, 0))
    return pl.pallas_call(
        functools.partial(_ssd_out_kernel, d_inner=d_inner),
        grid=(t // tm, d_inner // tk),
        in_specs=[lhs, lhs, lhs, lhs, vec, vec, pl.BlockSpec((tk, d), lambda i, k: (k, 0)),
                  row, _resident((1, d)), _resident((1, d))],
        out_specs=[row, row],
        out_shape=[jax.ShapeDtypeStruct((t, d), F32), jax.ShapeDtypeStruct((t, d), BF16)],
        scratch_shapes=[pltpu.VMEM((tm, d), F32), pltpu.VMEM((tm, 1), F32)],
        compiler_params=_params("parallel", "arbitrary"),
        name="ssd_out",
    )(yf, yb, xbc, z, d_exp, norm_g, w, x, g_post, g_next)


def _rope_tables(seq):
    rows = seq // GRID_W
    row = jnp.repeat(jnp.arange(rows, dtype=F32), GRID_W)
    col = jnp.tile(jnp.arange(GRID_W, dtype=F32), rows)
    axis_dim = HEAD_DIM // 2
    inv_freq = ROPE_THETA ** (-jnp.arange(0, axis_dim, 2, dtype=F32) / axis_dim)
    ang = jnp.concatenate([row[:, None] * inv_freq, col[:, None] * inv_freq], axis=-1)
    cos, sin = jnp.cos(ang), jnp.sin(ang)
    return jnp.concatenate([cos, cos], axis=-1), jnp.concatenate([-sin, sin], axis=-1)


def _pick(n, pref):
    t = min(pref, n)
    while n % t:
        t -= LANES
    return t


def kernel(x_prompt, x_sample, mix_pre_g, mix_post_g, ffn_pre_g, ffn_post_g, fa_w_in, fa_q_gain, fa_k_gain, fa_w_out, ssd_w_in, ssd_conv_w, ssd_conv_b, ssd_dt_bias, ssd_a_log, ssd_d, ssd_norm_g, ssd_w_out, ffn_w_up, ffn_conv_w, ffn_conv_b, ffn_w_down):
    depth = mix_pre_g.shape[0]
    d = x_prompt.shape[2]
    aw = N_Q_HEADS * HEAD_DIM
    kvw = N_KV_HEADS * HEAD_DIM
    fw = fa_w_in.shape[2] - aw - 2 * kvw
    assert fa_w_out.shape[1] == fw + aw and fw % FNET_GROUP_DIM == 0
    d_ff = ffn_w_down.shape[1]
    d_inner = ssd_w_out.shape[1]
    n_heads = d_inner // SSM_HEAD_DIM
    hpg = n_heads // SSM_N_GROUPS
    bc_w = SSM_N_GROUPS * SSM_D_STATE
    conv_dim = d_inner + 2 * bc_w
    assert hpg % 2 == 0 and 2 * hpg <= LANES and depth % 2 == 0
    tm = 512
    vec = lambda v: v.reshape(1, -1).astype(F32)

    def regroup(p):
        p = p.astype(F32).reshape(2, SSM_N_GROUPS, hpg).transpose(1, 0, 2).reshape(SSM_N_GROUPS, 2 * hpg)
        return jnp.pad(p, ((0, 0), (0, LANES - 2 * hpg))).reshape(1, SSM_N_GROUPS * LANES)

    layers = []
    for i in range(depth):
        j = i // 2
        lw = dict(ffn_pre=vec(ffn_pre_g[i]), ffn_post=vec(ffn_post_g[i]), mix_pre=vec(mix_pre_g[i]),
                  mix_post=vec(mix_post_g[i]), w_up=ffn_w_up[i].astype(BF16), w_down=ffn_w_down[i].astype(BF16),
                  ffn_cw=ffn_conv_w[i].astype(F32), ffn_cb=vec(ffn_conv_b[i]))
        if i % 2 == 0:
            wo = fa_w_out[j].astype(BF16)
            lw.update(w_in=fa_w_in[j].astype(BF16), qg=vec(fa_q_gain[j]), kg=vec(fa_k_gain[j]),
                      wo_f=wo[:fw], wo_a=wo[fw:])
        else:
            w_dt = ssd_w_in[j][:, d_inner + conv_dim:].reshape(d, 2, SSM_N_GROUPS, hpg)
            w_dt = jnp.pad(w_dt.transpose(0, 2, 1, 3).reshape(d, SSM_N_GROUPS, 2 * hpg),
                           ((0, 0), (0, 0), (0, LANES - 2 * hpg))).reshape(d, SSM_N_GROUPS * LANES)
            lw.update(w_in=ssd_w_in[j].astype(BF16), w_dt=w_dt.astype(BF16), dt_bias=regroup(ssd_dt_bias[j]),
                      a_pad=regroup(-jnp.exp(ssd_a_log[j].astype(F32))), cw=ssd_conv_w[j].astype(F32),
                      cb=vec(ssd_conv_b[j]), d_exp=jnp.repeat(ssd_d[j].astype(F32), SSM_HEAD_DIM).reshape(1, d_inner),
                      norm_g=vec(ssd_norm_g[j]), w_out=ssd_w_out[j].astype(BF16))
        layers.append(lw)

    def trunk(xin):
        batch, seq, _ = xin.shape
        assert seq % tm == 0
        t = batch * seq
        x = xin.reshape(t, d)
        cosf, sinf = _rope_tables(seq)
        h = None
        for i, lw in enumerate(layers):
            if i % 2 == 0:
                u, q, k, v = _inproj0(x, lw["mix_pre"], lw["w_in"], lw["qg"], lw["kg"], cosf, sinf,
                                      seq=seq, fw=fw, aw=aw, kvw=kvw, tm=tm)
                a = _attention(q.reshape(batch, seq, aw), k.reshape(batch, seq, kvw), v.reshape(batch, seq, kvw),
                               tq=_pick(seq, 1024), tk=_pick(seq, 1024)).reshape(t, aw)
                f = _fourier_mix(u, batch=batch, seq=seq, tn=_pick(128 * fw, 8192))
                x, h = _proj_res([f, a], [lw["wo_f"], lw["wo_a"]], x, lw["mix_post"], lw["ffn_pre"], tm=tm, tk=fw)
            else:
                z = _conv_mm(h, lw["w_in"], None, None, act="none", seq=seq, tm=tm, tn=_pick(d_inner, 1024),
                             col0=0, ncols=d_inner)
                xbc = _conv_mm(h, lw["w_in"], lw["cw"], lw["cb"], act="silu", seq=seq, tm=tm,
                               tn=_pick(conv_dim, 1024), col0=d_inner, ncols=conv_dim)
                dt = _dt_proj(h, lw["w_dt"], lw["dt_bias"], tm=tm)
                yf, yb = _ssd(xbc, dt, lw["a_pad"], batch=batch, seq=seq, d_inner=d_inner, cps=8)
                x, h = _ssd_out(yf, yb, xbc, z, lw["d_exp"], lw["norm_g"], lw["w_out"], x,
                                lw["mix_post"], lw["ffn_pre"], tm=tm, tk=_pick(d_inner, 1024))
            g_next = layers[i + 1]["mix_pre"] if i + 1 < depth else None
            x, h = _ffn(h, lw["w_up"], lw["ffn_cw"], lw["ffn_cb"], lw["w_down"], x, lw["ffn_post"], g_next,
                        seq=seq, tm=tm, tn=_pick(d_ff, 512))
        return x.reshape(batch, seq, d)

    return (trunk(x_prompt), trunk(x_sample))
```

```python
import functools
import math

import jax
import jax.numpy as jnp
import numpy as np
from jax import lax
from jax.experimental import pallas as pl
from jax.experimental.pallas import tpu as pltpu

F32 = jnp.float32
BF16 = jnp.bfloat16

EPS = 1e-6
GRID_W = 64
HEAD_DIM = 128
N_Q_HEADS = 8
N_KV_HEADS = 2
FNET_GROUP_DIM = 128
ROPE_THETA = 10000.0
SSM_HEAD_DIM = 64
SSM_N_GROUPS = 8
SSM_D_STATE = 128
SSM_CHUNK = 128
LANES = 128
HALO = 16
CONV_STRIP = 512
VMEM_LIMIT = 56 * 1024 * 1024

_NT = (((1,), (1,)), ((), ()))


def _params(*sem):
    return pltpu.CompilerParams(dimension_semantics=sem, vmem_limit_bytes=VMEM_LIMIT)


def _resident(shape):
    nd = len(shape)
    return pl.BlockSpec(shape, lambda *_: (0,) * nd, pipeline_mode=pl.Buffered(1))


def _rms(x, g):
    return x * lax.rsqrt(jnp.mean(x * x, axis=-1, keepdims=True) + EPS) * g


def _dot(a, b):
    return jnp.dot(a, b, preferred_element_type=F32)


def _inproj0_kernel(x_ref, g_ref, w_ref, qg_ref, kg_ref, cos_ref, sin_ref,
                    u_ref, q_ref, k_ref, v_ref, *, fw, aw, kvw, scale):
    h = _rms(x_ref[...], g_ref[...]).astype(BF16)
    u_ref[...] = _dot(h, w_ref[:, :fw]).astype(BF16)
    cosf = cos_ref[...]
    sinf = sin_ref[...]

    def norm_rope(t, gain, mult):
        t = _rms(t, gain)
        t = t * cosf + pltpu.roll(t, HEAD_DIM // 2, 1) * sinf
        return (t * mult).astype(BF16)

    q = _dot(h, w_ref[:, fw:fw + aw])
    for hh in range(aw // HEAD_DIM):
        sl = slice(hh * HEAD_DIM, (hh + 1) * HEAD_DIM)
        q_ref[:, sl] = norm_rope(q[:, sl], qg_ref[...], scale)
    k = _dot(h, w_ref[:, fw + aw:fw + aw + kvw])
    for hh in range(kvw // HEAD_DIM):
        sl = slice(hh * HEAD_DIM, (hh + 1) * HEAD_DIM)
        k_ref[:, sl] = norm_rope(k[:, sl], kg_ref[...], 1.0)
    v_ref[...] = _dot(h, w_ref[:, fw + aw + kvw:]).astype(BF16)


def _inproj0(x, g, w, qg, kg, cosf, sinf, *, seq, fw, aw, kvw, tm):
    t, d = x.shape
    n = w.shape[1]
    tps = seq // tm
    kern = functools.partial(_inproj0_kernel, fw=fw, aw=aw, kvw=kvw, scale=math.log2(math.e) * HEAD_DIM ** -0.5)
    row = lambda width: pl.BlockSpec((tm, width), lambda i: (i, 0))
    return pl.pallas_call(
        kern,
        grid=(t // tm,),
        in_specs=[row(d), _resident((1, d)), _resident((d, n)),
                  _resident((1, HEAD_DIM)), _resident((1, HEAD_DIM)),
                  pl.BlockSpec((tm, HEAD_DIM), lambda i: (i % tps, 0)),
                  pl.BlockSpec((tm, HEAD_DIM), lambda i: (i % tps, 0))],
        out_specs=[row(fw), row(aw), row(kvw), row(kvw)],
        out_shape=[jax.ShapeDtypeStruct((t, fw), BF16), jax.ShapeDtypeStruct((t, aw), BF16),
                   jax.ShapeDtypeStruct((t, kvw), BF16), jax.ShapeDtypeStruct((t, kvw), BF16)],
        compiler_params=_params("parallel"),
        name="inproj0",
    )(x, g, w, qg, kg, cosf, sinf)


def _attn_kernel(q_ref, k_ref, v_ref, o_ref, m_sc, acc_sc, *, group):
    ki = pl.program_id(3)

    @pl.when(ki == 0)
    def _():
        m_sc[...] = jnp.full(m_sc.shape, -jnp.inf, F32)
        acc_sc[...] = jnp.zeros(acc_sc.shape, F32)

    k = k_ref[0]
    v = v_ref[0]
    v_ext = jnp.concatenate([v, jnp.ones_like(v)], axis=1)
    heads = range(group)
    s = [lax.dot_general(q_ref[0, :, j * HEAD_DIM:(j + 1) * HEAD_DIM], k, _NT,
                         preferred_element_type=F32).astype(BF16) for j in heads]
    p, alpha = [], []
    for j in heads:
        m_prev = m_sc[j]
        m_new = jnp.maximum(m_prev, jnp.max(s[j], axis=-1, keepdims=True).astype(F32))
        alpha.append(jnp.exp2(m_prev - m_new))
        p.append(jnp.exp2(s[j] - m_new.astype(BF16)))
        m_sc[j] = m_new
    for j in heads:
        acc_sc[j] = alpha[j] * acc_sc[j] + _dot(p[j], v_ext)

    @pl.when(ki == pl.num_programs(3) - 1)
    def _():
        for j in heads:
            acc = acc_sc[j]
            o_ref[0, :, j * HEAD_DIM:(j + 1) * HEAD_DIM] = (acc[:, :HEAD_DIM] / acc[:, HEAD_DIM:]).astype(BF16)


def _attention(q, k, v, *, tq, tk):
    b, s, aw = q.shape
    group = aw // HEAD_DIM // N_KV_HEADS
    gw = group * HEAD_DIM
    return pl.pallas_call(
        functools.partial(_attn_kernel, group=group),
        grid=(b, N_KV_HEADS, s // tq, s // tk),
        in_specs=[pl.BlockSpec((1, tq, gw), lambda bi, h, qi, ki: (bi, qi, h)),
                  pl.BlockSpec((1, tk, HEAD_DIM), lambda bi, h, qi, ki: (bi, ki, h)),
                  pl.BlockSpec((1, tk, HEAD_DIM), lambda bi, h, qi, ki: (bi, ki, h))],
        out_specs=pl.BlockSpec((1, tq, gw), lambda bi, h, qi, ki: (bi, qi, h)),
        out_shape=jax.ShapeDtypeStruct((b, s, aw), BF16),
        scratch_shapes=[pltpu.VMEM((group, tq, 1), F32), pltpu.VMEM((group, tq, 2 * HEAD_DIM), F32)],
        compiler_params=_params("parallel", "parallel", "parallel", "arbitrary"),
        name="attention",
    )(q, k, v)


def _fourier1_kernel(f_ref, u_ref, y_ref, *, n1):
    y = _dot(f_ref[...], u_ref[0])
    y_ref[0, 0] = y[:n1].astype(BF16)
    y_ref[0, 1] = y[n1:].astype(BF16)


def _fourier2_kernel(m_ref, y_ref, cc_ref, sc_ref, o_ref, *, n2, groups):
    y = y_ref[0].reshape(2 * n2, groups * FNET_GROUP_DIM)
    g = _dot(m_ref[0], y)
    gr = g[:n2].astype(BF16)
    gi = g[n2:].astype(BF16)
    for c in range(groups):
        sl = slice(c * FNET_GROUP_DIM, (c + 1) * FNET_GROUP_DIM)
        o_ref[0, :, sl] = (_dot(gr[:, sl], cc_ref[...]) + _dot(gi[:, sl], sc_ref[...])).astype(BF16)


def _fourier_tables(seq):
    n2 = 128
    n1 = seq // n2
    a1 = 2.0 * np.pi * np.outer(np.arange(n1), np.arange(n1)) / n1
    f1 = np.concatenate([np.cos(a1), -np.sin(a1)], axis=0)
    kk = np.arange(n1)[:, None, None] + n1 * np.arange(n2)[None, :, None]
    th = 2.0 * np.pi * (kk * np.arange(n2)[None, None, :] % seq) / seq
    mc, ms = np.cos(th), np.sin(th)
    m = np.concatenate([np.concatenate([mc, ms], axis=2),
                        np.concatenate([-ms, mc], axis=2)], axis=1)
    ac = 2.0 * np.pi * np.outer(np.arange(FNET_GROUP_DIM), np.arange(FNET_GROUP_DIM)) / FNET_GROUP_DIM
    norm = 1.0 / math.sqrt(seq * FNET_GROUP_DIM)
    return (jnp.asarray(f1, BF16), jnp.asarray(m, BF16),
            jnp.asarray(np.cos(ac) * norm, BF16), jnp.asarray(np.sin(ac) * norm, BF16))


def _fourier_mix(u, *, batch, seq, tn):
    t, width = u.shape
    n2 = 128
    n1 = seq // n2
    groups = width // FNET_GROUP_DIM
    f1, m, cc, sc = _fourier_tables(seq)
    cols = n2 * width
    y = pl.pallas_call(
        functools.partial(_fourier1_kernel, n1=n1),
        grid=(batch, cols // tn),
        in_specs=[_resident((2 * n1, n1)),
                  pl.BlockSpec((1, n1, tn), lambda b, j: (b, 0, j))],
        out_specs=pl.BlockSpec((1, 2, n1, tn), lambda b, j: (b, 0, 0, j)),
        out_shape=jax.ShapeDtypeStruct((batch, 2, n1, cols), BF16),
        compiler_params=_params("parallel", "parallel"),
        name="fourier_seq1",
    )(f1, u.reshape(batch, n1, cols))
    y = y.reshape(batch, 2, seq, width)
    out = pl.pallas_call(
        functools.partial(_fourier2_kernel, n2=n2, groups=groups),
        grid=(n1, batch),
        in_specs=[pl.BlockSpec((1, 2 * n2, 2 * n2), lambda k1, b: (k1, 0, 0)),
                  pl.BlockSpec((1, 2, n2, width), lambda k1, b: (b, 0, k1, 0)),
                  _resident((FNET_GROUP_DIM, FNET_GROUP_DIM)),
                  _resident((FNET_GROUP_DIM, FNET_GROUP_DIM))],
        out_specs=pl.BlockSpec((1, n2, width), lambda k1, b: (b, 0, k1)),
        out_shape=jax.ShapeDtypeStruct((batch, n2, n1 * width), BF16),
        compiler_params=_params("parallel", "parallel"),
        name="fourier_seq2",
    )(m, y, cc, sc)
    return out.reshape(t, width)


def _proj_res_kernel(*refs, n_lhs, has_next):
    lhs = refs[:n_lhs]
    ws = refs[n_lhs:2 * n_lhs]
    x_ref, gp_ref = refs[2 * n_lhs:2 * n_lhs + 2]
    pos = 2 * n_lhs + 2
    gn_ref = refs[pos] if has_next else None
    pos += int(has_next)
    xo_ref = refs[pos]
    ho_ref = refs[pos + 1] if has_next else None
    acc = refs[-1]
    kk = pl.program_id(1)

    @pl.when(kk == 0)
    def _():
        acc[...] = jnp.zeros(acc.shape, F32)

    part = _dot(lhs[0][...], ws[0][...])
    for a, w in zip(lhs[1:], ws[1:]):
        part += _dot(a[...], w[...])
    acc[...] += part

    @pl.when(kk == pl.num_programs(1) - 1)
    def _():
        xn = x_ref[...] + _rms(acc[...], gp_ref[...])
        xo_ref[...] = xn
        if has_next:
            ho_ref[...] = _rms(xn, gn_ref[...]).astype(BF16)


def _proj_res(lhs, ws, x, g_post, g_next, *, tm, tk):
    t, d = x.shape
    n_lhs = len(lhs)
    kdim = lhs[0].shape[1]
    has_next = g_next is not None
    in_specs = [pl.BlockSpec((tm, tk), lambda i, k: (i, k)) for _ in lhs]
    in_specs += [pl.BlockSpec((tk, d), lambda i, k: (k, 0)) for _ in ws]
    in_specs += [pl.BlockSpec((tm, d), lambda i, k: (i, 0)), _resident((1, d))]
    args = list(lhs) + list(ws) + [x, g_post]
    out_specs = [pl.BlockSpec((tm, d), lambda i, k: (i, 0))]
    out_shape = [jax.ShapeDtypeStruct((t, d), F32)]
    if has_next:
        in_specs.append(_resident((1, d)))
        args.append(g_next)
        out_specs.append(pl.BlockSpec((tm, d), lambda i, k: (i, 0)))
        out_shape.append(jax.ShapeDtypeStruct((t, d), BF16))
    res = pl.pallas_call(
        functools.partial(_proj_res_kernel, n_lhs=n_lhs, has_next=has_next),
        grid=(t // tm, kdim // tk),
        in_specs=in_specs, out_specs=out_specs, out_shape=out_shape,
        scratch_shapes=[pltpu.VMEM((tm, d), F32)],
        compiler_params=_params("parallel", "arbitrary"),
        name="proj_res",
    )(*args)
    return res if has_next else (res[0], None)


def _gelu_tanh(x):
    return 0.5 * x * (1.0 + jnp.tanh(math.sqrt(2.0 / math.pi) * (x + 0.044715 * x * x * x)))


def _silu(x):
    return x * (1.0 / (1.0 + jnp.exp(-x)))


def _assemble_halo(hext, h_ref, hp_ref, hn_ref, *, tm, tps):
    i = pl.program_id(0)
    first = (i % tps) == 0
    last = (i % tps) == tps - 1
    hext[0:HALO] = jnp.where(first, jnp.zeros_like(hp_ref[...]), hp_ref[...])
    hext[HALO:HALO + tm] = h_ref[...]
    hext[HALO + tm:] = jnp.where(last, jnp.zeros_like(hn_ref[...]), hn_ref[...])


def _conv_rows(u, cw, cb, *, tm):
    rows = tm + 2 * HALO
    c = cb + cw[0:1] * pltpu.roll(u, 1, 0) + cw[1:2] * u + cw[2:3] * pltpu.roll(u, rows - 1, 0)
    return c[HALO:HALO + tm]


def _halo_specs(t, tm, kdim):
    hb = tm // HALO
    last_blk = t // HALO - 1
    return [pl.BlockSpec((tm, kdim), lambda i, j: (i, 0)),
            pl.BlockSpec((HALO, kdim), lambda i, j: (jnp.maximum(i * hb - 1, 0), 0)),
            pl.BlockSpec((HALO, kdim), lambda i, j: (jnp.minimum((i + 1) * hb, last_blk), 0))]


def _inproj1_kernel(h_ref, hp_ref, hn_ref, w_ref, cw_ref, cb_ref, z_ref, xbc_ref, hext, *, tm, tps, nz, sub):
    j = pl.program_id(1)
    tn = w_ref.shape[1]

    @pl.when(j == 0)
    def _():
        _assemble_halo(hext, h_ref, hp_ref, hn_ref, tm=tm, tps=tps)

    @pl.when(j < nz)
    def _():
        h = hext[HALO:HALO + tm, :]
        us = [_dot(h, w_ref[:, c0:c0 + sub]) for c0 in range(0, tn, sub)]
        for k, u in enumerate(us):
            z_ref[:, k * sub:(k + 1) * sub] = u.astype(BF16)

    @pl.when(j >= nz)
    def _():
        hx = hext[...]
        us = [_dot(hx, w_ref[:, c0:c0 + sub]) for c0 in range(0, tn, sub)]
        for k, u in enumerate(us):
            cols = slice(k * sub, (k + 1) * sub)
            y = _conv_rows(u, cw_ref[:, cols], cb_ref[:, cols], tm=tm)
            xbc_ref[:, cols] = _silu(y).astype(BF16)


def _inproj1(h, w, conv_w, conv_b, *, seq, tm, tn, sub, d_inner, conv_dim):
    t, kdim = h.shape
    nz = d_inner // tn
    nx = conv_dim // tn

    def xtile(j):
        return jnp.clip(j - nz, 0, nx - 1)

    in_specs = _halo_specs(t, tm, kdim) + [
        pl.BlockSpec((kdim, tn), lambda i, j: (0, j)),
        pl.BlockSpec((3, tn), lambda i, j: (0, xtile(j))),
        pl.BlockSpec((1, tn), lambda i, j: (0, xtile(j)))]
    return pl.pallas_call(
        functools.partial(_inproj1_kernel, tm=tm, tps=seq // tm, nz=nz, sub=sub),
        grid=(t // tm, nz + nx),
        in_specs=in_specs,
        out_specs=[pl.BlockSpec((tm, tn), lambda i, j: (i, jnp.minimum(j, nz - 1))),
                   pl.BlockSpec((tm, tn), lambda i, j: (i, xtile(j)))],
        out_shape=[jax.ShapeDtypeStruct((t, d_inner), BF16), jax.ShapeDtypeStruct((t, conv_dim), BF16)],
        scratch_shapes=[pltpu.VMEM((tm + 2 * HALO, kdim), BF16)],
        compiler_params=_params("parallel", "arbitrary"),
        name="inproj1",
    )(h, h, h, w, conv_w, conv_b)


def _ffn_kernel(*refs, tm, tps, nj, has_next):
    (h_ref, hp_ref, hn_ref, wg_ref, wv_ref, cwg_ref, cwv_ref, cbg_ref, cbv_ref,
     wd_ref, wdl_ref, x_ref, gp_ref) = refs[:13]
    pos = 13
    gn_ref = refs[pos] if has_next else None
    pos += int(has_next)
    xo_ref = refs[pos]
    ho_ref = refs[pos + 1] if has_next else None
    hext, act_a, act_b, acc = refs[-4:]
    j = pl.program_id(1)

    @pl.when(j == 0)
    def _():
        _assemble_halo(hext, h_ref, hp_ref, hn_ref, tm=tm, tps=tps)
        act_b[...] = jnp.zeros(act_b.shape, BF16)
        acc[...] = jnp.zeros(acc.shape, F32)

    def step(act_prev, act_cur):
        hx = hext[...]
        ug = _dot(hx, wg_ref[...])
        uv = _dot(hx, wv_ref[...])
        acc[...] += _dot(act_prev[...], wd_ref[...])
        gate = _conv_rows(ug, cwg_ref[...], cbg_ref[...], tm=tm)
        val = _conv_rows(uv, cwv_ref[...], cbv_ref[...], tm=tm)
        act_cur[...] = (_gelu_tanh(gate) * val).astype(BF16)

    @pl.when(j % 2 == 0)
    def _():
        step(act_b, act_a)

    @pl.when(j % 2 == 1)
    def _():
        step(act_a, act_b)

    @pl.when(j == nj - 1)
    def _():
        act_last = act_a if (nj - 1) % 2 == 0 else act_b
        m = acc[...] + _dot(act_last[...], wdl_ref[...])
        xn = x_ref[...] + _rms(m, gp_ref[...])
        xo_ref[...] = xn
        if has_next:
            ho_ref[...] = _rms(xn, gn_ref[...]).astype(BF16)


def _ffn(h, w_up, conv_w, conv_b, w_down, x, g_post, g_next, *, seq, tm, tn):
    t, d = x.shape
    d_ff = w_down.shape[0]
    nj = d_ff // tn
    has_next = g_next is not None
    row = pl.BlockSpec((tm, d), lambda i, j: (i, 0))
    in_specs = _halo_specs(t, tm, d) + [
        pl.BlockSpec((d, tn), lambda i, j: (0, j)),
        pl.BlockSpec((d, tn), lambda i, j: (0, nj + j)),
        pl.BlockSpec((3, tn), lambda i, j: (0, j)),
        pl.BlockSpec((3, tn), lambda i, j: (0, nj + j)),
        pl.BlockSpec((1, tn), lambda i, j: (0, j)),
        pl.BlockSpec((1, tn), lambda i, j: (0, nj + j)),
        pl.BlockSpec((tn, d), lambda i, j: (jnp.maximum(j - 1, 0), 0)),
        pl.BlockSpec((tn, d), lambda i, j: (nj - 1, 0), pipeline_mode=pl.Buffered(1)),
        row, _resident((1, d))]
    args = [h, h, h, w_up, w_up, conv_w, conv_w, conv_b, conv_b, w_down, w_down, x, g_post]
    out_specs = [row]
    out_shape = [jax.ShapeDtypeStruct((t, d), F32)]
    if has_next:
        in_specs.append(_resident((1, d)))
        args.append(g_next)
        out_specs.append(row)
        out_shape.append(jax.ShapeDtypeStruct((t, d), BF16))
    res = pl.pallas_call(
        functools.partial(_ffn_kernel, tm=tm, tps=seq // tm, nj=nj, has_next=has_next),
        grid=(t // tm, nj),
        in_specs=in_specs, out_specs=out_specs, out_shape=out_shape,
        scratch_shapes=[pltpu.VMEM((tm + 2 * HALO, d), BF16), pltpu.VMEM((tm, tn), BF16),
                        pltpu.VMEM((tm, tn), BF16), pltpu.VMEM((tm, d), F32)],
        compiler_params=_params("parallel", "arbitrary"),
        name="ffn",
    )(*args)
    return res if has_next else (res[0], None)


def _dt_kernel(h_ref, w_ref, b_ref, o_ref):
    r = _dot(h_ref[...], w_ref[...]) + b_ref[...]
    o_ref[...] = jnp.maximum(r, 0.0) + jnp.log(1.0 + jnp.exp(-jnp.abs(r)))


def _dt_proj(h, w, b, *, tm):
    t, kdim = h.shape
    n = w.shape[1]
    return pl.pallas_call(
        _dt_kernel,
        grid=(t // tm,),
        in_specs=[pl.BlockSpec((tm, kdim), lambda i: (i, 0)), _resident((kdim, n)), _resident((1, n))],
        out_specs=pl.BlockSpec((tm, n), lambda i: (i, 0)),
        out_shape=jax.ShapeDtypeStruct((t, n), F32),
        compiler_params=_params("parallel"),
        name="ssd_dt",
    )(h, w, b)


def _split3(x):
    hi = x.astype(BF16)
    r = x - hi.astype(F32)
    mid = r.astype(BF16)
    lo = (r - mid.astype(F32)).astype(BF16)
    return hi, mid, lo


def _cumsum_rows(tri, x):
    hi, mid, lo = _split3(x)
    return _dot(tri, hi) + _dot(tri, mid) + _dot(tri, lo)


def _expand(x, e):
    return _dot(x.astype(BF16), e)


class _Chunk:
    pass


def _ssd_positions(c, x_ref, b_ref, c_ref, dt_ref, a_ref, tri_ref):
    rows = slice(c.r0, c.r0 + SSM_CHUNK)
    c.x = x_ref[rows, :]
    c.bm = b_ref[rows, :]
    c.cm = c_ref[rows, :]
    c.dt = dt_ref[rows, :]
    c.dta = c.dt * a_ref[...]
    c.incl = _cumsum_rows(tri_ref[...], c.dta)


def _ssd_weights(c, e_ref):
    L = SSM_CHUNK
    total = c.incl[L - 1:L, :]
    if c.backward:
        c.pos = c.incl - c.dta
        w_state = c.dt * jnp.exp(c.pos)
        w_out = jnp.exp(total - c.pos)
    else:
        c.pos = c.incl
        w_state = c.dt * jnp.exp(total - c.pos)
        w_out = jnp.exp(c.pos)
    e = e_ref[...]
    c.w_out_x = _expand(w_out, e)
    c.carry = c.w_out_x[0:1] if c.backward else c.w_out_x[L - 1:L]
    c.xs = (c.x.astype(F32) * _expand(w_state, e)).astype(BF16)
    c.b_t = c.bm.astype(F32).T.astype(BF16)
    c.cb = lax.dot_general(c.cm, c.bm, _NT, preferred_element_type=F32)
    c.pos_t = c.pos.T
    c.dt_t = c.dt.T


def _ssd_state(c, h_in):
    c.y = _dot(c.cm, h_in.astype(BF16)) * c.w_out_x
    return h_in * c.carry + _dot(c.b_t, c.xs)


def _ssd_diag(c, y_ref, *, hpg):
    L = SSM_CHUNK
    rows = slice(c.r0, c.r0 + L)
    li = lax.broadcasted_iota(jnp.int32, (L, L), 0)
    si = lax.broadcasted_iota(jnp.int32, (L, L), 1)
    mask = (si >= li) if c.backward else (li >= si)
    lane = lax.broadcasted_iota(jnp.int32, (L, 2 * SSM_HEAD_DIM), 1)
    for jp in range(hpg // 2):
        sl = slice(jp * 2 * SSM_HEAD_DIM, (jp + 1) * 2 * SSM_HEAD_DIM)
        xp = c.x[:, sl]
        outs = []
        for j in (c.lane0 + 2 * jp, c.lane0 + 2 * jp + 1):
            col = c.pos[:, j:j + 1]
            row = c.pos_t[j:j + 1, :]
            d = (row - col) if c.backward else (col - row)
            w = c.cb * jnp.exp(jnp.where(mask, d, -1e30)) * c.dt_t[j:j + 1, :]
            outs.append(_dot(w.astype(BF16), xp))
        y_ref[rows, sl] = (c.y[:, sl] + jnp.where(lane < SSM_HEAD_DIM, outs[0], outs[1])).astype(y_ref.dtype)


def _ssd_kernel(xf, bf, cf, dtf, xb, bb, cb, dtb, a_ref, tri_ref, ef_ref, eb_ref,
                yf_ref, yb_ref, hf_sc, hb_sc, *, hpg, cps):
    @pl.when(pl.program_id(2) == 0)
    def _():
        hf_sc[...] = jnp.zeros(hf_sc.shape, F32)
        hb_sc[...] = jnp.zeros(hb_sc.shape, F32)

    fwd, bwd = [], []
    for k in range(cps):
        f, b = _Chunk(), _Chunk()
        f.r0, f.lane0, f.backward = k * SSM_CHUNK, 0, False
        b.r0, b.lane0, b.backward = (cps - 1 - k) * SSM_CHUNK, hpg, True
        fwd.append(f)
        bwd.append(b)
    for f, b in zip(fwd, bwd):
        _ssd_positions(f, xf, bf, cf, dtf, a_ref, tri_ref)
        _ssd_positions(b, xb, bb, cb, dtb, a_ref, tri_ref)
    for f, b in zip(fwd, bwd):
        _ssd_weights(f, ef_ref)
        _ssd_weights(b, eb_ref)
    hf = hf_sc[...]
    hb = hb_sc[...]
    for f, b in zip(fwd, bwd):
        hf = _ssd_state(f, hf)
        hb = _ssd_state(b, hb)
    hf_sc[...] = hf
    hb_sc[...] = hb
    for f, b in zip(fwd, bwd):
        _ssd_diag(f, yf_ref, hpg=hpg)
        _ssd_diag(b, yb_ref, hpg=hpg)


def _ssd(xbc, dt, a_pad, *, batch, seq, d_inner, cps):
    t = xbc.shape[0]
    L = SSM_CHUNK
    rows = cps * L
    nc = seq // rows
    gw = d_inner // SSM_N_GROUPS
    hpg = gw // SSM_HEAD_DIM
    b0 = d_inner // SSM_D_STATE
    c0 = b0 + SSM_N_GROUPS
    tri = jnp.asarray(np.tril(np.ones((L, L))), BF16)
    head_of_lane = np.arange(gw) // SSM_HEAD_DIM
    ef = jnp.asarray(np.arange(LANES)[:, None] == head_of_lane[None, :], BF16)
    eb = jnp.asarray(np.arange(LANES)[:, None] == (head_of_lane[None, :] + hpg), BF16)

    def fwd(col):
        return lambda b, g, c: (b * nc + c, col(g))

    def bwd(col):
        return lambda b, g, c: (b * nc + (nc - 1 - c), col(g))

    def specs(order):
        return [pl.BlockSpec((rows, gw), order(lambda g: g)),
                pl.BlockSpec((rows, SSM_D_STATE), order(lambda g: b0 + g)),
                pl.BlockSpec((rows, SSM_D_STATE), order(lambda g: c0 + g)),
                pl.BlockSpec((rows, LANES), order(lambda g: g))]

    in_specs = specs(fwd) + specs(bwd) + [
        pl.BlockSpec((1, LANES), lambda b, g, c: (0, g)),
        _resident((L, L)), _resident((LANES, gw)), _resident((LANES, gw))]
    return pl.pallas_call(
        functools.partial(_ssd_kernel, hpg=hpg, cps=cps),
        grid=(batch, SSM_N_GROUPS, nc),
        in_specs=in_specs,
        out_specs=[pl.BlockSpec((rows, gw), fwd(lambda g: g)), pl.BlockSpec((rows, gw), bwd(lambda g: g))],
        out_shape=[jax.ShapeDtypeStruct((t, d_inner), BF16), jax.ShapeDtypeStruct((t, d_inner), BF16)],
        scratch_shapes=[pltpu.VMEM((SSM_D_STATE, gw), F32), pltpu.VMEM((SSM_D_STATE, gw), F32)],
        compiler_params=_params("parallel", "parallel", "arbitrary"),
        name="ssd_scan",
    )(xbc, xbc, xbc, dt, xbc, xbc, xbc, dt, a_pad, tri, ef, eb)


def _ssd_out_kernel(yf_ref, yb_ref, xs_ref, z_ref, dsk_ref, ng_ref, w_ref, x_ref, gp_ref, gn_ref,
                    xo_ref, ho_ref, acc, ss, *, d_inner):
    kk = pl.program_id(1)

    @pl.when(kk == 0)
    def _():
        acc[...] = jnp.zeros(acc.shape, F32)
        ss[...] = jnp.zeros(ss.shape, F32)

    y = yf_ref[...].astype(F32) + yb_ref[...].astype(F32) + xs_ref[...].astype(F32) * dsk_ref[...]
    y = y * _silu(z_ref[...].astype(F32))
    ss[...] += jnp.sum(y * y, axis=-1, keepdims=True)
    acc[...] += _dot((y * ng_ref[...]).astype(BF16), w_ref[...])

    @pl.when(kk == pl.num_programs(1) - 1)
    def _():
        m = acc[...] * lax.rsqrt(ss[...] * (1.0 / d_inner) + EPS)
        xn = x_ref[...] + _rms(m, gp_ref[...])
        xo_ref[...] = xn
        ho_ref[...] = _rms(xn, gn_ref[...]).astype(BF16)


def _ssd_out(yf, yb, xbc, z, d_exp, norm_g, w, x, g_post, g_next, *, tm, tk):
    t, d = x.shape
    d_inner = yf.shape[1]
    lhs = pl.BlockSpec((tm, tk), lambda i, k: (i, k))
    vec = pl.BlockSpec((1, tk), lambda i, k: (0, k))
    row = pl.BlockSpec((tm, d), lambda i, k: (i, 0))
    return pl.pallas_call(
        functools.partial(_ssd_out_kernel, d_inner=d_inner),
        grid=(t // tm, d_inner // tk),
        in_specs=[lhs, lhs, lhs, lhs, vec, vec, pl.BlockSpec((tk, d), lambda i, k: (k, 0)),
                  row, _resident((1, d)), _resident((1, d))],
        out_specs=[row, row],
        out_shape=[jax.ShapeDtypeStruct((t, d), F32), jax.ShapeDtypeStruct((t, d), BF16)],
        scratch_shapes=[pltpu.VMEM((tm, d), F32), pltpu.VMEM((tm, 1), F32)],
        compiler_params=_params("parallel", "arbitrary"),
        name="ssd_out",
    )(yf, yb, xbc, z, d_exp, norm_g, w, x, g_post, g_next)


def _rope_tables(seq):
    rows = seq // GRID_W
    row = jnp.repeat(jnp.arange(rows, dtype=F32), GRID_W)
    col = jnp.tile(jnp.arange(GRID_W, dtype=F32), rows)
    axis_dim = HEAD_DIM // 2
    inv_freq = ROPE_THETA ** (-jnp.arange(0, axis_dim, 2, dtype=F32) / axis_dim)
    ang = jnp.concatenate([row[:, None] * inv_freq, col[:, None] * inv_freq], axis=-1)
    cos, sin = jnp.cos(ang), jnp.sin(ang)
    return jnp.concatenate([cos, cos], axis=-1), jnp.concatenate([-sin, sin], axis=-1)


def _pick(n, pref):
    t = min(pref, n)
    while n % t:
        t -= LANES
    return t


def kernel(x_prompt, x_sample, mix_pre_g, mix_post_g, ffn_pre_g, ffn_post_g, fa_w_in, fa_q_gain, fa_k_gain, fa_w_out, ssd_w_in, ssd_conv_w, ssd_conv_b, ssd_dt_bias, ssd_a_log, ssd_d, ssd_norm_g, ssd_w_out, ffn_w_up, ffn_conv_w, ffn_conv_b, ffn_w_down):
    depth = mix_pre_g.shape[0]
    d = x_prompt.shape[2]
    aw = N_Q_HEADS * HEAD_DIM
    kvw = N_KV_HEADS * HEAD_DIM
    fw = fa_w_in.shape[2] - aw - 2 * kvw
    assert fa_w_out.shape[1] == fw + aw and fw % FNET_GROUP_DIM == 0
    d_ff = ffn_w_down.shape[1]
    d_inner = ssd_w_out.shape[1]
    n_heads = d_inner // SSM_HEAD_DIM
    hpg = n_heads // SSM_N_GROUPS
    bc_w = SSM_N_GROUPS * SSM_D_STATE
    conv_dim = d_inner + 2 * bc_w
    assert hpg % 2 == 0 and 2 * hpg <= LANES and depth % 2 == 0
    tm = 512
    vec = lambda v: v.reshape(1, -1).astype(F32)

    def regroup(p):
        p = p.astype(F32).reshape(2, SSM_N_GROUPS, hpg).transpose(1, 0, 2).reshape(SSM_N_GROUPS, 2 * hpg)
        return jnp.pad(p, ((0, 0), (0, LANES - 2 * hpg))).reshape(1, SSM_N_GROUPS * LANES)

    layers = []
    for i in range(depth):
        j = i // 2
        lw = dict(ffn_pre=vec(ffn_pre_g[i]), ffn_post=vec(ffn_post_g[i]), mix_pre=vec(mix_pre_g[i]),
                  mix_post=vec(mix_post_g[i]), w_up=ffn_w_up[i].astype(BF16), w_down=ffn_w_down[i].astype(BF16),
                  ffn_cw=ffn_conv_w[i].astype(F32), ffn_cb=vec(ffn_conv_b[i]))
        if i % 2 == 0:
            wo = fa_w_out[j].astype(BF16)
            lw.update(w_in=fa_w_in[j].astype(BF16), qg=vec(fa_q_gain[j]), kg=vec(fa_k_gain[j]),
                      wo_f=wo[:fw], wo_a=wo[fw:])
        else:
            w_dt = ssd_w_in[j][:, d_inner + conv_dim:].reshape(d, 2, SSM_N_GROUPS, hpg)
            w_dt = jnp.pad(w_dt.transpose(0, 2, 1, 3).reshape(d, SSM_N_GROUPS, 2 * hpg),
                           ((0, 0), (0, 0), (0, LANES - 2 * hpg))).reshape(d, SSM_N_GROUPS * LANES)
            lw.update(w_in=ssd_w_in[j].astype(BF16), w_dt=w_dt.astype(BF16), dt_bias=regroup(ssd_dt_bias[j]),
                      a_pad=regroup(-jnp.exp(ssd_a_log[j].astype(F32))), cw=ssd_conv_w[j].astype(F32),
                      cb=vec(ssd_conv_b[j]), d_exp=jnp.repeat(ssd_d[j].astype(F32), SSM_HEAD_DIM).reshape(1, d_inner),
                      norm_g=vec(ssd_norm_g[j]), w_out=ssd_w_out[j].astype(BF16))
        layers.append(lw)

    def trunk(xin):
        batch, seq, _ = xin.shape
        assert seq % tm == 0
        t = batch * seq
        x = xin.reshape(t, d)
        cosf, sinf = _rope_tables(seq)
        h = None
        for i, lw in enumerate(layers):
            if i % 2 == 0:
                u, q, k, v = _inproj0(x, lw["mix_pre"], lw["w_in"], lw["qg"], lw["kg"], cosf, sinf,
                                      seq=seq, fw=fw, aw=aw, kvw=kvw, tm=tm)
                a = _attention(q.reshape(batch, seq, aw), k.reshape(batch, seq, kvw), v.reshape(batch, seq, kvw),
                               tq=_pick(seq, 1024), tk=_pick(seq, 1024)).reshape(t, aw)
                f = _fourier_mix(u, batch=batch, seq=seq, tn=_pick(128 * fw, 8192))
                x, h = _proj_res([f, a], [lw["wo_f"], lw["wo_a"]], x, lw["mix_post"], lw["ffn_pre"], tm=tm, tk=fw)
            else:
                z, xbc = _inproj1(h, lw["w_in"], lw["cw"], lw["cb"], seq=seq, tm=tm,
                                  tn=math.gcd(d_inner, conv_dim), sub=512, d_inner=d_inner, conv_dim=conv_dim)
                dt = _dt_proj(h, lw["w_dt"], lw["dt_bias"], tm=tm)
                yf, yb = _ssd(xbc, dt, lw["a_pad"], batch=batch, seq=seq, d_inner=d_inner, cps=8)
                x, h = _ssd_out(yf, yb, xbc, z, lw["d_exp"], lw["norm_g"], lw["w_out"], x,
                                lw["mix_post"], lw["ffn_pre"], tm=tm, tk=_pick(d_inner, 1024))
            g_next = layers[i + 1]["mix_pre"] if i + 1 < depth else None
            x, h = _ffn(h, lw["w_up"], lw["ffn_cw"], lw["ffn_cb"], lw["w_down"], x, lw["ffn_post"], g_next,
                        seq=seq, tm=tm, tn=_pick(d_ff, 512))
        return x.reshape(batch, seq, d)

    return (trunk(x_prompt), trunk(x_sample))
```

```python
import functools
import math

import jax
import jax.numpy as jnp
import numpy as np
from jax import lax
from jax.experimental import pallas as pl
from jax.experimental.pallas import tpu as pltpu

F32 = jnp.float32
BF16 = jnp.bfloat16

EPS = 1e-6
GRID_W = 64
HEAD_DIM = 128
N_Q_HEADS = 8
N_KV_HEADS = 2
FNET_GROUP_DIM = 128
ROPE_THETA = 10000.0
SSM_HEAD_DIM = 64
SSM_N_GROUPS = 8
SSM_D_STATE = 128
SSM_CHUNK = 128
LANES = 128
HALO = 16
OUT_SUB = 256
VMEM_LIMIT = 56 * 1024 * 1024

_NT = (((1,), (1,)), ((), ()))


def _params(*sem):
    return pltpu.CompilerParams(dimension_semantics=sem, vmem_limit_bytes=VMEM_LIMIT)


def _resident(shape):
    nd = len(shape)
    return pl.BlockSpec(shape, lambda *_: (0,) * nd, pipeline_mode=pl.Buffered(1))


def _rms(x, g):
    return x * lax.rsqrt(jnp.mean(x * x, axis=-1, keepdims=True) + EPS) * g


def _dot(a, b):
    return jnp.dot(a, b, preferred_element_type=F32)


def _inproj0_kernel(x_ref, g_ref, w_ref, qg_ref, kg_ref, cos_ref, sin_ref,
                    u_ref, q_ref, k_ref, v_ref, *, fw, aw, kvw, scale):
    h = _rms(x_ref[...], g_ref[...]).astype(BF16)
    cosf = cos_ref[...]
    sinf = sin_ref[...]

    def norm_rope(t, gain, mult):
        t = _rms(t, gain)
        t = t * cosf + pltpu.roll(t, HEAD_DIM // 2, 1) * sinf
        return (t * mult).astype(BF16)

    q = _dot(h, w_ref[:, fw:fw + aw])
    k = _dot(h, w_ref[:, fw + aw:fw + aw + kvw])
    v = _dot(h, w_ref[:, fw + aw + kvw:])
    u = _dot(h, w_ref[:, :fw])
    for hh in range(aw // HEAD_DIM):
        sl = slice(hh * HEAD_DIM, (hh + 1) * HEAD_DIM)
        q_ref[:, sl] = norm_rope(q[:, sl], qg_ref[...], scale)
    for hh in range(kvw // HEAD_DIM):
        sl = slice(hh * HEAD_DIM, (hh + 1) * HEAD_DIM)
        k_ref[:, sl] = norm_rope(k[:, sl], kg_ref[...], 1.0)
    v_ref[...] = v.astype(BF16)
    u_ref[...] = u.astype(BF16)


def _inproj0(x, g, w, qg, kg, cosf, sinf, *, seq, fw, aw, kvw, tm):
    t, d = x.shape
    n = w.shape[1]
    tps = seq // tm
    kern = functools.partial(_inproj0_kernel, fw=fw, aw=aw, kvw=kvw, scale=math.log2(math.e) * HEAD_DIM ** -0.5)
    row = lambda width: pl.BlockSpec((tm, width), lambda i: (i, 0))
    return pl.pallas_call(
        kern,
        grid=(t // tm,),
        in_specs=[row(d), _resident((1, d)), _resident((d, n)),
                  _resident((1, HEAD_DIM)), _resident((1, HEAD_DIM)),
                  pl.BlockSpec((tm, HEAD_DIM), lambda i: (i % tps, 0)),
                  pl.BlockSpec((tm, HEAD_DIM), lambda i: (i % tps, 0))],
        out_specs=[row(fw), row(aw), row(kvw), row(kvw)],
        out_shape=[jax.ShapeDtypeStruct((t, fw), BF16), jax.ShapeDtypeStruct((t, aw), BF16),
                   jax.ShapeDtypeStruct((t, kvw), BF16), jax.ShapeDtypeStruct((t, kvw), BF16)],
        compiler_params=_params("parallel"),
        name="inproj0",
    )(x, g, w, qg, kg, cosf, sinf)


def _attn_kernel(q_ref, k_ref, v_ref, o_ref, m_sc, acc_sc, *, group):
    ki = pl.program_id(3)

    @pl.when(ki == 0)
    def _():
        m_sc[...] = jnp.full(m_sc.shape, -jnp.inf, F32)
        acc_sc[...] = jnp.zeros(acc_sc.shape, F32)

    k = k_ref[0]
    v = v_ref[0]
    v_ext = jnp.concatenate([v, jnp.ones_like(v)], axis=1)
    heads = range(group)
    s = [lax.dot_general(q_ref[0, :, j * HEAD_DIM:(j + 1) * HEAD_DIM], k, _NT,
                         preferred_element_type=F32).astype(BF16) for j in heads]
    p, alpha = [], []
    for j in heads:
        m_prev = m_sc[j]
        m_new = jnp.maximum(m_prev, jnp.max(s[j], axis=-1, keepdims=True).astype(F32))
        alpha.append(jnp.exp2(m_prev - m_new))
        p.append(jnp.exp2(s[j] - m_new.astype(BF16)))
        m_sc[j] = m_new
    for j in heads:
        acc_sc[j] = alpha[j] * acc_sc[j] + _dot(p[j], v_ext)

    @pl.when(ki == pl.num_programs(3) - 1)
    def _():
        for j in heads:
            acc = acc_sc[j]
            o_ref[0, :, j * HEAD_DIM:(j + 1) * HEAD_DIM] = (acc[:, :HEAD_DIM] / acc[:, HEAD_DIM:]).astype(BF16)


def _attention(q, k, v, *, tq, tk):
    b, s, aw = q.shape
    group = aw // HEAD_DIM // N_KV_HEADS
    gw = group * HEAD_DIM
    return pl.pallas_call(
        functools.partial(_attn_kernel, group=group),
        grid=(b, N_KV_HEADS, s // tq, s // tk),
        in_specs=[pl.BlockSpec((1, tq, gw), lambda bi, h, qi, ki: (bi, qi, h)),
                  pl.BlockSpec((1, tk, HEAD_DIM), lambda bi, h, qi, ki: (bi, ki, h)),
                  pl.BlockSpec((1, tk, HEAD_DIM), lambda bi, h, qi, ki: (bi, ki, h))],
        out_specs=pl.BlockSpec((1, tq, gw), lambda bi, h, qi, ki: (bi, qi, h)),
        out_shape=jax.ShapeDtypeStruct((b, s, aw), BF16),
        scratch_shapes=[pltpu.VMEM((group, tq, 1), F32), pltpu.VMEM((group, tq, 2 * HEAD_DIM), F32)],
        compiler_params=_params("parallel", "parallel", "parallel", "arbitrary"),
        name="attention",
    )(q, k, v)


def _fourier1_kernel(f_ref, u_ref, y_ref, *, n1):
    y = _dot(f_ref[...], u_ref[0])
    y_ref[0, 0] = y[:n1].astype(BF16)
    y_ref[0, 1] = y[n1:].astype(BF16)


def _fourier2_kernel(m_ref, y_ref, cc_ref, sc_ref, o_ref, *, n2, groups):
    y = y_ref[0].reshape(2 * n2, groups * FNET_GROUP_DIM)
    g = _dot(m_ref[0], y)
    gr = g[:n2].astype(BF16)
    gi = g[n2:].astype(BF16)
    for c in range(groups):
        sl = slice(c * FNET_GROUP_DIM, (c + 1) * FNET_GROUP_DIM)
        o_ref[0, :, sl] = (_dot(gr[:, sl], cc_ref[...]) + _dot(gi[:, sl], sc_ref[...])).astype(BF16)


def _fourier_tables(seq):
    n2 = 128
    n1 = seq // n2
    a1 = 2.0 * np.pi * np.outer(np.arange(n1), np.arange(n1)) / n1
    f1 = np.concatenate([np.cos(a1), -np.sin(a1)], axis=0)
    kk = np.arange(n1)[:, None, None] + n1 * np.arange(n2)[None, :, None]
    th = 2.0 * np.pi * (kk * np.arange(n2)[None, None, :] % seq) / seq
    mc, ms = np.cos(th), np.sin(th)
    m = np.concatenate([np.concatenate([mc, ms], axis=2),
                        np.concatenate([-ms, mc], axis=2)], axis=1)
    ac = 2.0 * np.pi * np.outer(np.arange(FNET_GROUP_DIM), np.arange(FNET_GROUP_DIM)) / FNET_GROUP_DIM
    norm = 1.0 / math.sqrt(seq * FNET_GROUP_DIM)
    return (jnp.asarray(f1, BF16), jnp.asarray(m, BF16),
            jnp.asarray(np.cos(ac) * norm, BF16), jnp.asarray(np.sin(ac) * norm, BF16))


def _fourier_mix(u, *, batch, seq, tn):
    t, width = u.shape
    n2 = 128
    n1 = seq // n2
    groups = width // FNET_GROUP_DIM
    f1, m, cc, sc = _fourier_tables(seq)
    cols = n2 * width
    y = pl.pallas_call(
        functools.partial(_fourier1_kernel, n1=n1),
        grid=(batch, cols // tn),
        in_specs=[_resident((2 * n1, n1)),
                  pl.BlockSpec((1, n1, tn), lambda b, j: (b, 0, j))],
        out_specs=pl.BlockSpec((1, 2, n1, tn), lambda b, j: (b, 0, 0, j)),
        out_shape=jax.ShapeDtypeStruct((batch, 2, n1, cols), BF16),
        compiler_params=_params("parallel", "parallel"),
        name="fourier_seq1",
    )(f1, u.reshape(batch, n1, cols))
    y = y.reshape(batch, 2, seq, width)
    out = pl.pallas_call(
        functools.partial(_fourier2_kernel, n2=n2, groups=groups),
        grid=(n1, batch),
        in_specs=[pl.BlockSpec((1, 2 * n2, 2 * n2), lambda k1, b: (k1, 0, 0)),
                  pl.BlockSpec((1, 2, n2, width), lambda k1, b: (b, 0, k1, 0)),
                  _resident((FNET_GROUP_DIM, FNET_GROUP_DIM)),
                  _resident((FNET_GROUP_DIM, FNET_GROUP_DIM))],
        out_specs=pl.BlockSpec((1, n2, width), lambda k1, b: (b, 0, k1)),
        out_shape=jax.ShapeDtypeStruct((batch, n2, n1 * width), BF16),
        compiler_params=_params("parallel", "parallel"),
        name="fourier_seq2",
    )(m, y, cc, sc)
    return out.reshape(t, width)


def _proj_res_kernel(*refs, n_lhs, has_next):
    lhs = refs[:n_lhs]
    ws = refs[n_lhs:2 * n_lhs]
    x_ref, gp_ref = refs[2 * n_lhs:2 * n_lhs + 2]
    pos = 2 * n_lhs + 2
    gn_ref = refs[pos] if has_next else None
    pos += int(has_next)
    xo_ref = refs[pos]
    ho_ref = refs[pos + 1] if has_next else None
    acc = refs[-1]
    kk = pl.program_id(1)

    @pl.when(kk == 0)
    def _():
        acc[...] = jnp.zeros(acc.shape, F32)

    part = _dot(lhs[0][...], ws[0][...])
    for a, w in zip(lhs[1:], ws[1:]):
        part += _dot(a[...], w[...])
    acc[...] += part

    @pl.when(kk == pl.num_programs(1) - 1)
    def _():
        xn = x_ref[...] + _rms(acc[...], gp_ref[...])
        xo_ref[...] = xn
        if has_next:
            ho_ref[...] = _rms(xn, gn_ref[...]).astype(BF16)


def _proj_res(lhs, ws, x, g_post, g_next, *, tm, tk):
    t, d = x.shape
    n_lhs = len(lhs)
    kdim = lhs[0].shape[1]
    has_next = g_next is not None
    in_specs = [pl.BlockSpec((tm, tk), lambda i, k: (i, k)) for _ in lhs]
    in_specs += [pl.BlockSpec((tk, d), lambda i, k: (k, 0)) for _ in ws]
    in_specs += [pl.BlockSpec((tm, d), lambda i, k: (i, 0)), _resident((1, d))]
    args = list(lhs) + list(ws) + [x, g_post]
    out_specs = [pl.BlockSpec((tm, d), lambda i, k: (i, 0))]
    out_shape = [jax.ShapeDtypeStruct((t, d), F32)]
    if has_next:
        in_specs.append(_resident((1, d)))
        args.append(g_next)
        out_specs.append(pl.BlockSpec((tm, d), lambda i, k: (i, 0)))
        out_shape.append(jax.ShapeDtypeStruct((t, d), BF16))
    res = pl.pallas_call(
        functools.partial(_proj_res_kernel, n_lhs=n_lhs, has_next=has_next),
        grid=(t // tm, kdim // tk),
        in_specs=in_specs, out_specs=out_specs, out_shape=out_shape,
        scratch_shapes=[pltpu.VMEM((tm, d), F32)],
        compiler_params=_params("parallel", "arbitrary"),
        name="proj_res",
    )(*args)
    return res if has_next else (res[0], None)


def _gelu_tanh(x):
    return 0.5 * x * (1.0 + jnp.tanh(math.sqrt(2.0 / math.pi) * (x + 0.044715 * x * x * x)))


def _silu(x):
    return x * (1.0 / (1.0 + jnp.exp(-x)))


def _assemble_halo(hext, h_ref, hp_ref, hn_ref, *, tm, tps):
    i = pl.program_id(0)
    first = (i % tps) == 0
    last = (i % tps) == tps - 1
    prev_blk = jnp.where(first, jnp.zeros_like(hp_ref[...]), hp_ref[...])
    next_blk = jnp.where(last, jnp.zeros_like(hn_ref[...]), hn_ref[...])
    row = lax.broadcasted_iota(jnp.int32, prev_blk.shape, 0)
    hext[0:tm] = h_ref[...]
    hext[tm:] = jnp.where(row < HALO // 2, next_blk, prev_blk)


def _conv_rows(u, cw, cb, *, tm):
    rows = tm + HALO
    c = cb + cw[0:1] * pltpu.roll(u, 1, 0) + cw[1:2] * u + cw[2:3] * pltpu.roll(u, rows - 1, 0)
    return c[0:tm]


def _halo_specs(t, tm, kdim):
    hb = tm // HALO
    last_blk = t // HALO - 1
    return [pl.BlockSpec((tm, kdim), lambda i, j: (i, 0)),
            pl.BlockSpec((HALO, kdim), lambda i, j: (jnp.maximum(i * hb - 1, 0), 0)),
            pl.BlockSpec((HALO, kdim), lambda i, j: (jnp.minimum((i + 1) * hb, last_blk), 0))]


def _inproj1_kernel(h_ref, hp_ref, hn_ref, w_ref, cw_ref, cb_ref, z_ref, xbc_ref, hext, *, tm, tps, nz, sub):
    j = pl.program_id(1)
    tn = w_ref.shape[1]

    @pl.when(j == 0)
    def _():
        _assemble_halo(hext, h_ref, hp_ref, hn_ref, tm=tm, tps=tps)

    @pl.when(j < nz)
    def _():
        h = hext[0:tm, :]
        us = [_dot(h, w_ref[:, c0:c0 + sub]) for c0 in range(0, tn, sub)]
        for k, u in enumerate(us):
            z_ref[:, k * sub:(k + 1) * sub] = u.astype(BF16)

    @pl.when(j >= nz)
    def _():
        hx = hext[...]
        us = [_dot(hx, w_ref[:, c0:c0 + sub]) for c0 in range(0, tn, sub)]
        for k, u in enumerate(us):
            cols = slice(k * sub, (k + 1) * sub)
            y = _conv_rows(u, cw_ref[:, cols], cb_ref[:, cols], tm=tm)
            xbc_ref[:, cols] = _silu(y).astype(BF16)


def _inproj1(h, w, conv_w, conv_b, *, seq, tm, tn, sub, d_inner, conv_dim):
    t, kdim = h.shape
    nz = d_inner // tn
    nx = conv_dim // tn

    def xtile(j):
        return jnp.clip(j - nz, 0, nx - 1)

    in_specs = _halo_specs(t, tm, kdim) + [
        pl.BlockSpec((kdim, tn), lambda i, j: (0, j)),
        pl.BlockSpec((3, tn), lambda i, j: (0, xtile(j))),
        pl.BlockSpec((1, tn), lambda i, j: (0, xtile(j)))]
    return pl.pallas_call(
        functools.partial(_inproj1_kernel, tm=tm, tps=seq // tm, nz=nz, sub=sub),
        grid=(t // tm, nz + nx),
        in_specs=in_specs,
        out_specs=[pl.BlockSpec((tm, tn), lambda i, j: (i, jnp.minimum(j, nz - 1))),
                   pl.BlockSpec((tm, tn), lambda i, j: (i, xtile(j)))],
        out_shape=[jax.ShapeDtypeStruct((t, d_inner), BF16), jax.ShapeDtypeStruct((t, conv_dim), BF16)],
        scratch_shapes=[pltpu.VMEM((tm + HALO, kdim), BF16)],
        compiler_params=_params("parallel", "arbitrary"),
        name="inproj1",
    )(h, h, h, w, conv_w, conv_b)


def _ffn_kernel(*refs, tm, tps, nj, has_next):
    (h_ref, hp_ref, hn_ref, wg_ref, wv_ref, cwg_ref, cwv_ref, cbg_ref, cbv_ref,
     wd_ref, wdl_ref, x_ref, gp_ref) = refs[:13]
    pos = 13
    gn_ref = refs[pos] if has_next else None
    pos += int(has_next)
    xo_ref = refs[pos]
    ho_ref = refs[pos + 1] if has_next else None
    hext, act_a, act_b, acc = refs[-4:]
    j = pl.program_id(1)

    @pl.when(j == 0)
    def _():
        _assemble_halo(hext, h_ref, hp_ref, hn_ref, tm=tm, tps=tps)
        act_b[...] = jnp.zeros(act_b.shape, BF16)
        acc[...] = jnp.zeros(acc.shape, F32)

    def step(act_prev, act_cur):
        hx = hext[...]
        ug = _dot(hx, wg_ref[...])
        uv = _dot(hx, wv_ref[...])
        acc[...] += _dot(act_prev[...], wd_ref[...])
        gate = _conv_rows(ug, cwg_ref[...], cbg_ref[...], tm=tm)
        val = _conv_rows(uv, cwv_ref[...], cbv_ref[...], tm=tm)
        act_cur[...] = (_gelu_tanh(gate) * val).astype(BF16)

    @pl.when(j % 2 == 0)
    def _():
        step(act_b, act_a)

    @pl.when(j % 2 == 1)
    def _():
        step(act_a, act_b)

    @pl.when(j == nj - 1)
    def _():
        act_last = act_a if (nj - 1) % 2 == 0 else act_b
        m = acc[...] + _dot(act_last[...], wdl_ref[...])
        xn = x_ref[...] + _rms(m, gp_ref[...])
        xo_ref[...] = xn
        if has_next:
            ho_ref[...] = _rms(xn, gn_ref[...]).astype(BF16)


def _ffn(h, w_up, conv_w, conv_b, w_down, x, g_post, g_next, *, seq, tm, tn):
    t, d = x.shape
    d_ff = w_down.shape[0]
    nj = d_ff // tn
    has_next = g_next is not None
    row = pl.BlockSpec((tm, d), lambda i, j: (i, 0))
    in_specs = _halo_specs(t, tm, d) + [
        pl.BlockSpec((d, tn), lambda i, j: (0, j)),
        pl.BlockSpec((d, tn), lambda i, j: (0, nj + j)),
        pl.BlockSpec((3, tn), lambda i, j: (0, j)),
        pl.BlockSpec((3, tn), lambda i, j: (0, nj + j)),
        pl.BlockSpec((1, tn), lambda i, j: (0, j)),
        pl.BlockSpec((1, tn), lambda i, j: (0, nj + j)),
        pl.BlockSpec((tn, d), lambda i, j: (jnp.maximum(j - 1, 0), 0)),
        pl.BlockSpec((tn, d), lambda i, j: (nj - 1, 0), pipeline_mode=pl.Buffered(1)),
        row, _resident((1, d))]
    args = [h, h, h, w_up, w_up, conv_w, conv_w, conv_b, conv_b, w_down, w_down, x, g_post]
    out_specs = [row]
    out_shape = [jax.ShapeDtypeStruct((t, d), F32)]
    if has_next:
        in_specs.append(_resident((1, d)))
        args.append(g_next)
        out_specs.append(row)
        out_shape.append(jax.ShapeDtypeStruct((t, d), BF16))
    res = pl.pallas_call(
        functools.partial(_ffn_kernel, tm=tm, tps=seq // tm, nj=nj, has_next=has_next),
        grid=(t // tm, nj),
        in_specs=in_specs, out_specs=out_specs, out_shape=out_shape,
        scratch_shapes=[pltpu.VMEM((tm + HALO, d), BF16), pltpu.VMEM((tm, tn), BF16),
                        pltpu.VMEM((tm, tn), BF16), pltpu.VMEM((tm, d), F32)],
        compiler_params=_params("parallel", "arbitrary"),
        name="ffn",
    )(*args)
    return res if has_next else (res[0], None)


def _dt_kernel(h_ref, w_ref, b_ref, o_ref):
    r = _dot(h_ref[...], w_ref[...]) + b_ref[...]
    o_ref[...] = jnp.maximum(r, 0.0) + jnp.log(1.0 + jnp.exp(-jnp.abs(r)))


def _dt_proj(h, w, b, *, tm):
    t, kdim = h.shape
    n = w.shape[1]
    return pl.pallas_call(
        _dt_kernel,
        grid=(t // tm,),
        in_specs=[pl.BlockSpec((tm, kdim), lambda i: (i, 0)), _resident((kdim, n)), _resident((1, n))],
        out_specs=pl.BlockSpec((tm, n), lambda i: (i, 0)),
        out_shape=jax.ShapeDtypeStruct((t, n), F32),
        compiler_params=_params("parallel"),
        name="ssd_dt",
    )(h, w, b)


def _split3(x):
    hi = x.astype(BF16)
    r = x - hi.astype(F32)
    mid = r.astype(BF16)
    lo = (r - mid.astype(F32)).astype(BF16)
    return hi, mid, lo


def _cumsum_rows(tri, x):
    hi, mid, lo = _split3(x)
    return _dot(tri, hi) + _dot(tri, mid) + _dot(tri, lo)


def _expand(x, e):
    return _dot(x.astype(BF16), e)


class _Chunk:
    pass


def _ssd_positions(c, x_ref, b_ref, c_ref, dt_ref, a_ref, tri_ref):
    rows = slice(c.r0, c.r0 + SSM_CHUNK)
    c.x = x_ref[rows, :]
    c.bm = b_ref[rows, :]
    c.cm = c_ref[rows, :]
    c.dt = dt_ref[rows, :]
    c.dta = c.dt * a_ref[...]
    c.incl = _cumsum_rows(tri_ref[...], c.dta)


def _ssd_weights(c, e_ref):
    L = SSM_CHUNK
    total = c.incl[L - 1:L, :]
    if c.backward:
        c.pos = c.incl - c.dta
        w_state = c.dt * jnp.exp(c.pos)
        w_out = jnp.exp(total - c.pos)
    else:
        c.pos = c.incl
        w_state = c.dt * jnp.exp(total - c.pos)
        w_out = jnp.exp(c.pos)
    e = e_ref[...]
    c.w_out_x = _expand(w_out, e)
    c.carry = c.w_out_x[0:1] if c.backward else c.w_out_x[L - 1:L]
    c.xs = (c.x.astype(F32) * _expand(w_state, e)).astype(BF16)
    c.b_t = c.bm.astype(F32).T.astype(BF16)
    c.cb = lax.dot_general(c.cm, c.bm, _NT, preferred_element_type=F32)
    c.pos_t = c.pos.T
    c.dt_t = c.dt.T


def _ssd_state(c, h_in):
    c.y = _dot(c.cm, h_in.astype(BF16)) * c.w_out_x
    return h_in * c.carry + _dot(c.b_t, c.xs)


def _ssd_diag(c, y_ref, *, hpg):
    L = SSM_CHUNK
    rows = slice(c.r0, c.r0 + L)
    li = lax.broadcasted_iota(jnp.int32, (L, L), 0)
    si = lax.broadcasted_iota(jnp.int32, (L, L), 1)
    mask = (si >= li) if c.backward else (li >= si)
    lane = lax.broadcasted_iota(jnp.int32, (L, 2 * SSM_HEAD_DIM), 1)
    for jp in range(hpg // 2):
        sl = slice(jp * 2 * SSM_HEAD_DIM, (jp + 1) * 2 * SSM_HEAD_DIM)
        xp = c.x[:, sl]
        outs = []
        for j in (c.lane0 + 2 * jp, c.lane0 + 2 * jp + 1):
            col = c.pos[:, j:j + 1]
            row = c.pos_t[j:j + 1, :]
            d = (row - col) if c.backward else (col - row)
            w = c.cb * jnp.exp(jnp.where(mask, d, -1e30)) * c.dt_t[j:j + 1, :]
            outs.append(_dot(w.astype(BF16), xp))
        y_ref[rows, sl] = (c.y[:, sl] + jnp.where(lane < SSM_HEAD_DIM, outs[0], outs[1])).astype(y_ref.dtype)


def _ssd_kernel(xf, bf, cf, dtf, xb, bb, cb, dtb, a_ref, tri_ref, ef_ref, eb_ref,
                yf_ref, yb_ref, hf_sc, hb_sc, *, hpg, cps):
    @pl.when(pl.program_id(2) == 0)
    def _():
        hf_sc[...] = jnp.zeros(hf_sc.shape, F32)
        hb_sc[...] = jnp.zeros(hb_sc.shape, F32)

    fwd, bwd = [], []
    for k in range(cps):
        f, b = _Chunk(), _Chunk()
        f.r0, f.lane0, f.backward = k * SSM_CHUNK, 0, False
        b.r0, b.lane0, b.backward = (cps - 1 - k) * SSM_CHUNK, hpg, True
        fwd.append(f)
        bwd.append(b)
    for f, b in zip(fwd, bwd):
        _ssd_positions(f, xf, bf, cf, dtf, a_ref, tri_ref)
        _ssd_positions(b, xb, bb, cb, dtb, a_ref, tri_ref)
    for f, b in zip(fwd, bwd):
        _ssd_weights(f, ef_ref)
        _ssd_weights(b, eb_ref)
    hf = hf_sc[...]
    hb = hb_sc[...]
    for f, b in zip(fwd, bwd):
        hf = _ssd_state(f, hf)
        hb = _ssd_state(b, hb)
    hf_sc[...] = hf
    hb_sc[...] = hb
    for f, b in zip(fwd, bwd):
        _ssd_diag(f, yf_ref, hpg=hpg)
        _ssd_diag(b, yb_ref, hpg=hpg)


def _ssd(xbc, dt, a_pad, *, batch, seq, d_inner, cps):
    t = xbc.shape[0]
    L = SSM_CHUNK
    rows = cps * L
    nc = seq // rows
    gw = d_inner // SSM_N_GROUPS
    hpg = gw // SSM_HEAD_DIM
    b0 = d_inner // SSM_D_STATE
    c0 = b0 + SSM_N_GROUPS
    tri = jnp.asarray(np.tril(np.ones((L, L))), BF16)
    head_of_lane = np.arange(gw) // SSM_HEAD_DIM
    ef = jnp.asarray(np.arange(LANES)[:, None] == head_of_lane[None, :], BF16)
    eb = jnp.asarray(np.arange(LANES)[:, None] == (head_of_lane[None, :] + hpg), BF16)

    def fwd(col):
        return lambda b, g, c: (b * nc + c, col(g))

    def bwd(col):
        return lambda b, g, c: (b * nc + (nc - 1 - c), col(g))

    def specs(order):
        return [pl.BlockSpec((rows, gw), order(lambda g: g)),
                pl.BlockSpec((rows, SSM_D_STATE), order(lambda g: b0 + g)),
                pl.BlockSpec((rows, SSM_D_STATE), order(lambda g: c0 + g)),
                pl.BlockSpec((rows, LANES), order(lambda g: g))]

    in_specs = specs(fwd) + specs(bwd) + [
        pl.BlockSpec((1, LANES), lambda b, g, c: (0, g)),
        _resident((L, L)), _resident((LANES, gw)), _resident((LANES, gw))]
    return pl.pallas_call(
        functools.partial(_ssd_kernel, hpg=hpg, cps=cps),
        grid=(batch, SSM_N_GROUPS, nc),
        in_specs=in_specs,
        out_specs=[pl.BlockSpec((rows, gw), fwd(lambda g: g)), pl.BlockSpec((rows, gw), bwd(lambda g: g))],
        out_shape=[jax.ShapeDtypeStruct((t, d_inner), BF16), jax.ShapeDtypeStruct((t, d_inner), BF16)],
        scratch_shapes=[pltpu.VMEM((SSM_D_STATE, gw), F32), pltpu.VMEM((SSM_D_STATE, gw), F32)],
        compiler_params=_params("parallel", "parallel", "arbitrary"),
        name="ssd_scan",
    )(xbc, xbc, xbc, dt, xbc, xbc, xbc, dt, a_pad, tri, ef, eb)


def _ssd_out_kernel(yf_ref, yb_ref, xs_ref, z_ref, dsk_ref, ng_ref, w_ref, x_ref, gp_ref, gn_ref,
                    xo_ref, ho_ref, acc, ss, *, d_inner):
    kk = pl.program_id(1)

    @pl.when(kk == 0)
    def _():
        acc[...] = jnp.zeros(acc.shape, F32)
        ss[...] = jnp.zeros(ss.shape, F32)

    tk = w_ref.shape[0]
    sq = None
    part = None
    for c0 in range(0, tk, OUT_SUB):
        cols = slice(c0, c0 + OUT_SUB)
        y = (yf_ref[:, cols].astype(F32) + yb_ref[:, cols].astype(F32)
             + xs_ref[:, cols].astype(F32) * dsk_ref[:, cols])
        y = y * _silu(z_ref[:, cols].astype(F32))
        y2 = y * y
        for l0 in range(0, OUT_SUB, LANES):
            sq = y2[:, l0:l0 + LANES] if sq is None else sq + y2[:, l0:l0 + LANES]
        d = _dot((y * ng_ref[:, cols]).astype(BF16), w_ref[cols, :])
        part = d if part is None else part + d
    acc[...] += part
    ss[...] += sq

    @pl.when(kk == pl.num_programs(1) - 1)
    def _():
        ssum = jnp.sum(ss[...], axis=-1, keepdims=True)
        m = acc[...] * lax.rsqrt(ssum * (1.0 / d_inner) + EPS)
        xn = x_ref[...] + _rms(m, gp_ref[...])
        xo_ref[...] = xn
        ho_ref[...] = _rms(xn, gn_ref[...]).astype(BF16)


def _ssd_out(yf, yb, xbc, z, d_exp, norm_g, w, x, g_post, g_next, *, tm, tk):
    t, d = x.shape
    d_inner = yf.shape[1]
    lhs = pl.BlockSpec((tm, tk), lambda i, k: (i, k))
    vec = pl.BlockSpec((1, tk), lambda i, k: (0, k))
    row = pl.BlockSpec((tm, d), lambda i, k: (i, 0))
    return pl.pallas_call(
        functools.partial(_ssd_out_kernel, d_inner=d_inner),
        grid=(t // tm, d_inner // tk),
        in_specs=[lhs, lhs, lhs, lhs, vec, vec, pl.BlockSpec((tk, d), lambda i, k: (k, 0)),
                  row, _resident((1, d)), _resident((1, d))],
        out_specs=[row, row],
        out_shape=[jax.ShapeDtypeStruct((t, d), F32), jax.ShapeDtypeStruct((t, d), BF16)],
        scratch_shapes=[pltpu.VMEM((tm, d), F32), pltpu.VMEM((tm, LANES), F32)],
        compiler_params=_params("parallel", "arbitrary"),
        name="ssd_out",
    )(yf, yb, xbc, z, d_exp, norm_g, w, x, g_post, g_next)


def _rope_tables(seq):
    rows = seq // GRID_W
    row = jnp.repeat(jnp.arange(rows, dtype=F32), GRID_W)
    col = jnp.tile(jnp.arange(GRID_W, dtype=F32), rows)
    axis_dim = HEAD_DIM // 2
    inv_freq = ROPE_THETA ** (-jnp.arange(0, axis_dim, 2, dtype=F32) / axis_dim)
    ang = jnp.concatenate([row[:, None] * inv_freq, col[:, None] * inv_freq], axis=-1)
    cos, sin = jnp.cos(ang), jnp.sin(ang)
    return jnp.concatenate([cos, cos], axis=-1), jnp.concatenate([-sin, sin], axis=-1)


def _pick(n, pref):
    t = min(pref, n)
    while n % t:
        t -= LANES
    return t


def kernel(x_prompt, x_sample, mix_pre_g, mix_post_g, ffn_pre_g, ffn_post_g, fa_w_in, fa_q_gain, fa_k_gain, fa_w_out, ssd_w_in, ssd_conv_w, ssd_conv_b, ssd_dt_bias, ssd_a_log, ssd_d, ssd_norm_g, ssd_w_out, ffn_w_up, ffn_conv_w, ffn_conv_b, ffn_w_down):
    depth = mix_pre_g.shape[0]
    d = x_prompt.shape[2]
    aw = N_Q_HEADS * HEAD_DIM
    kvw = N_KV_HEADS * HEAD_DIM
    fw = fa_w_in.shape[2] - aw - 2 * kvw
    assert fa_w_out.shape[1] == fw + aw and fw % FNET_GROUP_DIM == 0
    d_ff = ffn_w_down.shape[1]
    d_inner = ssd_w_out.shape[1]
    n_heads = d_inner // SSM_HEAD_DIM
    hpg = n_heads // SSM_N_GROUPS
    bc_w = SSM_N_GROUPS * SSM_D_STATE
    conv_dim = d_inner + 2 * bc_w
    assert hpg % 2 == 0 and 2 * hpg <= LANES and depth % 2 == 0
    tm = 512
    vec = lambda v: v.reshape(1, -1).astype(F32)

    def regroup(p):
        p = p.astype(F32).reshape(2, SSM_N_GROUPS, hpg).transpose(1, 0, 2).reshape(SSM_N_GROUPS, 2 * hpg)
        return jnp.pad(p, ((0, 0), (0, LANES - 2 * hpg))).reshape(1, SSM_N_GROUPS * LANES)

    layers = []
    for i in range(depth):
        j = i // 2
        lw = dict(ffn_pre=vec(ffn_pre_g[i]), ffn_post=vec(ffn_post_g[i]), mix_pre=vec(mix_pre_g[i]),
                  mix_post=vec(mix_post_g[i]), w_up=ffn_w_up[i].astype(BF16), w_down=ffn_w_down[i].astype(BF16),
                  ffn_cw=ffn_conv_w[i].astype(F32), ffn_cb=vec(ffn_conv_b[i]))
        if i % 2 == 0:
            wo = fa_w_out[j].astype(BF16)
            lw.update(w_in=fa_w_in[j].astype(BF16), qg=vec(fa_q_gain[j]), kg=vec(fa_k_gain[j]),
                      wo_f=wo[:fw], wo_a=wo[fw:])
        else:
            w_dt = ssd_w_in[j][:, d_inner + conv_dim:].reshape(d, 2, SSM_N_GROUPS, hpg)
            w_dt = jnp.pad(w_dt.transpose(0, 2, 1, 3).reshape(d, SSM_N_GROUPS, 2 * hpg),
                           ((0, 0), (0, 0), (0, LANES - 2 * hpg))).reshape(d, SSM_N_GROUPS * LANES)
            lw.update(w_in=ssd_w_in[j].astype(BF16), w_dt=w_dt.astype(BF16), dt_bias=regroup(ssd_dt_bias[j]),
                      a_pad=regroup(-jnp.exp(ssd_a_log[j].astype(F32))), cw=ssd_conv_w[j].astype(F32),
                      cb=vec(ssd_conv_b[j]), d_exp=jnp.repeat(ssd_d[j].astype(F32), SSM_HEAD_DIM).reshape(1, d_inner),
                      norm_g=vec(ssd_norm_g[j]), w_out=ssd_w_out[j].astype(BF16))
        layers.append(lw)

    def trunk(xin):
        batch, seq, _ = xin.shape
        assert seq % tm == 0
        t = batch * seq
        x = xin.reshape(t, d)
        cosf, sinf = _rope_tables(seq)
        h = None
        for i, lw in enumerate(layers):
            if i % 2 == 0:
                u, q, k, v = _inproj0(x, lw["mix_pre"], lw["w_in"], lw["qg"], lw["kg"], cosf, sinf,
                                      seq=seq, fw=fw, aw=aw, kvw=kvw, tm=tm)
                a = _attention(q.reshape(batch, seq, aw), k.reshape(batch, seq, kvw), v.reshape(batch, seq, kvw),
                               tq=_pick(seq, 1024), tk=_pick(seq, 1024)).reshape(t, aw)
                f = _fourier_mix(u, batch=batch, seq=seq, tn=_pick(128 * fw, 8192))
                x, h = _proj_res([f, a], [lw["wo_f"], lw["wo_a"]], x, lw["mix_post"], lw["ffn_pre"], tm=tm, tk=fw)
            else:
                z, xbc = _inproj1(h, lw["w_in"], lw["cw"], lw["cb"], seq=seq, tm=tm,
                                  tn=math.gcd(d_inner, conv_dim), sub=512, d_inner=d_inner, conv_dim=conv_dim)
                dt = _dt_proj(h, lw["w_dt"], lw["dt_bias"], tm=tm)
                yf, yb = _ssd(xbc, dt, lw["a_pad"], batch=batch, seq=seq, d_inner=d_inner, cps=8)
                x, h = _ssd_out(yf, yb, xbc, z, lw["d_exp"], lw["norm_g"], lw["w_out"], x,
                                lw["mix_post"], lw["ffn_pre"], tm=tm, tk=_pick(d_inner, 1024))
            g_next = layers[i + 1]["mix_pre"] if i + 1 < depth else None
            x, h = _ffn(h, lw["w_up"], lw["ffn_cw"], lw["ffn_cb"], lw["w_down"], x, lw["ffn_post"], g_next,
                        seq=seq, tm=tm, tn=_pick(d_ff, 512))
        return x.reshape(batch, seq, d)

    return (trunk(x_prompt), trunk(x_sample))
```

```python
import functools
import math

import jax
import jax.numpy as jnp
import numpy as np
from jax import lax
from jax.experimental import pallas as pl
from jax.experimental.pallas import tpu as pltpu

F32 = jnp.float32
BF16 = jnp.bfloat16

EPS = 1e-6
GRID_W = 64
HEAD_DIM = 128
N_Q_HEADS = 8
N_KV_HEADS = 2
FNET_GROUP_DIM = 128
ROPE_THETA = 10000.0
SSM_HEAD_DIM = 64
SSM_N_GROUPS = 8
SSM_D_STATE = 128
SSM_CHUNK = 128
LANES = 128
HALO = 16
OUT_SUB = 256
VMEM_LIMIT = 56 * 1024 * 1024

_NT = (((1,), (1,)), ((), ()))


def _params(*sem):
    return pltpu.CompilerParams(dimension_semantics=sem, vmem_limit_bytes=VMEM_LIMIT)


def _resident(shape):
    nd = len(shape)
    return pl.BlockSpec(shape, lambda *_: (0,) * nd, pipeline_mode=pl.Buffered(1))


def _rms(x, g):
    return x * lax.rsqrt(jnp.mean(x * x, axis=-1, keepdims=True) + EPS) * g


def _dot(a, b):
    return jnp.dot(a, b, preferred_element_type=F32)


def _inproj0_kernel(x_ref, g_ref, w_ref, qg_ref, kg_ref, cos_ref, sin_ref,
                    u_ref, q_ref, k_ref, v_ref, *, fw, aw, kvw, scale):
    h = _rms(x_ref[...], g_ref[...]).astype(BF16)
    cosf = cos_ref[...]
    sinf = sin_ref[...]

    def norm_rope(t, gain, mult):
        t = _rms(t, gain)
        t = t * cosf + pltpu.roll(t, HEAD_DIM // 2, 1) * sinf
        return (t * mult).astype(BF16)

    q = _dot(h, w_ref[:, fw:fw + aw])
    k = _dot(h, w_ref[:, fw + aw:fw + aw + kvw])
    v = _dot(h, w_ref[:, fw + aw + kvw:])
    u = _dot(h, w_ref[:, :fw])
    for hh in range(aw // HEAD_DIM):
        sl = slice(hh * HEAD_DIM, (hh + 1) * HEAD_DIM)
        q_ref[:, sl] = norm_rope(q[:, sl], qg_ref[...], scale)
    for hh in range(kvw // HEAD_DIM):
        sl = slice(hh * HEAD_DIM, (hh + 1) * HEAD_DIM)
        k_ref[:, sl] = norm_rope(k[:, sl], kg_ref[...], 1.0)
    v_ref[...] = v.astype(BF16)
    u_ref[...] = u.astype(BF16)


def _inproj0(x, g, w, qg, kg, cosf, sinf, *, seq, fw, aw, kvw, tm):
    t, d = x.shape
    n = w.shape[1]
    tps = seq // tm
    kern = functools.partial(_inproj0_kernel, fw=fw, aw=aw, kvw=kvw, scale=math.log2(math.e) * HEAD_DIM ** -0.5)
    row = lambda width: pl.BlockSpec((tm, width), lambda i: (i, 0))
    return pl.pallas_call(
        kern,
        grid=(t // tm,),
        in_specs=[row(d), _resident((1, d)), _resident((d, n)),
                  _resident((1, HEAD_DIM)), _resident((1, HEAD_DIM)),
                  pl.BlockSpec((tm, HEAD_DIM), lambda i: (i % tps, 0)),
                  pl.BlockSpec((tm, HEAD_DIM), lambda i: (i % tps, 0))],
        out_specs=[row(fw), row(aw), row(kvw), row(kvw)],
        out_shape=[jax.ShapeDtypeStruct((t, fw), BF16), jax.ShapeDtypeStruct((t, aw), BF16),
                   jax.ShapeDtypeStruct((t, kvw), BF16), jax.ShapeDtypeStruct((t, kvw), BF16)],
        compiler_params=_params("parallel"),
        name="inproj0",
    )(x, g, w, qg, kg, cosf, sinf)


def _attn_kernel(q_ref, k_ref, v_ref, o_ref, m_sc, acc_sc, *, group):
    ki = pl.program_id(3)

    @pl.when(ki == 0)
    def _():
        m_sc[...] = jnp.full(m_sc.shape, -jnp.inf, F32)
        acc_sc[...] = jnp.zeros(acc_sc.shape, F32)

    k = k_ref[0]
    v = v_ref[0]
    v_ext = jnp.concatenate([v, jnp.ones_like(v)], axis=1)
    heads = range(group)
    s = [lax.dot_general(q_ref[0, :, j * HEAD_DIM:(j + 1) * HEAD_DIM], k, _NT,
                         preferred_element_type=F32).astype(BF16) for j in heads]
    p, alpha = [], []
    for j in heads:
        m_prev = m_sc[j]
        m_new = jnp.maximum(m_prev, jnp.max(s[j], axis=-1, keepdims=True).astype(F32))
        alpha.append(jnp.exp2(m_prev - m_new))
        p.append(jnp.exp2(s[j] - m_new.astype(BF16)))
        m_sc[j] = m_new
    for j in heads:
        acc_sc[j] = alpha[j] * acc_sc[j] + _dot(p[j], v_ext)

    @pl.when(ki == pl.num_programs(3) - 1)
    def _():
        for j in heads:
            acc = acc_sc[j]
            o_ref[0, :, j * HEAD_DIM:(j + 1) * HEAD_DIM] = (acc[:, :HEAD_DIM] / acc[:, HEAD_DIM:]).astype(BF16)


def _attention(q, k, v, *, tq, tk):
    b, s, aw = q.shape
    group = aw // HEAD_DIM // N_KV_HEADS
    gw = group * HEAD_DIM
    return pl.pallas_call(
        functools.partial(_attn_kernel, group=group),
        grid=(b, N_KV_HEADS, s // tq, s // tk),
        in_specs=[pl.BlockSpec((1, tq, gw), lambda bi, h, qi, ki: (bi, qi, h)),
                  pl.BlockSpec((1, tk, HEAD_DIM), lambda bi, h, qi, ki: (bi, ki, h)),
                  pl.BlockSpec((1, tk, HEAD_DIM), lambda bi, h, qi, ki: (bi, ki, h))],
        out_specs=pl.BlockSpec((1, tq, gw), lambda bi, h, qi, ki: (bi, qi, h)),
        out_shape=jax.ShapeDtypeStruct((b, s, aw), BF16),
        scratch_shapes=[pltpu.VMEM((group, tq, 1), F32), pltpu.VMEM((group, tq, 2 * HEAD_DIM), F32)],
        compiler_params=_params("parallel", "parallel", "parallel", "arbitrary"),
        name="attention",
    )(q, k, v)


def _fourier1_kernel(f_ref, u_ref, y_ref, *, n1):
    y = _dot(f_ref[...], u_ref[0])
    y_ref[0, 0] = y[:n1].astype(BF16)
    y_ref[0, 1] = y[n1:].astype(BF16)


def _fourier2_kernel(m_ref, y_ref, cc_ref, sc_ref, o_ref, *, n2, groups):
    y = y_ref[0].reshape(2 * n2, groups * FNET_GROUP_DIM)
    g = _dot(m_ref[0], y)
    gr = g[:n2].astype(BF16)
    gi = g[n2:].astype(BF16)
    for c in range(groups):
        sl = slice(c * FNET_GROUP_DIM, (c + 1) * FNET_GROUP_DIM)
        o_ref[0, :, sl] = (_dot(gr[:, sl], cc_ref[...]) + _dot(gi[:, sl], sc_ref[...])).astype(BF16)


def _fourier_tables(seq):
    n2 = 128
    n1 = seq // n2
    a1 = 2.0 * np.pi * np.outer(np.arange(n1), np.arange(n1)) / n1
    f1 = np.concatenate([np.cos(a1), -np.sin(a1)], axis=0)
    kk = np.arange(n1)[:, None, None] + n1 * np.arange(n2)[None, :, None]
    th = 2.0 * np.pi * (kk * np.arange(n2)[None, None, :] % seq) / seq
    mc, ms = np.cos(th), np.sin(th)
    m = np.concatenate([np.concatenate([mc, ms], axis=2),
                        np.concatenate([-ms, mc], axis=2)], axis=1)
    ac = 2.0 * np.pi * np.outer(np.arange(FNET_GROUP_DIM), np.arange(FNET_GROUP_DIM)) / FNET_GROUP_DIM
    norm = 1.0 / math.sqrt(seq * FNET_GROUP_DIM)
    return (jnp.asarray(f1, BF16), jnp.asarray(m, BF16),
            jnp.asarray(np.cos(ac) * norm, BF16), jnp.asarray(np.sin(ac) * norm, BF16))


def _fourier_mix(u, *, batch, seq, tn):
    t, width = u.shape
    n2 = 128
    n1 = seq // n2
    groups = width // FNET_GROUP_DIM
    f1, m, cc, sc = _fourier_tables(seq)
    cols = n2 * width
    y = pl.pallas_call(
        functools.partial(_fourier1_kernel, n1=n1),
        grid=(batch, cols // tn),
        in_specs=[_resident((2 * n1, n1)),
                  pl.BlockSpec((1, n1, tn), lambda b, j: (b, 0, j))],
        out_specs=pl.BlockSpec((1, 2, n1, tn), lambda b, j: (b, 0, 0, j)),
        out_shape=jax.ShapeDtypeStruct((batch, 2, n1, cols), BF16),
        compiler_params=_params("parallel", "parallel"),
        name="fourier_seq1",
    )(f1, u.reshape(batch, n1, cols))
    y = y.reshape(batch, 2, seq, width)
    out = pl.pallas_call(
        functools.partial(_fourier2_kernel, n2=n2, groups=groups),
        grid=(n1, batch),
        in_specs=[pl.BlockSpec((1, 2 * n2, 2 * n2), lambda k1, b: (k1, 0, 0)),
                  pl.BlockSpec((1, 2, n2, width), lambda k1, b: (b, 0, k1, 0)),
                  _resident((FNET_GROUP_DIM, FNET_GROUP_DIM)),
                  _resident((FNET_GROUP_DIM, FNET_GROUP_DIM))],
        out_specs=pl.BlockSpec((1, n2, width), lambda k1, b: (b, 0, k1)),
        out_shape=jax.ShapeDtypeStruct((batch, n2, n1 * width), BF16),
        compiler_params=_params("parallel", "parallel"),
        name="fourier_seq2",
    )(m, y, cc, sc)
    return out.reshape(t, width)


def _proj_res_kernel(*refs, n_lhs, has_next):
    lhs = refs[:n_lhs]
    ws = refs[n_lhs:2 * n_lhs]
    x_ref, gp_ref = refs[2 * n_lhs:2 * n_lhs + 2]
    pos = 2 * n_lhs + 2
    gn_ref = refs[pos] if has_next else None
    pos += int(has_next)
    xo_ref = refs[pos]
    ho_ref = refs[pos + 1] if has_next else None
    acc = refs[-1]
    kk = pl.program_id(1)

    @pl.when(kk == 0)
    def _():
        acc[...] = jnp.zeros(acc.shape, F32)

    part = _dot(lhs[0][...], ws[0][...])
    for a, w in zip(lhs[1:], ws[1:]):
        part += _dot(a[...], w[...])
    acc[...] += part

    @pl.when(kk == pl.num_programs(1) - 1)
    def _():
        xn = x_ref[...] + _rms(acc[...], gp_ref[...])
        xo_ref[...] = xn
        if has_next:
            ho_ref[...] = _rms(xn, gn_ref[...]).astype(BF16)


def _proj_res(lhs, ws, x, g_post, g_next, *, tm, tk):
    t, d = x.shape
    n_lhs = len(lhs)
    kdim = lhs[0].shape[1]
    has_next = g_next is not None
    in_specs = [pl.BlockSpec((tm, tk), lambda i, k: (i, k)) for _ in lhs]
    in_specs += [pl.BlockSpec((tk, d), lambda i, k: (k, 0)) for _ in ws]
    in_specs += [pl.BlockSpec((tm, d), lambda i, k: (i, 0)), _resident((1, d))]
    args = list(lhs) + list(ws) + [x, g_post]
    out_specs = [pl.BlockSpec((tm, d), lambda i, k: (i, 0))]
    out_shape = [jax.ShapeDtypeStruct((t, d), F32)]
    if has_next:
        in_specs.append(_resident((1, d)))
        args.append(g_next)
        out_specs.append(pl.BlockSpec((tm, d), lambda i, k: (i, 0)))
        out_shape.append(jax.ShapeDtypeStruct((t, d), BF16))
    res = pl.pallas_call(
        functools.partial(_proj_res_kernel, n_lhs=n_lhs, has_next=has_next),
        grid=(t // tm, kdim // tk),
        in_specs=in_specs, out_specs=out_specs, out_shape=out_shape,
        scratch_shapes=[pltpu.VMEM((tm, d), F32)],
        compiler_params=_params("parallel", "arbitrary"),
        name="proj_res",
    )(*args)
    return res if has_next else (res[0], None)


def _gelu_tanh(x):
    return 0.5 * x * (1.0 + jnp.tanh(math.sqrt(2.0 / math.pi) * (x + 0.044715 * x * x * x)))


def _silu(x):
    return x * (1.0 / (1.0 + jnp.exp(-x)))


def _assemble_halo(hext, h_ref, hp_ref, hn_ref, *, tm, tps):
    i = pl.program_id(0)
    first = (i % tps) == 0
    last = (i % tps) == tps - 1
    prev_blk = jnp.where(first, jnp.zeros_like(hp_ref[...]), hp_ref[...])
    next_blk = jnp.where(last, jnp.zeros_like(hn_ref[...]), hn_ref[...])
    row = lax.broadcasted_iota(jnp.int32, prev_blk.shape, 0)
    hext[0:tm] = h_ref[...]
    hext[tm:] = jnp.where(row < HALO // 2, next_blk, prev_blk)


def _conv_rows(u, cw, cb, *, tm):
    rows = tm + HALO
    c = cb + cw[0:1] * pltpu.roll(u, 1, 0) + cw[1:2] * u + cw[2:3] * pltpu.roll(u, rows - 1, 0)
    return c[0:tm]


def _halo_specs(t, tm, kdim):
    hb = tm // HALO
    last_blk = t // HALO - 1
    return [pl.BlockSpec((tm, kdim), lambda i, j: (i, 0)),
            pl.BlockSpec((HALO, kdim), lambda i, j: (jnp.maximum(i * hb - 1, 0), 0)),
            pl.BlockSpec((HALO, kdim), lambda i, j: (jnp.minimum((i + 1) * hb, last_blk), 0))]


def _inproj1_kernel(h_ref, hp_ref, hn_ref, w_ref, cw_ref, cb_ref, z_ref, xbc_ref, hext, *, tm, tps, nz, sub):
    j = pl.program_id(1)
    tn = w_ref.shape[1]

    @pl.when(j == 0)
    def _():
        _assemble_halo(hext, h_ref, hp_ref, hn_ref, tm=tm, tps=tps)

    @pl.when(j < nz)
    def _():
        h = hext[0:tm, :]
        us = [_dot(h, w_ref[:, c0:c0 + sub]) for c0 in range(0, tn, sub)]
        for k, u in enumerate(us):
            z_ref[:, k * sub:(k + 1) * sub] = u.astype(BF16)

    @pl.when(j >= nz)
    def _():
        hx = hext[...]
        us = [_dot(hx, w_ref[:, c0:c0 + sub]) for c0 in range(0, tn, sub)]
        for k, u in enumerate(us):
            cols = slice(k * sub, (k + 1) * sub)
            y = _conv_rows(u, cw_ref[:, cols], cb_ref[:, cols], tm=tm)
            xbc_ref[:, cols] = _silu(y).astype(BF16)


def _inproj1(h, w, conv_w, conv_b, *, seq, tm, tn, sub, d_inner, conv_dim):
    t, kdim = h.shape
    nz = d_inner // tn
    nx = conv_dim // tn

    def xtile(j):
        return jnp.clip(j - nz, 0, nx - 1)

    in_specs = _halo_specs(t, tm, kdim) + [
        pl.BlockSpec((kdim, tn), lambda i, j: (0, j)),
        pl.BlockSpec((3, tn), lambda i, j: (0, xtile(j))),
        pl.BlockSpec((1, tn), lambda i, j: (0, xtile(j)))]
    return pl.pallas_call(
        functools.partial(_inproj1_kernel, tm=tm, tps=seq // tm, nz=nz, sub=sub),
        grid=(t // tm, nz + nx),
        in_specs=in_specs,
        out_specs=[pl.BlockSpec((tm, tn), lambda i, j: (i, jnp.minimum(j, nz - 1))),
                   pl.BlockSpec((tm, tn), lambda i, j: (i, xtile(j)))],
        out_shape=[jax.ShapeDtypeStruct((t, d_inner), BF16), jax.ShapeDtypeStruct((t, conv_dim), BF16)],
        scratch_shapes=[pltpu.VMEM((tm + HALO, kdim), BF16)],
        compiler_params=_params("parallel", "arbitrary"),
        name="inproj1",
    )(h, h, h, w, conv_w, conv_b)


def _ffn_kernel(*refs, tm, tps, nj, has_next):
    (h_ref, hp_ref, hn_ref, wg_ref, wv_ref, cwg_ref, cwv_ref, cbg_ref, cbv_ref,
     wd_ref, wdl_ref, x_ref, gp_ref) = refs[:13]
    pos = 13
    gn_ref = refs[pos] if has_next else None
    pos += int(has_next)
    xo_ref = refs[pos]
    ho_ref = refs[pos + 1] if has_next else None
    hext, act_a, act_b, acc = refs[-4:]
    j = pl.program_id(1)

    @pl.when(j == 0)
    def _():
        _assemble_halo(hext, h_ref, hp_ref, hn_ref, tm=tm, tps=tps)
        act_b[...] = jnp.zeros(act_b.shape, BF16)
        acc[...] = jnp.zeros(acc.shape, F32)

    def step(act_prev, act_cur):
        hx = hext[...]
        ug = _dot(hx, wg_ref[...])
        uv = _dot(hx, wv_ref[...])
        acc[...] += _dot(act_prev[...], wd_ref[...])
        gate = _conv_rows(ug, cwg_ref[...], cbg_ref[...], tm=tm)
        val = _conv_rows(uv, cwv_ref[...], cbv_ref[...], tm=tm)
        act_cur[...] = (_gelu_tanh(gate) * val).astype(BF16)

    @pl.when(j % 2 == 0)
    def _():
        step(act_b, act_a)

    @pl.when(j % 2 == 1)
    def _():
        step(act_a, act_b)

    @pl.when(j == nj - 1)
    def _():
        act_last = act_a if (nj - 1) % 2 == 0 else act_b
        m = acc[...] + _dot(act_last[...], wdl_ref[...])
        xn = x_ref[...] + _rms(m, gp_ref[...])
        xo_ref[...] = xn
        if has_next:
            ho_ref[...] = _rms(xn, gn_ref[...]).astype(BF16)


def _ffn(h, w_up, conv_w, conv_b, w_down, x, g_post, g_next, *, seq, tm, tn):
    t, d = x.shape
    d_ff = w_down.shape[0]
    nj = d_ff // tn
    has_next = g_next is not None
    row = pl.BlockSpec((tm, d), lambda i, j: (i, 0))
    in_specs = _halo_specs(t, tm, d) + [
        pl.BlockSpec((d, tn), lambda i, j: (0, j)),
        pl.BlockSpec((d, tn), lambda i, j: (0, nj + j)),
        pl.BlockSpec((3, tn), lambda i, j: (0, j)),
        pl.BlockSpec((3, tn), lambda i, j: (0, nj + j)),
        pl.BlockSpec((1, tn), lambda i, j: (0, j)),
        pl.BlockSpec((1, tn), lambda i, j: (0, nj + j)),
        pl.BlockSpec((tn, d), lambda i, j: (jnp.maximum(j - 1, 0), 0)),
        pl.BlockSpec((tn, d), lambda i, j: (nj - 1, 0), pipeline_mode=pl.Buffered(1)),
        row, _resident((1, d))]
    args = [h, h, h, w_up, w_up, conv_w, conv_w, conv_b, conv_b, w_down, w_down, x, g_post]
    out_specs = [row]
    out_shape = [jax.ShapeDtypeStruct((t, d), F32)]
    if has_next:
        in_specs.append(_resident((1, d)))
        args.append(g_next)
        out_specs.append(row)
        out_shape.append(jax.ShapeDtypeStruct((t, d), BF16))
    res = pl.pallas_call(
        functools.partial(_ffn_kernel, tm=tm, tps=seq // tm, nj=nj, has_next=has_next),
        grid=(t // tm, nj),
        in_specs=in_specs, out_specs=out_specs, out_shape=out_shape,
        scratch_shapes=[pltpu.VMEM((tm + HALO, d), BF16), pltpu.VMEM((tm, tn), BF16),
                        pltpu.VMEM((tm, tn), BF16), pltpu.VMEM((tm, d), F32)],
        compiler_params=_params("parallel", "arbitrary"),
        name="ffn",
    )(*args)
    return res if has_next else (res[0], None)


def _dt_kernel(h_ref, w_ref, b_ref, o_ref):
    r = _dot(h_ref[...], w_ref[...]) + b_ref[...]
    o_ref[...] = jnp.maximum(r, 0.0) + jnp.log(1.0 + jnp.exp(-jnp.abs(r)))


def _dt_proj(h, w, b, *, tm):
    t, kdim = h.shape
    n = w.shape[1]
    return pl.pallas_call(
        _dt_kernel,
        grid=(t // tm,),
        in_specs=[pl.BlockSpec((tm, kdim), lambda i: (i, 0)), _resident((kdim, n)), _resident((1, n))],
        out_specs=pl.BlockSpec((tm, n), lambda i: (i, 0)),
        out_shape=jax.ShapeDtypeStruct((t, n), F32),
        compiler_params=_params("parallel"),
        name="ssd_dt",
    )(h, w, b)


def _split3(x):
    hi = x.astype(BF16)
    r = x - hi.astype(F32)
    mid = r.astype(BF16)
    lo = (r - mid.astype(F32)).astype(BF16)
    return hi, mid, lo


def _cumsum_rows(tri, x):
    hi, mid, lo = _split3(x)
    return _dot(tri, hi) + _dot(tri, mid) + _dot(tri, lo)


def _expand(x, e):
    return _dot(x.astype(BF16), e)


class _Chunk:
    pass


def _ssd_positions(c, x_ref, b_ref, c_ref, dt_ref, a_ref, tri_ref):
    rows = slice(c.r0, c.r0 + SSM_CHUNK)
    c.x = x_ref[rows, :]
    c.bm = b_ref[rows, :]
    c.cm = c_ref[rows, :]
    c.dt = dt_ref[rows, :]
    c.dta = c.dt * a_ref[...]
    c.incl = _cumsum_rows(tri_ref[...], c.dta)


def _ssd_weights(c, e_ref):
    L = SSM_CHUNK
    total = c.incl[L - 1:L, :]
    if c.backward:
        c.pos = c.incl - c.dta
        w_state = c.dt * jnp.exp(c.pos)
        w_out = jnp.exp(total - c.pos)
    else:
        c.pos = c.incl
        w_state = c.dt * jnp.exp(total - c.pos)
        w_out = jnp.exp(c.pos)
    e = e_ref[...]
    c.w_out_x = _expand(w_out, e)
    c.carry = c.w_out_x[0:1] if c.backward else c.w_out_x[L - 1:L]
    c.xs = (c.x.astype(F32) * _expand(w_state, e)).astype(BF16)
    c.b_t = c.bm.astype(F32).T.astype(BF16)
    c.cb = lax.dot_general(c.cm, c.bm, _NT, preferred_element_type=F32)
    c.pos_t = c.pos.T
    c.dt_t = c.dt.T


def _ssd_state(c, h_in):
    c.y = _dot(c.cm, h_in.astype(BF16)) * c.w_out_x
    return h_in * c.carry + _dot(c.b_t, c.xs)


def _ssd_diag(c, y_ref, *, hpg):
    L = SSM_CHUNK
    rows = slice(c.r0, c.r0 + L)
    li = lax.broadcasted_iota(jnp.int32, (L, L), 0)
    si = lax.broadcasted_iota(jnp.int32, (L, L), 1)
    mask = (si >= li) if c.backward else (li >= si)
    lane = lax.broadcasted_iota(jnp.int32, (L, 2 * SSM_HEAD_DIM), 1)
    for jp in range(hpg // 2):
        sl = slice(jp * 2 * SSM_HEAD_DIM, (jp + 1) * 2 * SSM_HEAD_DIM)
        xp = c.x[:, sl]
        outs = []
        for j in (c.lane0 + 2 * jp, c.lane0 + 2 * jp + 1):
            col = c.pos[:, j:j + 1]
            row = c.pos_t[j:j + 1, :]
            d = (row - col) if c.backward else (col - row)
            w = c.cb * jnp.exp(jnp.where(mask, d, -1e30)) * c.dt_t[j:j + 1, :]
            outs.append(_dot(w.astype(BF16), xp))
        y_ref[rows, sl] = (c.y[:, sl] + jnp.where(lane < SSM_HEAD_DIM, outs[0], outs[1])).astype(y_ref.dtype)


def _ssd_kernel(xf, bf, cf, dtf, xb, bb, cb, dtb, a_ref, tri_ref, ef_ref, eb_ref,
                yf_ref, yb_ref, hf_sc, hb_sc, *, hpg, cps):
    @pl.when(pl.program_id(2) == 0)
    def _():
        hf_sc[...] = jnp.zeros(hf_sc.shape, F32)
        hb_sc[...] = jnp.zeros(hb_sc.shape, F32)

    fwd, bwd = [], []
    for k in range(cps):
        f, b = _Chunk(), _Chunk()
        f.r0, f.lane0, f.backward = k * SSM_CHUNK, 0, False
        b.r0, b.lane0, b.backward = (cps - 1 - k) * SSM_CHUNK, hpg, True
        fwd.append(f)
        bwd.append(b)
    for f, b in zip(fwd, bwd):
        _ssd_positions(f, xf, bf, cf, dtf, a_ref, tri_ref)
        _ssd_positions(b, xb, bb, cb, dtb, a_ref, tri_ref)
    for f, b in zip(fwd, bwd):
        _ssd_weights(f, ef_ref)
        _ssd_weights(b, eb_ref)
    hf = hf_sc[...]
    hb = hb_sc[...]
    for f, b in zip(fwd, bwd):
        hf = _ssd_state(f, hf)
        hb = _ssd_state(b, hb)
    hf_sc[...] = hf
    hb_sc[...] = hb
    for f, b in zip(fwd, bwd):
        _ssd_diag(f, yf_ref, hpg=hpg)
        _ssd_diag(b, yb_ref, hpg=hpg)


def _ssd(xbc, dt, a_pad, *, batch, seq, d_inner, cps):
    t = xbc.shape[0]
    L = SSM_CHUNK
    rows = cps * L
    nc = seq // rows
    gw = d_inner // SSM_N_GROUPS
    hpg = gw // SSM_HEAD_DIM
    b0 = d_inner // SSM_D_STATE
    c0 = b0 + SSM_N_GROUPS
    tri = jnp.asarray(np.tril(np.ones((L, L))), BF16)
    head_of_lane = np.arange(gw) // SSM_HEAD_DIM
    ef = jnp.asarray(np.arange(LANES)[:, None] == head_of_lane[None, :], BF16)
    eb = jnp.asarray(np.arange(LANES)[:, None] == (head_of_lane[None, :] + hpg), BF16)

    def fwd(col):
        return lambda b, g, c: (b * nc + c, col(g))

    def bwd(col):
        return lambda b, g, c: (b * nc + (nc - 1 - c), col(g))

    def specs(order):
        return [pl.BlockSpec((rows, gw), order(lambda g: g)),
                pl.BlockSpec((rows, SSM_D_STATE), order(lambda g: b0 + g)),
                pl.BlockSpec((rows, SSM_D_STATE), order(lambda g: c0 + g)),
                pl.BlockSpec((rows, LANES), order(lambda g: g))]

    in_specs = specs(fwd) + specs(bwd) + [
        pl.BlockSpec((1, LANES), lambda b, g, c: (0, g)),
        _resident((L, L)), _resident((LANES, gw)), _resident((LANES, gw))]
    return pl.pallas_call(
        functools.partial(_ssd_kernel, hpg=hpg, cps=cps),
        grid=(batch, SSM_N_GROUPS, nc),
        in_specs=in_specs,
        out_specs=[pl.BlockSpec((rows, gw), fwd(lambda g: g)), pl.BlockSpec((rows, gw), bwd(lambda g: g))],
        out_shape=[jax.ShapeDtypeStruct((t, d_inner), BF16), jax.ShapeDtypeStruct((t, d_inner), BF16)],
        scratch_shapes=[pltpu.VMEM((SSM_D_STATE, gw), F32), pltpu.VMEM((SSM_D_STATE, gw), F32)],
        compiler_params=_params("parallel", "parallel", "arbitrary"),
        name="ssd_scan",
    )(xbc, xbc, xbc, dt, xbc, xbc, xbc, dt, a_pad, tri, ef, eb)


def _ssd_out_kernel(yf_ref, yb_ref, xs_ref, z_ref, dsk_ref, ng_ref, w_ref, x_ref, gp_ref, gn_ref,
                    xo_ref, ho_ref, acc, ss, *, d_inner):
    kk = pl.program_id(1)

    @pl.when(kk == 0)
    def _():
        acc[...] = jnp.zeros(acc.shape, F32)
        ss[...] = jnp.zeros(ss.shape, F32)

    tk = w_ref.shape[0]
    sq = None
    part = None
    for c0 in range(0, tk, OUT_SUB):
        cols = slice(c0, c0 + OUT_SUB)
        y = (yf_ref[:, cols].astype(F32) + yb_ref[:, cols].astype(F32)
             + xs_ref[:, cols].astype(F32) * dsk_ref[:, cols])
        y = y * _silu(z_ref[:, cols].astype(F32))
        y2 = y * y
        for l0 in range(0, OUT_SUB, LANES):
            sq = y2[:, l0:l0 + LANES] if sq is None else sq + y2[:, l0:l0 + LANES]
        d = _dot((y * ng_ref[:, cols]).astype(BF16), w_ref[cols, :])
        part = d if part is None else part + d
    acc[...] += part
    ss[...] += sq

    @pl.when(kk == pl.num_programs(1) - 1)
    def _():
        ssum = jnp.sum(ss[...], axis=-1, keepdims=True)
        m = acc[...] * lax.rsqrt(ssum * (1.0 / d_inner) + EPS)
        xn = x_ref[...] + _rms(m, gp_ref[...])
        xo_ref[...] = xn
        ho_ref[...] = _rms(xn, gn_ref[...]).astype(BF16)


def _ssd_out(yf, yb, xbc, z, d_exp, norm_g, w, x, g_post, g_next, *, tm, tk):
    t, d = x.shape
    d_inner = yf.shape[1]
    lhs = pl.BlockSpec((tm, tk), lambda i, k: (i, k))
    vec = pl.BlockSpec((1, tk), lambda i, k: (0, k))
    row = pl.BlockSpec((tm, d), lambda i, k: (i, 0))
    w_spec = _resident((tk, d)) if tk == d_inner else pl.BlockSpec((tk, d), lambda i, k: (k, 0))
    return pl.pallas_call(
        functools.partial(_ssd_out_kernel, d_inner=d_inner),
        grid=(t // tm, d_inner // tk),
        in_specs=[lhs, lhs, lhs, lhs, vec, vec, w_spec, row, _resident((1, d)), _resident((1, d))],
        out_specs=[row, row],
        out_shape=[jax.ShapeDtypeStruct((t, d), F32), jax.ShapeDtypeStruct((t, d), BF16)],
        scratch_shapes=[pltpu.VMEM((tm, d), F32), pltpu.VMEM((tm, LANES), F32)],
        compiler_params=_params("parallel", "arbitrary"),
        name="ssd_out",
    )(yf, yb, xbc, z, d_exp, norm_g, w, x, g_post, g_next)


def _rope_tables(seq):
    rows = seq // GRID_W
    row = jnp.repeat(jnp.arange(rows, dtype=F32), GRID_W)
    col = jnp.tile(jnp.arange(GRID_W, dtype=F32), rows)
    axis_dim = HEAD_DIM // 2
    inv_freq = ROPE_THETA ** (-jnp.arange(0, axis_dim, 2, dtype=F32) / axis_dim)
    ang = jnp.concatenate([row[:, None] * inv_freq, col[:, None] * inv_freq], axis=-1)
    cos, sin = jnp.cos(ang), jnp.sin(ang)
    return jnp.concatenate([cos, cos], axis=-1), jnp.concatenate([-sin, sin], axis=-1)


def _pick(n, pref):
    t = min(pref, n)
    while n % t:
        t -= LANES
    return t


def kernel(x_prompt, x_sample, mix_pre_g, mix_post_g, ffn_pre_g, ffn_post_g, fa_w_in, fa_q_gain, fa_k_gain, fa_w_out, ssd_w_in, ssd_conv_w, ssd_conv_b, ssd_dt_bias, ssd_a_log, ssd_d, ssd_norm_g, ssd_w_out, ffn_w_up, ffn_conv_w, ffn_conv_b, ffn_w_down):
    depth = mix_pre_g.shape[0]
    d = x_prompt.shape[2]
    aw = N_Q_HEADS * HEAD_DIM
    kvw = N_KV_HEADS * HEAD_DIM
    fw = fa_w_in.shape[2] - aw - 2 * kvw
    assert fa_w_out.shape[1] == fw + aw and fw % FNET_GROUP_DIM == 0
    d_ff = ffn_w_down.shape[1]
    d_inner = ssd_w_out.shape[1]
    n_heads = d_inner // SSM_HEAD_DIM
    hpg = n_heads // SSM_N_GROUPS
    bc_w = SSM_N_GROUPS * SSM_D_STATE
    conv_dim = d_inner + 2 * bc_w
    assert hpg % 2 == 0 and 2 * hpg <= LANES and depth % 2 == 0
    tm = 512
    vec = lambda v: v.reshape(1, -1).astype(F32)

    def regroup(p):
        p = p.astype(F32).reshape(2, SSM_N_GROUPS, hpg).transpose(1, 0, 2).reshape(SSM_N_GROUPS, 2 * hpg)
        return jnp.pad(p, ((0, 0), (0, LANES - 2 * hpg))).reshape(1, SSM_N_GROUPS * LANES)

    layers = []
    for i in range(depth):
        j = i // 2
        lw = dict(ffn_pre=vec(ffn_pre_g[i]), ffn_post=vec(ffn_post_g[i]), mix_pre=vec(mix_pre_g[i]),
                  mix_post=vec(mix_post_g[i]), w_up=ffn_w_up[i].astype(BF16), w_down=ffn_w_down[i].astype(BF16),
                  ffn_cw=ffn_conv_w[i].astype(F32), ffn_cb=vec(ffn_conv_b[i]))
        if i % 2 == 0:
            wo = fa_w_out[j].astype(BF16)
            lw.update(w_in=fa_w_in[j].astype(BF16), qg=vec(fa_q_gain[j]), kg=vec(fa_k_gain[j]),
                      wo_f=wo[:fw], wo_a=wo[fw:])
        else:
            w_dt = ssd_w_in[j][:, d_inner + conv_dim:].reshape(d, 2, SSM_N_GROUPS, hpg)
            w_dt = jnp.pad(w_dt.transpose(0, 2, 1, 3).reshape(d, SSM_N_GROUPS, 2 * hpg),
                           ((0, 0), (0, 0), (0, LANES - 2 * hpg))).reshape(d, SSM_N_GROUPS * LANES)
            lw.update(w_in=ssd_w_in[j].astype(BF16), w_dt=w_dt.astype(BF16), dt_bias=regroup(ssd_dt_bias[j]),
                      a_pad=regroup(-jnp.exp(ssd_a_log[j].astype(F32))), cw=ssd_conv_w[j].astype(F32),
                      cb=vec(ssd_conv_b[j]), d_exp=jnp.repeat(ssd_d[j].astype(F32), SSM_HEAD_DIM).reshape(1, d_inner),
                      norm_g=vec(ssd_norm_g[j]), w_out=ssd_w_out[j].astype(BF16))
        layers.append(lw)

    def trunk(xin):
        batch, seq, _ = xin.shape
        assert seq % tm == 0
        t = batch * seq
        x = xin.reshape(t, d)
        cosf, sinf = _rope_tables(seq)
        h = None
        for i, lw in enumerate(layers):
            if i % 2 == 0:
                u, q, k, v = _inproj0(x, lw["mix_pre"], lw["w_in"], lw["qg"], lw["kg"], cosf, sinf,
                                      seq=seq, fw=fw, aw=aw, kvw=kvw, tm=tm)
                a = _attention(q.reshape(batch, seq, aw), k.reshape(batch, seq, kvw), v.reshape(batch, seq, kvw),
                               tq=_pick(seq, 1024), tk=_pick(seq, 1024)).reshape(t, aw)
                f = _fourier_mix(u, batch=batch, seq=seq, tn=_pick(128 * fw, 8192))
                x, h = _proj_res([f, a], [lw["wo_f"], lw["wo_a"]], x, lw["mix_post"], lw["ffn_pre"], tm=tm, tk=fw)
            else:
                z, xbc = _inproj1(h, lw["w_in"], lw["cw"], lw["cb"], seq=seq, tm=tm,
                                  tn=math.gcd(d_inner, conv_dim), sub=512, d_inner=d_inner, conv_dim=conv_dim)
                dt = _dt_proj(h, lw["w_dt"], lw["dt_bias"], tm=tm)
                yf, yb = _ssd(xbc, dt, lw["a_pad"], batch=batch, seq=seq, d_inner=d_inner, cps=8)
                x, h = _ssd_out(yf, yb, xbc, z, lw["d_exp"], lw["norm_g"], lw["w_out"], x,
                                lw["mix_post"], lw["ffn_pre"], tm=tm // 2, tk=d_inner)
            g_next = layers[i + 1]["mix_pre"] if i + 1 < depth else None
            x, h = _ffn(h, lw["w_up"], lw["ffn_cw"], lw["ffn_cb"], lw["w_down"], x, lw["ffn_post"], g_next,
                        seq=seq, tm=tm, tn=_pick(d_ff, 512))
        return x.reshape(batch, seq, d)

    return (trunk(x_prompt), trunk(x_sample))
```

```python
import functools
import math

import jax
import jax.numpy as jnp
import numpy as np
from jax import lax
from jax.experimental import pallas as pl
from jax.experimental.pallas import tpu as pltpu

F32 = jnp.float32
BF16 = jnp.bfloat16

EPS = 1e-6
GRID_W = 64
HEAD_DIM = 128
N_Q_HEADS = 8
N_KV_HEADS = 2
FNET_GROUP_DIM = 128
ROPE_THETA = 10000.0
SSM_HEAD_DIM = 64
SSM_N_GROUPS = 8
SSM_D_STATE = 128
SSM_CHUNK = 128
LANES = 128
HALO = 16
OUT_SUB = 256
IN1_STEPS = 4
VMEM_LIMIT = 56 * 1024 * 1024

_NT = (((1,), (1,)), ((), ()))


def _params(*sem):
    return pltpu.CompilerParams(dimension_semantics=sem, vmem_limit_bytes=VMEM_LIMIT)


def _resident(shape):
    nd = len(shape)
    return pl.BlockSpec(shape, lambda *_: (0,) * nd, pipeline_mode=pl.Buffered(1))


def _rms(x, g):
    return x * lax.rsqrt(jnp.mean(x * x, axis=-1, keepdims=True) + EPS) * g


def _dot(a, b):
    return jnp.dot(a, b, preferred_element_type=F32)


def _inproj0_kernel(x_ref, g_ref, w_ref, qg_ref, kg_ref, cos_ref, sin_ref,
                    u_ref, q_ref, k_ref, v_ref, *, fw, aw, kvw, scale):
    h = _rms(x_ref[...], g_ref[...]).astype(BF16)
    cosf = cos_ref[...]
    sinf = sin_ref[...]

    def norm_rope(t, gain, mult):
        t = _rms(t, gain)
        t = t * cosf + pltpu.roll(t, HEAD_DIM // 2, 1) * sinf
        return (t * mult).astype(BF16)

    q = _dot(h, w_ref[:, fw:fw + aw])
    k = _dot(h, w_ref[:, fw + aw:fw + aw + kvw])
    v = _dot(h, w_ref[:, fw + aw + kvw:])
    u = _dot(h, w_ref[:, :fw])
    for hh in range(aw // HEAD_DIM):
        sl = slice(hh * HEAD_DIM, (hh + 1) * HEAD_DIM)
        q_ref[:, sl] = norm_rope(q[:, sl], qg_ref[...], scale)
    for hh in range(kvw // HEAD_DIM):
        sl = slice(hh * HEAD_DIM, (hh + 1) * HEAD_DIM)
        k_ref[:, sl] = norm_rope(k[:, sl], kg_ref[...], 1.0)
    v_ref[...] = v.astype(BF16)
    u_ref[...] = u.astype(BF16)


def _inproj0(x, g, w, qg, kg, cosf, sinf, *, seq, fw, aw, kvw, tm):
    t, d = x.shape
    n = w.shape[1]
    tps = seq // tm
    kern = functools.partial(_inproj0_kernel, fw=fw, aw=aw, kvw=kvw, scale=math.log2(math.e) * HEAD_DIM ** -0.5)
    row = lambda width: pl.BlockSpec((tm, width), lambda i: (i, 0))
    return pl.pallas_call(
        kern,
        grid=(t // tm,),
        in_specs=[row(d), _resident((1, d)), _resident((d, n)),
                  _resident((1, HEAD_DIM)), _resident((1, HEAD_DIM)),
                  pl.BlockSpec((tm, HEAD_DIM), lambda i: (i % tps, 0)),
                  pl.BlockSpec((tm, HEAD_DIM), lambda i: (i % tps, 0))],
        out_specs=[row(fw), row(aw), row(kvw), row(kvw)],
        out_shape=[jax.ShapeDtypeStruct((t, fw), BF16), jax.ShapeDtypeStruct((t, aw), BF16),
                   jax.ShapeDtypeStruct((t, kvw), BF16), jax.ShapeDtypeStruct((t, kvw), BF16)],
        compiler_params=_params("parallel"),
        name="inproj0",
    )(x, g, w, qg, kg, cosf, sinf)


def _attn_kernel(q_ref, k_ref, v_ref, o_ref, m_sc, acc_sc, *, group):
    ki = pl.program_id(3)

    @pl.when(ki == 0)
    def _():
        m_sc[...] = jnp.full(m_sc.shape, -jnp.inf, F32)
        acc_sc[...] = jnp.zeros(acc_sc.shape, F32)

    k = k_ref[0]
    v = v_ref[0]
    v_ext = jnp.concatenate([v, jnp.ones_like(v)], axis=1)
    heads = range(group)
    s = [lax.dot_general(q_ref[0, :, j * HEAD_DIM:(j + 1) * HEAD_DIM], k, _NT,
                         preferred_element_type=F32).astype(BF16) for j in heads]
    p, alpha = [], []
    for j in heads:
        m_prev = m_sc[j]
        m_new = jnp.maximum(m_prev, jnp.max(s[j], axis=-1, keepdims=True).astype(F32))
        alpha.append(jnp.exp2(m_prev - m_new))
        p.append(jnp.exp2(s[j] - m_new.astype(BF16)))
        m_sc[j] = m_new
    for j in heads:
        acc_sc[j] = alpha[j] * acc_sc[j] + _dot(p[j], v_ext)

    @pl.when(ki == pl.num_programs(3) - 1)
    def _():
        for j in heads:
            acc = acc_sc[j]
            o_ref[0, :, j * HEAD_DIM:(j + 1) * HEAD_DIM] = (acc[:, :HEAD_DIM] / acc[:, HEAD_DIM:]).astype(BF16)


def _attention(q, k, v, *, tq, tk):
    b, s, aw = q.shape
    group = aw // HEAD_DIM // N_KV_HEADS
    gw = group * HEAD_DIM
    return pl.pallas_call(
        functools.partial(_attn_kernel, group=group),
        grid=(b, N_KV_HEADS, s // tq, s // tk),
        in_specs=[pl.BlockSpec((1, tq, gw), lambda bi, h, qi, ki: (bi, qi, h)),
                  pl.BlockSpec((1, tk, HEAD_DIM), lambda bi, h, qi, ki: (bi, ki, h)),
                  pl.BlockSpec((1, tk, HEAD_DIM), lambda bi, h, qi, ki: (bi, ki, h))],
        out_specs=pl.BlockSpec((1, tq, gw), lambda bi, h, qi, ki: (bi, qi, h)),
        out_shape=jax.ShapeDtypeStruct((b, s, aw), BF16),
        scratch_shapes=[pltpu.VMEM((group, tq, 1), F32), pltpu.VMEM((group, tq, 2 * HEAD_DIM), F32)],
        compiler_params=_params("parallel", "parallel", "parallel", "arbitrary"),
        name="attention",
    )(q, k, v)


def _fourier1_kernel(f_ref, u_ref, y_ref, *, n1):
    y = _dot(f_ref[...], u_ref[0])
    y_ref[0, 0] = y[:n1].astype(BF16)
    y_ref[0, 1] = y[n1:].astype(BF16)


def _fourier2_kernel(m_ref, y_ref, cc_ref, sc_ref, o_ref, *, n2, groups):
    y = y_ref[0].reshape(2 * n2, groups * FNET_GROUP_DIM)
    g = _dot(m_ref[0], y)
    gr = g[:n2].astype(BF16)
    gi = g[n2:].astype(BF16)
    for c in range(groups):
        sl = slice(c * FNET_GROUP_DIM, (c + 1) * FNET_GROUP_DIM)
        o_ref[0, :, sl] = (_dot(gr[:, sl], cc_ref[...]) + _dot(gi[:, sl], sc_ref[...])).astype(BF16)


def _fourier_tables(seq):
    n2 = 128
    n1 = seq // n2
    a1 = 2.0 * np.pi * np.outer(np.arange(n1), np.arange(n1)) / n1
    f1 = np.concatenate([np.cos(a1), -np.sin(a1)], axis=0)
    kk = np.arange(n1)[:, None, None] + n1 * np.arange(n2)[None, :, None]
    th = 2.0 * np.pi * (kk * np.arange(n2)[None, None, :] % seq) / seq
    mc, ms = np.cos(th), np.sin(th)
    m = np.concatenate([np.concatenate([mc, ms], axis=2),
                        np.concatenate([-ms, mc], axis=2)], axis=1)
    ac = 2.0 * np.pi * np.outer(np.arange(FNET_GROUP_DIM), np.arange(FNET_GROUP_DIM)) / FNET_GROUP_DIM
    norm = 1.0 / math.sqrt(seq * FNET_GROUP_DIM)
    return (jnp.asarray(f1, BF16), jnp.asarray(m, BF16),
            jnp.asarray(np.cos(ac) * norm, BF16), jnp.asarray(np.sin(ac) * norm, BF16))


def _fourier_mix(u, *, batch, seq, tn):
    t, width = u.shape
    n2 = 128
    n1 = seq // n2
    groups = width // FNET_GROUP_DIM
    f1, m, cc, sc = _fourier_tables(seq)
    cols = n2 * width
    y = pl.pallas_call(
        functools.partial(_fourier1_kernel, n1=n1),
        grid=(batch, cols // tn),
        in_specs=[_resident((2 * n1, n1)),
                  pl.BlockSpec((1, n1, tn), lambda b, j: (b, 0, j))],
        out_specs=pl.BlockSpec((1, 2, n1, tn), lambda b, j: (b, 0, 0, j)),
        out_shape=jax.ShapeDtypeStruct((batch, 2, n1, cols), BF16),
        compiler_params=_params("parallel", "parallel"),
        name="fourier_seq1",
    )(f1, u.reshape(batch, n1, cols))
    y = y.reshape(batch, 2, seq, width)
    out = pl.pallas_call(
        functools.partial(_fourier2_kernel, n2=n2, groups=groups),
        grid=(n1, batch),
        in_specs=[pl.BlockSpec((1, 2 * n2, 2 * n2), lambda k1, b: (k1, 0, 0)),
                  pl.BlockSpec((1, 2, n2, width), lambda k1, b: (b, 0, k1, 0)),
                  _resident((FNET_GROUP_DIM, FNET_GROUP_DIM)),
                  _resident((FNET_GROUP_DIM, FNET_GROUP_DIM))],
        out_specs=pl.BlockSpec((1, n2, width), lambda k1, b: (b, 0, k1)),
        out_shape=jax.ShapeDtypeStruct((batch, n2, n1 * width), BF16),
        compiler_params=_params("parallel", "parallel"),
        name="fourier_seq2",
    )(m, y, cc, sc)
    return out.reshape(t, width)


def _proj_res_kernel(*refs, n_lhs, has_next):
    lhs = refs[:n_lhs]
    ws = refs[n_lhs:2 * n_lhs]
    x_ref, gp_ref = refs[2 * n_lhs:2 * n_lhs + 2]
    pos = 2 * n_lhs + 2
    gn_ref = refs[pos] if has_next else None
    pos += int(has_next)
    xo_ref = refs[pos]
    ho_ref = refs[pos + 1] if has_next else None
    acc = refs[-1]
    kk = pl.program_id(1)

    @pl.when(kk == 0)
    def _():
        acc[...] = jnp.zeros(acc.shape, F32)

    part = _dot(lhs[0][...], ws[0][...])
    for a, w in zip(lhs[1:], ws[1:]):
        part += _dot(a[...], w[...])
    acc[...] += part

    @pl.when(kk == pl.num_programs(1) - 1)
    def _():
        xn = x_ref[...] + _rms(acc[...], gp_ref[...])
        xo_ref[...] = xn
        if has_next:
            ho_ref[...] = _rms(xn, gn_ref[...]).astype(BF16)


def _proj_res(lhs, ws, x, g_post, g_next, *, tm, tk):
    t, d = x.shape
    n_lhs = len(lhs)
    kdim = lhs[0].shape[1]
    has_next = g_next is not None
    in_specs = [pl.BlockSpec((tm, tk), lambda i, k: (i, k)) for _ in lhs]
    in_specs += [pl.BlockSpec((tk, d), lambda i, k: (k, 0)) for _ in ws]
    in_specs += [pl.BlockSpec((tm, d), lambda i, k: (i, 0)), _resident((1, d))]
    args = list(lhs) + list(ws) + [x, g_post]
    out_specs = [pl.BlockSpec((tm, d), lambda i, k: (i, 0))]
    out_shape = [jax.ShapeDtypeStruct((t, d), F32)]
    if has_next:
        in_specs.append(_resident((1, d)))
        args.append(g_next)
        out_specs.append(pl.BlockSpec((tm, d), lambda i, k: (i, 0)))
        out_shape.append(jax.ShapeDtypeStruct((t, d), BF16))
    res = pl.pallas_call(
        functools.partial(_proj_res_kernel, n_lhs=n_lhs, has_next=has_next),
        grid=(t // tm, kdim // tk),
        in_specs=in_specs, out_specs=out_specs, out_shape=out_shape,
        scratch_shapes=[pltpu.VMEM((tm, d), F32)],
        compiler_params=_params("parallel", "arbitrary"),
        name="proj_res",
    )(*args)
    return res if has_next else (res[0], None)


def _gelu_tanh(x):
    return 0.5 * x * (1.0 + jnp.tanh(math.sqrt(2.0 / math.pi) * (x + 0.044715 * x * x * x)))


def _silu(x):
    return x * (1.0 / (1.0 + jnp.exp(-x)))


def _assemble_halo(hext, h_ref, hp_ref, hn_ref, *, tm, tps):
    i = pl.program_id(0)
    first = (i % tps) == 0
    last = (i % tps) == tps - 1
    prev_blk = jnp.where(first, jnp.zeros_like(hp_ref[...]), hp_ref[...])
    next_blk = jnp.where(last, jnp.zeros_like(hn_ref[...]), hn_ref[...])
    row = lax.broadcasted_iota(jnp.int32, prev_blk.shape, 0)
    hext[0:tm] = h_ref[...]
    hext[tm:] = jnp.where(row < HALO // 2, next_blk, prev_blk)


def _conv_rows(u, cw, cb, *, tm):
    rows = tm + HALO
    c = cb + cw[0:1] * pltpu.roll(u, 1, 0) + cw[1:2] * u + cw[2:3] * pltpu.roll(u, rows - 1, 0)
    return c[0:tm]


def _halo_specs(t, tm, kdim):
    hb = tm // HALO
    last_blk = t // HALO - 1
    return [pl.BlockSpec((tm, kdim), lambda i, j: (i, 0)),
            pl.BlockSpec((HALO, kdim), lambda i, j: (jnp.maximum(i * hb - 1, 0), 0)),
            pl.BlockSpec((HALO, kdim), lambda i, j: (jnp.minimum((i + 1) * hb, last_blk), 0))]


def _inproj1_kernel(h_ref, hp_ref, hn_ref, w_ref, cw_ref, cb_ref, z_ref, xbc_ref, hext, *, tm, tps, sub):
    xw = xbc_ref.shape[1]
    zw = z_ref.shape[1]

    @pl.when(pl.program_id(1) == 0)
    def _():
        _assemble_halo(hext, h_ref, hp_ref, hn_ref, tm=tm, tps=tps)

    hx = hext[...]
    h = hext[0:tm, :]
    ux = [_dot(hx, w_ref[:, c0:c0 + sub]) for c0 in range(0, xw, sub)]
    uz = [_dot(h, w_ref[:, xw + c0:xw + c0 + sub]) for c0 in range(0, zw, sub)]
    for k, u in enumerate(ux):
        cols = slice(k * sub, (k + 1) * sub)
        y = _conv_rows(u, cw_ref[:, cols], cb_ref[:, cols], tm=tm)
        xbc_ref[:, cols] = _silu(y).astype(BF16)
    for k, u in enumerate(uz):
        z_ref[:, k * sub:(k + 1) * sub] = u.astype(BF16)


def _inproj1(h, w_steps, conv_w, conv_b, *, seq, tm, steps, sub, d_inner, conv_dim):
    t, kdim = h.shape
    xw, zw = conv_dim // steps, d_inner // steps
    in_specs = _halo_specs(t, tm, kdim) + [
        pl.BlockSpec((kdim, xw + zw), lambda i, j: (0, j)),
        pl.BlockSpec((3, xw), lambda i, j: (0, j)),
        pl.BlockSpec((1, xw), lambda i, j: (0, j))]
    return pl.pallas_call(
        functools.partial(_inproj1_kernel, tm=tm, tps=seq // tm, sub=sub),
        grid=(t // tm, steps),
        in_specs=in_specs,
        out_specs=[pl.BlockSpec((tm, zw), lambda i, j: (i, j)), pl.BlockSpec((tm, xw), lambda i, j: (i, j))],
        out_shape=[jax.ShapeDtypeStruct((t, d_inner), BF16), jax.ShapeDtypeStruct((t, conv_dim), BF16)],
        scratch_shapes=[pltpu.VMEM((tm + HALO, kdim), BF16)],
        compiler_params=_params("parallel", "arbitrary"),
        name="inproj1",
    )(h, h, h, w_steps, conv_w, conv_b)


def _ffn_kernel(*refs, tm, tps, nj, has_next):
    (h_ref, hp_ref, hn_ref, wg_ref, wv_ref, cwg_ref, cwv_ref, cbg_ref, cbv_ref,
     wd_ref, wdl_ref, x_ref, gp_ref) = refs[:13]
    pos = 13
    gn_ref = refs[pos] if has_next else None
    pos += int(has_next)
    xo_ref = refs[pos]
    ho_ref = refs[pos + 1] if has_next else None
    hext, act_a, act_b, acc = refs[-4:]
    j = pl.program_id(1)

    def step(act_prev, act_cur):
        hx = hext[...]
        ug = _dot(hx, wg_ref[...])
        uv = _dot(hx, wv_ref[...])
        if act_prev is None:
            acc[...] = jnp.zeros(acc.shape, F32)
        else:
            acc[...] += _dot(act_prev[...], wd_ref[...])
        gate = _conv_rows(ug, cwg_ref[...], cbg_ref[...], tm=tm)
        val = _conv_rows(uv, cwv_ref[...], cbv_ref[...], tm=tm)
        act_cur[...] = (_gelu_tanh(gate) * val).astype(BF16)

    @pl.when(j == 0)
    def _():
        _assemble_halo(hext, h_ref, hp_ref, hn_ref, tm=tm, tps=tps)
        step(None, act_a)

    @pl.when((j > 0) & (j % 2 == 0))
    def _():
        step(act_b, act_a)

    @pl.when(j % 2 == 1)
    def _():
        step(act_a, act_b)

    @pl.when(j == nj - 1)
    def _():
        act_last = act_a if (nj - 1) % 2 == 0 else act_b
        m = acc[...] + _dot(act_last[...], wdl_ref[...])
        xn = x_ref[...] + _rms(m, gp_ref[...])
        xo_ref[...] = xn
        if has_next:
            ho_ref[...] = _rms(xn, gn_ref[...]).astype(BF16)


def _ffn(h, w_up, conv_w, conv_b, w_down, x, g_post, g_next, *, seq, tm, tn):
    t, d = x.shape
    d_ff = w_down.shape[0]
    nj = d_ff // tn
    has_next = g_next is not None
    row = pl.BlockSpec((tm, d), lambda i, j: (i, 0))
    in_specs = _halo_specs(t, tm, d) + [
        pl.BlockSpec((d, tn), lambda i, j: (0, j)),
        pl.BlockSpec((d, tn), lambda i, j: (0, nj + j)),
        pl.BlockSpec((3, tn), lambda i, j: (0, j)),
        pl.BlockSpec((3, tn), lambda i, j: (0, nj + j)),
        pl.BlockSpec((1, tn), lambda i, j: (0, j)),
        pl.BlockSpec((1, tn), lambda i, j: (0, nj + j)),
        pl.BlockSpec((tn, d), lambda i, j: (jnp.maximum(j - 1, 0), 0)),
        pl.BlockSpec((tn, d), lambda i, j: (nj - 1, 0), pipeline_mode=pl.Buffered(1)),
        row, _resident((1, d))]
    args = [h, h, h, w_up, w_up, conv_w, conv_w, conv_b, conv_b, w_down, w_down, x, g_post]
    out_specs = [row]
    out_shape = [jax.ShapeDtypeStruct((t, d), F32)]
    if has_next:
        in_specs.append(_resident((1, d)))
        args.append(g_next)
        out_specs.append(row)
        out_shape.append(jax.ShapeDtypeStruct((t, d), BF16))
    res = pl.pallas_call(
        functools.partial(_ffn_kernel, tm=tm, tps=seq // tm, nj=nj, has_next=has_next),
        grid=(t // tm, nj),
        in_specs=in_specs, out_specs=out_specs, out_shape=out_shape,
        scratch_shapes=[pltpu.VMEM((tm + HALO, d), BF16), pltpu.VMEM((tm, tn), BF16),
                        pltpu.VMEM((tm, tn), BF16), pltpu.VMEM((tm, d), F32)],
        compiler_params=_params("parallel", "arbitrary"),
        name="ffn",
    )(*args)
    return res if has_next else (res[0], None)


def _dt_kernel(h_ref, w_ref, b_ref, o_ref):
    r = _dot(h_ref[...], w_ref[...]) + b_ref[...]
    o_ref[...] = jnp.maximum(r, 0.0) + jnp.log(1.0 + jnp.exp(-jnp.abs(r)))


def _dt_proj(h, w, b, *, tm):
    t, kdim = h.shape
    n = w.shape[1]
    return pl.pallas_call(
        _dt_kernel,
        grid=(t // tm,),
        in_specs=[pl.BlockSpec((tm, kdim), lambda i: (i, 0)), _resident((kdim, n)), _resident((1, n))],
        out_specs=pl.BlockSpec((tm, n), lambda i: (i, 0)),
        out_shape=jax.ShapeDtypeStruct((t, n), F32),
        compiler_params=_params("parallel"),
        name="ssd_dt",
    )(h, w, b)


def _split3(x):
    hi = x.astype(BF16)
    r = x - hi.astype(F32)
    mid = r.astype(BF16)
    lo = (r - mid.astype(F32)).astype(BF16)
    return hi, mid, lo


def _cumsum_rows(tri, x):
    hi, mid, lo = _split3(x)
    return _dot(tri, hi) + _dot(tri, mid) + _dot(tri, lo)


def _expand(x, e):
    return _dot(x.astype(BF16), e)


class _Chunk:
    pass


def _ssd_positions(c, x_ref, b_ref, c_ref, dt_ref, a_ref, tri_ref):
    rows = slice(c.r0, c.r0 + SSM_CHUNK)
    c.x = x_ref[rows, :]
    c.bm = b_ref[rows, :]
    c.cm = c_ref[rows, :]
    c.dt = dt_ref[rows, :]
    c.dta = c.dt * a_ref[...]
    c.incl = _cumsum_rows(tri_ref[...], c.dta)


def _ssd_weights(c, e_ref):
    L = SSM_CHUNK
    total = c.incl[L - 1:L, :]
    if c.backward:
        c.pos = c.incl - c.dta
        w_state = c.dt * jnp.exp(c.pos)
        w_out = jnp.exp(total - c.pos)
    else:
        c.pos = c.incl
        w_state = c.dt * jnp.exp(total - c.pos)
        w_out = jnp.exp(c.pos)
    e = e_ref[...]
    c.w_out_x = _expand(w_out, e)
    c.carry = c.w_out_x[0:1] if c.backward else c.w_out_x[L - 1:L]
    c.xs = (c.x.astype(F32) * _expand(w_state, e)).astype(BF16)
    c.b_t = c.bm.astype(F32).T.astype(BF16)
    c.cb = lax.dot_general(c.cm, c.bm, _NT, preferred_element_type=F32)
    c.pos_t = c.pos.T
    c.dt_t = c.dt.T


def _ssd_state(c, h_in):
    c.y = _dot(c.cm, h_in.astype(BF16)) * c.w_out_x
    return h_in * c.carry + _dot(c.b_t, c.xs)


def _ssd_diag(c, y_ref, *, hpg):
    L = SSM_CHUNK
    rows = slice(c.r0, c.r0 + L)
    li = lax.broadcasted_iota(jnp.int32, (L, L), 0)
    si = lax.broadcasted_iota(jnp.int32, (L, L), 1)
    mask = (si >= li) if c.backward else (li >= si)
    lane = lax.broadcasted_iota(jnp.int32, (L, 2 * SSM_HEAD_DIM), 1)
    for jp in range(hpg // 2):
        sl = slice(jp * 2 * SSM_HEAD_DIM, (jp + 1) * 2 * SSM_HEAD_DIM)
        xp = c.x[:, sl]
        outs = []
        for j in (c.lane0 + 2 * jp, c.lane0 + 2 * jp + 1):
            col = c.pos[:, j:j + 1]
            row = c.pos_t[j:j + 1, :]
            d = (row - col) if c.backward else (col - row)
            w = c.cb * jnp.exp(jnp.where(mask, d, -1e30)) * c.dt_t[j:j + 1, :]
            outs.append(_dot(w.astype(BF16), xp))
        y_ref[rows, sl] = (c.y[:, sl] + jnp.where(lane < SSM_HEAD_DIM, outs[0], outs[1])).astype(y_ref.dtype)


def _ssd_kernel(xf, bf, cf, dtf, xb, bb, cb, dtb, a_ref, tri_ref, ef_ref, eb_ref,
                yf_ref, yb_ref, hf_sc, hb_sc, *, hpg, cps):
    @pl.when(pl.program_id(2) == 0)
    def _():
        hf_sc[...] = jnp.zeros(hf_sc.shape, F32)
        hb_sc[...] = jnp.zeros(hb_sc.shape, F32)

    fwd, bwd = [], []
    for k in range(cps):
        f, b = _Chunk(), _Chunk()
        f.r0, f.lane0, f.backward = k * SSM_CHUNK, 0, False
        b.r0, b.lane0, b.backward = (cps - 1 - k) * SSM_CHUNK, hpg, True
        fwd.append(f)
        bwd.append(b)
    for f, b in zip(fwd, bwd):
        _ssd_positions(f, xf, bf, cf, dtf, a_ref, tri_ref)
        _ssd_positions(b, xb, bb, cb, dtb, a_ref, tri_ref)
    for f, b in zip(fwd, bwd):
        _ssd_weights(f, ef_ref)
        _ssd_weights(b, eb_ref)
    hf = hf_sc[...]
    hb = hb_sc[...]
    for f, b in zip(fwd, bwd):
        hf = _ssd_state(f, hf)
        hb = _ssd_state(b, hb)
    hf_sc[...] = hf
    hb_sc[...] = hb
    for f, b in zip(fwd, bwd):
        _ssd_diag(f, yf_ref, hpg=hpg)
        _ssd_diag(b, yb_ref, hpg=hpg)


def _ssd(xbc, dt, a_pad, *, batch, seq, d_inner, cps):
    t = xbc.shape[0]
    L = SSM_CHUNK
    rows = cps * L
    nc = seq // rows
    gw = d_inner // SSM_N_GROUPS
    hpg = gw // SSM_HEAD_DIM
    b0 = d_inner // SSM_D_STATE
    c0 = b0 + SSM_N_GROUPS
    tri = jnp.asarray(np.tril(np.ones((L, L))), BF16)
    head_of_lane = np.arange(gw) // SSM_HEAD_DIM
    ef = jnp.asarray(np.arange(LANES)[:, None] == head_of_lane[None, :], BF16)
    eb = jnp.asarray(np.arange(LANES)[:, None] == (head_of_lane[None, :] + hpg), BF16)

    def fwd(col):
        return lambda b, g, c: (b * nc + c, col(g))

    def bwd(col):
        return lambda b, g, c: (b * nc + (nc - 1 - c), col(g))

    def specs(order):
        return [pl.BlockSpec((rows, gw), order(lambda g: g)),
                pl.BlockSpec((rows, SSM_D_STATE), order(lambda g: b0 + g)),
                pl.BlockSpec((rows, SSM_D_STATE), order(lambda g: c0 + g)),
                pl.BlockSpec((rows, LANES), order(lambda g: g))]

    in_specs = specs(fwd) + specs(bwd) + [
        pl.BlockSpec((1, LANES), lambda b, g, c: (0, g)),
        _resident((L, L)), _resident((LANES, gw)), _resident((LANES, gw))]
    return pl.pallas_call(
        functools.partial(_ssd_kernel, hpg=hpg, cps=cps),
        grid=(batch, SSM_N_GROUPS, nc),
        in_specs=in_specs,
        out_specs=[pl.BlockSpec((rows, gw), fwd(lambda g: g)), pl.BlockSpec((rows, gw), bwd(lambda g: g))],
        out_shape=[jax.ShapeDtypeStruct((t, d_inner), BF16), jax.ShapeDtypeStruct((t, d_inner), BF16)],
        scratch_shapes=[pltpu.VMEM((SSM_D_STATE, gw), F32), pltpu.VMEM((SSM_D_STATE, gw), F32)],
        compiler_params=_params("parallel", "parallel", "arbitrary"),
        name="ssd_scan",
    )(xbc, xbc, xbc, dt, xbc, xbc, xbc, dt, a_pad, tri, ef, eb)


def _ssd_out_kernel(yf_ref, yb_ref, xs_ref, z_ref, dsk_ref, ng_ref, w_ref, x_ref, gp_ref, gn_ref,
                    xo_ref, ho_ref, acc, ss, *, d_inner):
    kk = pl.program_id(1)

    @pl.when(kk == 0)
    def _():
        acc[...] = jnp.zeros(acc.shape, F32)
        ss[...] = jnp.zeros(ss.shape, F32)

    tk = w_ref.shape[0]
    sq = None
    part = None
    for c0 in range(0, tk, OUT_SUB):
        cols = slice(c0, c0 + OUT_SUB)
        y = (yf_ref[:, cols].astype(F32) + yb_ref[:, cols].astype(F32)
             + xs_ref[:, cols].astype(F32) * dsk_ref[:, cols])
        y = y * _silu(z_ref[:, cols].astype(F32))
        y2 = y * y
        for l0 in range(0, OUT_SUB, LANES):
            sq = y2[:, l0:l0 + LANES] if sq is None else sq + y2[:, l0:l0 + LANES]
        d = _dot((y * ng_ref[:, cols]).astype(BF16), w_ref[cols, :])
        part = d if part is None else part + d
    acc[...] += part
    ss[...] += sq

    @pl.when(kk == pl.num_programs(1) - 1)
    def _():
        ssum = jnp.sum(ss[...], axis=-1, keepdims=True)
        m = acc[...] * lax.rsqrt(ssum * (1.0 / d_inner) + EPS)
        xn = x_ref[...] + _rms(m, gp_ref[...])
        xo_ref[...] = xn
        ho_ref[...] = _rms(xn, gn_ref[...]).astype(BF16)


def _ssd_out(yf, yb, xbc, z, d_exp, norm_g, w, x, g_post, g_next, *, tm, tk):
    t, d = x.shape
    d_inner = yf.shape[1]
    lhs = pl.BlockSpec((tm, tk), lambda i, k: (i, k))
    vec = pl.BlockSpec((1, tk), lambda i, k: (0, k))
    row = pl.BlockSpec((tm, d), lambda i, k: (i, 0))
    w_spec = _resident((tk, d)) if tk == d_inner else pl.BlockSpec((tk, d), lambda i, k: (k, 0))
    return pl.pallas_call(
        functools.partial(_ssd_out_kernel, d_inner=d_inner),
        grid=(t // tm, d_inner // tk),
        in_specs=[lhs, lhs, lhs, lhs, vec, vec, w_spec, row, _resident((1, d)), _resident((1, d))],
        out_specs=[row, row],
        out_shape=[jax.ShapeDtypeStruct((t, d), F32), jax.ShapeDtypeStruct((t, d), BF16)],
        scratch_shapes=[pltpu.VMEM((tm, d), F32), pltpu.VMEM((tm, LANES), F32)],
        compiler_params=_params("parallel", "arbitrary"),
        name="ssd_out",
    )(yf, yb, xbc, z, d_exp, norm_g, w, x, g_post, g_next)


def _rope_tables(seq):
    rows = seq // GRID_W
    row = jnp.repeat(jnp.arange(rows, dtype=F32), GRID_W)
    col = jnp.tile(jnp.arange(GRID_W, dtype=F32), rows)
    axis_dim = HEAD_DIM // 2
    inv_freq = ROPE_THETA ** (-jnp.arange(0, axis_dim, 2, dtype=F32) / axis_dim)
    ang = jnp.concatenate([row[:, None] * inv_freq, col[:, None] * inv_freq], axis=-1)
    cos, sin = jnp.cos(ang), jnp.sin(ang)
    return jnp.concatenate([cos, cos], axis=-1), jnp.concatenate([-sin, sin], axis=-1)


def _pick(n, pref):
    t = min(pref, n)
    while n % t:
        t -= LANES
    return t


def kernel(x_prompt, x_sample, mix_pre_g, mix_post_g, ffn_pre_g, ffn_post_g, fa_w_in, fa_q_gain, fa_k_gain, fa_w_out, ssd_w_in, ssd_conv_w, ssd_conv_b, ssd_dt_bias, ssd_a_log, ssd_d, ssd_norm_g, ssd_w_out, ffn_w_up, ffn_conv_w, ffn_conv_b, ffn_w_down):
    depth = mix_pre_g.shape[0]
    d = x_prompt.shape[2]
    aw = N_Q_HEADS * HEAD_DIM
    kvw = N_KV_HEADS * HEAD_DIM
    fw = fa_w_in.shape[2] - aw - 2 * kvw
    assert fa_w_out.shape[1] == fw + aw and fw % FNET_GROUP_DIM == 0
    d_ff = ffn_w_down.shape[1]
    d_inner = ssd_w_out.shape[1]
    n_heads = d_inner // SSM_HEAD_DIM
    hpg = n_heads // SSM_N_GROUPS
    bc_w = SSM_N_GROUPS * SSM_D_STATE
    conv_dim = d_inner + 2 * bc_w
    assert hpg % 2 == 0 and 2 * hpg <= LANES and depth % 2 == 0
    tm = 512
    vec = lambda v: v.reshape(1, -1).astype(F32)

    def regroup(p):
        p = p.astype(F32).reshape(2, SSM_N_GROUPS, hpg).transpose(1, 0, 2).reshape(SSM_N_GROUPS, 2 * hpg)
        return jnp.pad(p, ((0, 0), (0, LANES - 2 * hpg))).reshape(1, SSM_N_GROUPS * LANES)

    layers = []
    for i in range(depth):
        j = i // 2
        lw = dict(ffn_pre=vec(ffn_pre_g[i]), ffn_post=vec(ffn_post_g[i]), mix_pre=vec(mix_pre_g[i]),
                  mix_post=vec(mix_post_g[i]), w_up=ffn_w_up[i].astype(BF16), w_down=ffn_w_down[i].astype(BF16),
                  ffn_cw=ffn_conv_w[i].astype(F32), ffn_cb=vec(ffn_conv_b[i]))
        if i % 2 == 0:
            wo = fa_w_out[j].astype(BF16)
            lw.update(w_in=fa_w_in[j].astype(BF16), qg=vec(fa_q_gain[j]), kg=vec(fa_k_gain[j]),
                      wo_f=wo[:fw], wo_a=wo[fw:])
        else:
            w_dt = ssd_w_in[j][:, d_inner + conv_dim:].reshape(d, 2, SSM_N_GROUPS, hpg)
            w_dt = jnp.pad(w_dt.transpose(0, 2, 1, 3).reshape(d, SSM_N_GROUPS, 2 * hpg),
                           ((0, 0), (0, 0), (0, LANES - 2 * hpg))).reshape(d, SSM_N_GROUPS * LANES)
            w_bf = ssd_w_in[j].astype(BF16)
            xw, zw = conv_dim // IN1_STEPS, d_inner // IN1_STEPS
            w_steps = jnp.concatenate(
                [w_bf[:, d_inner:d_inner + conv_dim].reshape(d, IN1_STEPS, xw),
                 w_bf[:, :d_inner].reshape(d, IN1_STEPS, zw)], axis=2).reshape(d, conv_dim + d_inner)
            lw.update(w_in=w_steps, w_dt=w_dt.astype(BF16), dt_bias=regroup(ssd_dt_bias[j]),
                      a_pad=regroup(-jnp.exp(ssd_a_log[j].astype(F32))), cw=ssd_conv_w[j].astype(F32),
                      cb=vec(ssd_conv_b[j]), d_exp=jnp.repeat(ssd_d[j].astype(F32), SSM_HEAD_DIM).reshape(1, d_inner),
                      norm_g=vec(ssd_norm_g[j]), w_out=ssd_w_out[j].astype(BF16))
        layers.append(lw)

    def trunk(xin):
        batch, seq, _ = xin.shape
        assert seq % tm == 0
        t = batch * seq
        x = xin.reshape(t, d)
        cosf, sinf = _rope_tables(seq)
        h = None
        for i, lw in enumerate(layers):
            if i % 2 == 0:
                u, q, k, v = _inproj0(x, lw["mix_pre"], lw["w_in"], lw["qg"], lw["kg"], cosf, sinf,
                                      seq=seq, fw=fw, aw=aw, kvw=kvw, tm=tm)
                a = _attention(q.reshape(batch, seq, aw), k.reshape(batch, seq, kvw), v.reshape(batch, seq, kvw),
                               tq=_pick(seq, 1024), tk=_pick(seq, 1024)).reshape(t, aw)
                f = _fourier_mix(u, batch=batch, seq=seq, tn=_pick(128 * fw, 8192))
                x, h = _proj_res([f, a], [lw["wo_f"], lw["wo_a"]], x, lw["mix_post"], lw["ffn_pre"], tm=tm, tk=fw)
            else:
                z, xbc = _inproj1(h, lw["w_in"], lw["cw"], lw["cb"], seq=seq, tm=tm,
                                  steps=IN1_STEPS, sub=512, d_inner=d_inner, conv_dim=conv_dim)
                dt = _dt_proj(h, lw["w_dt"], lw["dt_bias"], tm=tm)
                yf, yb = _ssd(xbc, dt, lw["a_pad"], batch=batch, seq=seq, d_inner=d_inner, cps=8)
                x, h = _ssd_out(yf, yb, xbc, z, lw["d_exp"], lw["norm_g"], lw["w_out"], x,
                                lw["mix_post"], lw["ffn_pre"], tm=tm // 2, tk=d_inner)
            g_next = layers[i + 1]["mix_pre"] if i + 1 < depth else None
            x, h = _ffn(h, lw["w_up"], lw["ffn_cw"], lw["ffn_cb"], lw["w_down"], x, lw["ffn_post"], g_next,
                        seq=seq, tm=tm, tn=_pick(d_ff, 512))
        return x.reshape(batch, seq, d)

    return (trunk(x_prompt), trunk(x_sample))
```

```python
import functools
import math

import jax
import jax.numpy as jnp
import numpy as np
from jax import lax
from jax.experimental import pallas as pl
from jax.experimental.pallas import tpu as pltpu

F32 = jnp.float32
BF16 = jnp.bfloat16

EPS = 1e-6
GRID_W = 64
HEAD_DIM = 128
N_Q_HEADS = 8
N_KV_HEADS = 2
FNET_GROUP_DIM = 128
ROPE_THETA = 10000.0
SSM_HEAD_DIM = 64
SSM_N_GROUPS = 8
SSM_D_STATE = 128
SSM_CHUNK = 128
LANES = 128
HALO = 16
OUT_SUB = 256
IN1_STEPS = 4
IN1_SUB = 512
TOKEN_TILE = 512
SSD_OUT_TILE = 256
ATTN_TILE = 1024
FOURIER_LANES = 8192
FFN_CHUNK = 512
SSD_CHUNKS_PER_STEP = 8
VMEM_LIMIT = 56 * 1024 * 1024

_NT = (((1,), (1,)), ((), ()))


def _params(*sem):
    return pltpu.CompilerParams(dimension_semantics=sem, vmem_limit_bytes=VMEM_LIMIT)


def _resident(shape):
    nd = len(shape)
    return pl.BlockSpec(shape, lambda *_: (0,) * nd, pipeline_mode=pl.Buffered(1))


def _rms(x, g):
    return x * lax.rsqrt(jnp.mean(x * x, axis=-1, keepdims=True) + EPS) * g


def _dot(a, b):
    return jnp.dot(a, b, preferred_element_type=F32)


def _inproj0_kernel(x_ref, g_ref, w_ref, qg_ref, kg_ref, cos_ref, sin_ref,
                    u_ref, q_ref, k_ref, v_ref, *, fw, aw, kvw, scale):
    h = _rms(x_ref[...], g_ref[...]).astype(BF16)
    cosf = cos_ref[...]
    sinf = sin_ref[...]

    def norm_rope(t, gain, mult):
        t = _rms(t, gain)
        t = t * cosf + pltpu.roll(t, HEAD_DIM // 2, 1) * sinf
        return (t * mult).astype(BF16)

    q = _dot(h, w_ref[:, fw:fw + aw])
    k = _dot(h, w_ref[:, fw + aw:fw + aw + kvw])
    v = _dot(h, w_ref[:, fw + aw + kvw:])
    u = _dot(h, w_ref[:, :fw])
    for hh in range(aw // HEAD_DIM):
        sl = slice(hh * HEAD_DIM, (hh + 1) * HEAD_DIM)
        q_ref[:, sl] = norm_rope(q[:, sl], qg_ref[...], scale)
    for hh in range(kvw // HEAD_DIM):
        sl = slice(hh * HEAD_DIM, (hh + 1) * HEAD_DIM)
        k_ref[:, sl] = norm_rope(k[:, sl], kg_ref[...], 1.0)
    v_ref[...] = v.astype(BF16)
    u_ref[...] = u.astype(BF16)


def _inproj0(x, g, w, qg, kg, cosf, sinf, *, seq, fw, aw, kvw, tm):
    t, d = x.shape
    n = w.shape[1]
    tps = seq // tm
    kern = functools.partial(_inproj0_kernel, fw=fw, aw=aw, kvw=kvw, scale=math.log2(math.e) * HEAD_DIM ** -0.5)
    row = lambda width: pl.BlockSpec((tm, width), lambda i: (i, 0))
    return pl.pallas_call(
        kern,
        grid=(t // tm,),
        in_specs=[row(d), _resident((1, d)), _resident((d, n)),
                  _resident((1, HEAD_DIM)), _resident((1, HEAD_DIM)),
                  pl.BlockSpec((tm, HEAD_DIM), lambda i: (i % tps, 0)),
                  pl.BlockSpec((tm, HEAD_DIM), lambda i: (i % tps, 0))],
        out_specs=[row(fw), row(aw), row(kvw), row(kvw)],
        out_shape=[jax.ShapeDtypeStruct((t, fw), BF16), jax.ShapeDtypeStruct((t, aw), BF16),
                   jax.ShapeDtypeStruct((t, kvw), BF16), jax.ShapeDtypeStruct((t, kvw), BF16)],
        compiler_params=_params("parallel"),
        name="inproj0",
    )(x, g, w, qg, kg, cosf, sinf)


def _attn_kernel(q_ref, k_ref, v_ref, o_ref, m_sc, acc_sc, *, group):
    ki = pl.program_id(3)

    @pl.when(ki == 0)
    def _():
        m_sc[...] = jnp.full(m_sc.shape, -jnp.inf, F32)
        acc_sc[...] = jnp.zeros(acc_sc.shape, F32)

    k = k_ref[0]
    v = v_ref[0]
    v_ext = jnp.concatenate([v, jnp.ones_like(v)], axis=1)
    heads = range(group)
    s = [lax.dot_general(q_ref[0, :, j * HEAD_DIM:(j + 1) * HEAD_DIM], k, _NT,
                         preferred_element_type=F32).astype(BF16) for j in heads]
    p, alpha = [], []
    for j in heads:
        m_prev = m_sc[j]
        m_new = jnp.maximum(m_prev, jnp.max(s[j], axis=-1, keepdims=True).astype(F32))
        alpha.append(jnp.exp2(m_prev - m_new))
        p.append(jnp.exp2(s[j] - m_new.astype(BF16)))
        m_sc[j] = m_new
    for j in heads:
        acc_sc[j] = alpha[j] * acc_sc[j] + _dot(p[j], v_ext)

    @pl.when(ki == pl.num_programs(3) - 1)
    def _():
        for j in heads:
            acc = acc_sc[j]
            o_ref[0, :, j * HEAD_DIM:(j + 1) * HEAD_DIM] = (acc[:, :HEAD_DIM] / acc[:, HEAD_DIM:]).astype(BF16)


def _attention(q, k, v, *, tq, tk):
    b, s, aw = q.shape
    group = aw // HEAD_DIM // N_KV_HEADS
    gw = group * HEAD_DIM
    return pl.pallas_call(
        functools.partial(_attn_kernel, group=group),
        grid=(b, N_KV_HEADS, s // tq, s // tk),
        in_specs=[pl.BlockSpec((1, tq, gw), lambda bi, h, qi, ki: (bi, qi, h)),
                  pl.BlockSpec((1, tk, HEAD_DIM), lambda bi, h, qi, ki: (bi, ki, h)),
                  pl.BlockSpec((1, tk, HEAD_DIM), lambda bi, h, qi, ki: (bi, ki, h))],
        out_specs=pl.BlockSpec((1, tq, gw), lambda bi, h, qi, ki: (bi, qi, h)),
        out_shape=jax.ShapeDtypeStruct((b, s, aw), BF16),
        scratch_shapes=[pltpu.VMEM((group, tq, 1), F32), pltpu.VMEM((group, tq, 2 * HEAD_DIM), F32)],
        compiler_params=_params("parallel", "parallel", "parallel", "arbitrary"),
        name="attention",
    )(q, k, v)


def _fourier1_kernel(f_ref, u_ref, y_ref, *, n1):
    y = _dot(f_ref[...], u_ref[0])
    y_ref[0, 0] = y[:n1].astype(BF16)
    y_ref[0, 1] = y[n1:].astype(BF16)


def _fourier2_kernel(m_ref, y_ref, cc_ref, sc_ref, o_ref, *, n2, groups):
    y = y_ref[0].reshape(2 * n2, groups * FNET_GROUP_DIM)
    g = _dot(m_ref[0], y)
    gr = g[:n2].astype(BF16)
    gi = g[n2:].astype(BF16)
    for c in range(groups):
        sl = slice(c * FNET_GROUP_DIM, (c + 1) * FNET_GROUP_DIM)
        o_ref[0, :, sl] = (_dot(gr[:, sl], cc_ref[...]) + _dot(gi[:, sl], sc_ref[...])).astype(BF16)


def _fourier_tables(seq):
    n2 = 128
    n1 = seq // n2
    a1 = 2.0 * np.pi * np.outer(np.arange(n1), np.arange(n1)) / n1
    f1 = np.concatenate([np.cos(a1), -np.sin(a1)], axis=0)
    kk = np.arange(n1)[:, None, None] + n1 * np.arange(n2)[None, :, None]
    th = 2.0 * np.pi * (kk * np.arange(n2)[None, None, :] % seq) / seq
    mc, ms = np.cos(th), np.sin(th)
    m = np.concatenate([np.concatenate([mc, ms], axis=2),
                        np.concatenate([-ms, mc], axis=2)], axis=1)
    ac = 2.0 * np.pi * np.outer(np.arange(FNET_GROUP_DIM), np.arange(FNET_GROUP_DIM)) / FNET_GROUP_DIM
    norm = 1.0 / math.sqrt(seq * FNET_GROUP_DIM)
    return (jnp.asarray(f1, BF16), jnp.asarray(m, BF16),
            jnp.asarray(np.cos(ac) * norm, BF16), jnp.asarray(np.sin(ac) * norm, BF16))


def _fourier_mix(u, *, batch, seq, tn):
    t, width = u.shape
    n2 = 128
    n1 = seq // n2
    groups = width // FNET_GROUP_DIM
    f1, m, cc, sc = _fourier_tables(seq)
    cols = n2 * width
    y = pl.pallas_call(
        functools.partial(_fourier1_kernel, n1=n1),
        grid=(batch, cols // tn),
        in_specs=[_resident((2 * n1, n1)),
                  pl.BlockSpec((1, n1, tn), lambda b, j: (b, 0, j))],
        out_specs=pl.BlockSpec((1, 2, n1, tn), lambda b, j: (b, 0, 0, j)),
        out_shape=jax.ShapeDtypeStruct((batch, 2, n1, cols), BF16),
        compiler_params=_params("parallel", "parallel"),
        name="fourier_seq1",
    )(f1, u.reshape(batch, n1, cols))
    y = y.reshape(batch, 2, seq, width)
    out = pl.pallas_call(
        functools.partial(_fourier2_kernel, n2=n2, groups=groups),
        grid=(n1, batch),
        in_specs=[pl.BlockSpec((1, 2 * n2, 2 * n2), lambda k1, b: (k1, 0, 0)),
                  pl.BlockSpec((1, 2, n2, width), lambda k1, b: (b, 0, k1, 0)),
                  _resident((FNET_GROUP_DIM, FNET_GROUP_DIM)),
                  _resident((FNET_GROUP_DIM, FNET_GROUP_DIM))],
        out_specs=pl.BlockSpec((1, n2, width), lambda k1, b: (b, 0, k1)),
        out_shape=jax.ShapeDtypeStruct((batch, n2, n1 * width), BF16),
        compiler_params=_params("parallel", "parallel"),
        name="fourier_seq2",
    )(m, y, cc, sc)
    return out.reshape(t, width)


def _proj_res_kernel(f_ref, a_ref, wf_ref, wa_ref, x_ref, gp_ref, gn_ref, xo_ref, ho_ref):
    half = x_ref.shape[0] // 2
    halves = [slice(0, half), slice(half, 2 * half)]
    ms = [_dot(f_ref[r, :], wf_ref[...]) + _dot(a_ref[r, :], wa_ref[...]) for r in halves]
    for r, m in zip(halves, ms):
        xn = x_ref[r, :] + _rms(m, gp_ref[...])
        xo_ref[r, :] = xn
        ho_ref[r, :] = _rms(xn, gn_ref[...]).astype(BF16)


def _proj_res(f, a, wf, wa, x, g_post, g_next, *, tm):
    t, d = x.shape
    row = pl.BlockSpec((tm, d), lambda i: (i, 0))
    lhs = lambda arr: pl.BlockSpec((tm, arr.shape[1]), lambda i: (i, 0))
    return pl.pallas_call(
        _proj_res_kernel,
        grid=(t // tm,),
        in_specs=[lhs(f), lhs(a), _resident(wf.shape), _resident(wa.shape), row,
                  _resident((1, d)), _resident((1, d))],
        out_specs=[row, row],
        out_shape=[jax.ShapeDtypeStruct((t, d), F32), jax.ShapeDtypeStruct((t, d), BF16)],
        compiler_params=_params("parallel"),
        name="proj_res",
    )(f, a, wf, wa, x, g_post, g_next)


def _gelu_tanh(x):
    return 0.5 * x * (1.0 + jnp.tanh(math.sqrt(2.0 / math.pi) * (x + 0.044715 * x * x * x)))


def _silu(x):
    return x * (1.0 / (1.0 + jnp.exp(-x)))


def _assemble_halo(hext, h_ref, hp_ref, hn_ref, *, tm, tps):
    i = pl.program_id(0)
    first = (i % tps) == 0
    last = (i % tps) == tps - 1
    prev_blk = jnp.where(first, jnp.zeros_like(hp_ref[...]), hp_ref[...])
    next_blk = jnp.where(last, jnp.zeros_like(hn_ref[...]), hn_ref[...])
    row = lax.broadcasted_iota(jnp.int32, prev_blk.shape, 0)
    hext[0:tm] = h_ref[...]
    hext[tm:] = jnp.where(row < HALO // 2, next_blk, prev_blk)


def _conv_rows(u, cw, cb, *, tm):
    rows = tm + HALO
    c = cb + cw[0:1] * pltpu.roll(u, 1, 0) + cw[1:2] * u + cw[2:3] * pltpu.roll(u, rows - 1, 0)
    return c[0:tm]


def _halo_specs(t, tm, kdim):
    hb = tm // HALO
    last_blk = t // HALO - 1
    return [pl.BlockSpec((tm, kdim), lambda i, j: (i, 0)),
            pl.BlockSpec((HALO, kdim), lambda i, j: (jnp.maximum(i * hb - 1, 0), 0)),
            pl.BlockSpec((HALO, kdim), lambda i, j: (jnp.minimum((i + 1) * hb, last_blk), 0))]


def _inproj1_kernel(h_ref, hp_ref, hn_ref, w_ref, cw_ref, cb_ref, z_ref, xbc_ref, hext, *, tm, tps, sub):
    xw = xbc_ref.shape[1]
    zw = z_ref.shape[1]

    @pl.when(pl.program_id(1) == 0)
    def _():
        _assemble_halo(hext, h_ref, hp_ref, hn_ref, tm=tm, tps=tps)

    hx = hext[...]
    h = hext[0:tm, :]
    ux = [_dot(hx, w_ref[:, c0:c0 + sub]) for c0 in range(0, xw, sub)]
    uz = [_dot(h, w_ref[:, xw + c0:xw + c0 + sub]) for c0 in range(0, zw, sub)]
    for k, u in enumerate(ux):
        cols = slice(k * sub, (k + 1) * sub)
        y = _conv_rows(u, cw_ref[:, cols], cb_ref[:, cols], tm=tm)
        xbc_ref[:, cols] = _silu(y).astype(BF16)
    for k, u in enumerate(uz):
        z_ref[:, k * sub:(k + 1) * sub] = u.astype(BF16)


def _inproj1(h, w_steps, conv_w, conv_b, *, seq, tm, steps, sub, d_inner, conv_dim):
    t, kdim = h.shape
    xw, zw = conv_dim // steps, d_inner // steps
    in_specs = _halo_specs(t, tm, kdim) + [
        pl.BlockSpec((kdim, xw + zw), lambda i, j: (0, j)),
        pl.BlockSpec((3, xw), lambda i, j: (0, j)),
        pl.BlockSpec((1, xw), lambda i, j: (0, j))]
    return pl.pallas_call(
        functools.partial(_inproj1_kernel, tm=tm, tps=seq // tm, sub=sub),
        grid=(t // tm, steps),
        in_specs=in_specs,
        out_specs=[pl.BlockSpec((tm, zw), lambda i, j: (i, j)), pl.BlockSpec((tm, xw), lambda i, j: (i, j))],
        out_shape=[jax.ShapeDtypeStruct((t, d_inner), BF16), jax.ShapeDtypeStruct((t, conv_dim), BF16)],
        scratch_shapes=[pltpu.VMEM((tm + HALO, kdim), BF16)],
        compiler_params=_params("parallel", "arbitrary"),
        name="inproj1",
    )(h, h, h, w_steps, conv_w, conv_b)


def _ffn_kernel(*refs, tm, tps, nj, has_next):
    (h_ref, hp_ref, hn_ref, wg_ref, wv_ref, cwg_ref, cwv_ref, cbg_ref, cbv_ref,
     wd_ref, wdl_ref, x_ref, gp_ref) = refs[:13]
    pos = 13
    gn_ref = refs[pos] if has_next else None
    pos += int(has_next)
    xo_ref = refs[pos]
    ho_ref = refs[pos + 1] if has_next else None
    hext, act_a, act_b, acc = refs[-4:]
    j = pl.program_id(1)

    def step(act_prev, act_cur):
        hx = hext[...]
        ug = _dot(hx, wg_ref[...])
        uv = _dot(hx, wv_ref[...])
        if act_prev is None:
            acc[...] = jnp.zeros(acc.shape, F32)
        else:
            acc[...] += _dot(act_prev[...], wd_ref[...])
        gate = _conv_rows(ug, cwg_ref[...], cbg_ref[...], tm=tm)
        val = _conv_rows(uv, cwv_ref[...], cbv_ref[...], tm=tm)
        act_cur[...] = (_gelu_tanh(gate) * val).astype(BF16)

    @pl.when(j == 0)
    def _():
        _assemble_halo(hext, h_ref, hp_ref, hn_ref, tm=tm, tps=tps)
        step(None, act_a)

    @pl.when((j > 0) & (j % 2 == 0))
    def _():
        step(act_b, act_a)

    @pl.when(j % 2 == 1)
    def _():
        step(act_a, act_b)

    @pl.when(j == nj - 1)
    def _():
        act_last = act_a if (nj - 1) % 2 == 0 else act_b
        m = acc[...] + _dot(act_last[...], wdl_ref[...])
        xn = x_ref[...] + _rms(m, gp_ref[...])
        xo_ref[...] = xn
        if has_next:
            ho_ref[...] = _rms(xn, gn_ref[...]).astype(BF16)


def _ffn(h, w_up, conv_w, conv_b, w_down, x, g_post, g_next, *, seq, tm, tn):
    t, d = x.shape
    d_ff = w_down.shape[0]
    nj = d_ff // tn
    has_next = g_next is not None
    row = pl.BlockSpec((tm, d), lambda i, j: (i, 0))
    in_specs = _halo_specs(t, tm, d) + [
        pl.BlockSpec((d, tn), lambda i, j: (0, j)),
        pl.BlockSpec((d, tn), lambda i, j: (0, nj + j)),
        pl.BlockSpec((3, tn), lambda i, j: (0, j)),
        pl.BlockSpec((3, tn), lambda i, j: (0, nj + j)),
        pl.BlockSpec((1, tn), lambda i, j: (0, j)),
        pl.BlockSpec((1, tn), lambda i, j: (0, nj + j)),
        pl.BlockSpec((tn, d), lambda i, j: (jnp.maximum(j - 1, 0), 0)),
        pl.BlockSpec((tn, d), lambda i, j: (nj - 1, 0), pipeline_mode=pl.Buffered(1)),
        row, _resident((1, d))]
    args = [h, h, h, w_up, w_up, conv_w, conv_w, conv_b, conv_b, w_down, w_down, x, g_post]
    out_specs = [row]
    out_shape = [jax.ShapeDtypeStruct((t, d), F32)]
    if has_next:
        in_specs.append(_resident((1, d)))
        args.append(g_next)
        out_specs.append(row)
        out_shape.append(jax.ShapeDtypeStruct((t, d), BF16))
    res = pl.pallas_call(
        functools.partial(_ffn_kernel, tm=tm, tps=seq // tm, nj=nj, has_next=has_next),
        grid=(t // tm, nj),
        in_specs=in_specs, out_specs=out_specs, out_shape=out_shape,
        scratch_shapes=[pltpu.VMEM((tm + HALO, d), BF16), pltpu.VMEM((tm, tn), BF16),
                        pltpu.VMEM((tm, tn), BF16), pltpu.VMEM((tm, d), F32)],
        compiler_params=_params("parallel", "arbitrary"),
        name="ffn",
    )(*args)
    return res if has_next else (res[0], None)


def _dt_kernel(h_ref, w_ref, b_ref, o_ref, *, per_group):
    r = _dot(h_ref[...], w_ref[...]) + b_ref[...]
    sp = jnp.maximum(r, 0.0) + jnp.log(1.0 + jnp.exp(-jnp.abs(r)))
    for g in range(o_ref.shape[1] // LANES):
        o_ref[:, g * LANES:(g + 1) * LANES] = sp if g == 0 else pltpu.roll(sp, LANES - g * per_group, 1)


def _dt_proj(h, w, b, *, tm, groups, per_group):
    t, kdim = h.shape
    assert w.shape[1] == LANES == groups * per_group
    return pl.pallas_call(
        functools.partial(_dt_kernel, per_group=per_group),
        grid=(t // tm,),
        in_specs=[pl.BlockSpec((tm, kdim), lambda i: (i, 0)), _resident((kdim, LANES)), _resident((1, LANES))],
        out_specs=pl.BlockSpec((tm, groups * LANES), lambda i: (i, 0)),
        out_shape=jax.ShapeDtypeStruct((t, groups * LANES), F32),
        compiler_params=_params("parallel"),
        name="ssd_dt",
    )(h, w, b)


def _split3(x):
    hi = x.astype(BF16)
    r = x - hi.astype(F32)
    mid = r.astype(BF16)
    lo = (r - mid.astype(F32)).astype(BF16)
    return hi, mid, lo


def _cumsum_rows(tri, x):
    hi, mid, lo = _split3(x)
    return _dot(tri, hi) + _dot(tri, mid) + _dot(tri, lo)


def _expand(x, e):
    return _dot(x.astype(BF16), e)


class _Chunk:
    pass


def _ssd_positions(c, x_ref, b_ref, c_ref, dt_ref, a_ref, tri_ref):
    rows = slice(c.r0, c.r0 + SSM_CHUNK)
    c.x = x_ref[rows, :]
    c.bm = b_ref[rows, :]
    c.cm = c_ref[rows, :]
    c.dt = dt_ref[rows, :]
    c.dta = c.dt * a_ref[...]
    c.incl = _cumsum_rows(tri_ref[...], c.dta)


def _ssd_weights(c, e_ref):
    L = SSM_CHUNK
    total = c.incl[L - 1:L, :]
    if c.backward:
        c.pos = c.incl - c.dta
        w_state = c.dt * jnp.exp(c.pos)
        w_out = jnp.exp(total - c.pos)
    else:
        c.pos = c.incl
        w_state = c.dt * jnp.exp(total - c.pos)
        w_out = jnp.exp(c.pos)
    e = e_ref[...]
    c.w_out_x = _expand(w_out, e)
    c.carry = c.w_out_x[0:1] if c.backward else c.w_out_x[L - 1:L]
    c.xs = (c.x.astype(F32) * _expand(w_state, e)).astype(BF16)
    c.b_t = c.bm.astype(F32).T.astype(BF16)
    c.cb = lax.dot_general(c.cm, c.bm, _NT, preferred_element_type=F32)
    c.pos_t = c.pos.T
    c.dt_t = c.dt.T


def _ssd_state(c, h_in):
    c.y = _dot(c.cm, h_in.astype(BF16)) * c.w_out_x
    return h_in * c.carry + _dot(c.b_t, c.xs)


def _ssd_diag(c, y_ref, *, hpg):
    L = SSM_CHUNK
    rows = slice(c.r0, c.r0 + L)
    li = lax.broadcasted_iota(jnp.int32, (L, L), 0)
    si = lax.broadcasted_iota(jnp.int32, (L, L), 1)
    mask = (si >= li) if c.backward else (li >= si)
    lane = lax.broadcasted_iota(jnp.int32, (L, 2 * SSM_HEAD_DIM), 1)
    for jp in range(hpg // 2):
        sl = slice(jp * 2 * SSM_HEAD_DIM, (jp + 1) * 2 * SSM_HEAD_DIM)
        xp = c.x[:, sl]
        outs = []
        for j in (c.lane0 + 2 * jp, c.lane0 + 2 * jp + 1):
            col = c.pos[:, j:j + 1]
            row = c.pos_t[j:j + 1, :]
            d = (row - col) if c.backward else (col - row)
            w = c.cb * jnp.exp(jnp.where(mask, d, -1e30)) * c.dt_t[j:j + 1, :]
            outs.append(_dot(w.astype(BF16), xp))
        y_ref[rows, sl] = (c.y[:, sl] + jnp.where(lane < SSM_HEAD_DIM, outs[0], outs[1])).astype(y_ref.dtype)


def _ssd_kernel(xf, bf, cf, dtf, xb, bb, cb, dtb, a_ref, tri_ref, ef_ref, eb_ref,
                yf_ref, yb_ref, hf_sc, hb_sc, *, hpg, cps):
    @pl.when(pl.program_id(2) == 0)
    def _():
        hf_sc[...] = jnp.zeros(hf_sc.shape, F32)
        hb_sc[...] = jnp.zeros(hb_sc.shape, F32)

    fwd, bwd = [], []
    for k in range(cps):
        f, b = _Chunk(), _Chunk()
        f.r0, f.lane0, f.backward = k * SSM_CHUNK, 0, False
        b.r0, b.lane0, b.backward = (cps - 1 - k) * SSM_CHUNK, hpg, True
        fwd.append(f)
        bwd.append(b)
    for f, b in zip(fwd, bwd):
        _ssd_positions(f, xf, bf, cf, dtf, a_ref, tri_ref)
        _ssd_positions(b, xb, bb, cb, dtb, a_ref, tri_ref)
    for f, b in zip(fwd, bwd):
        _ssd_weights(f, ef_ref)
        _ssd_weights(b, eb_ref)
    hf = hf_sc[...]
    hb = hb_sc[...]
    for f, b in zip(fwd, bwd):
        hf = _ssd_state(f, hf)
        hb = _ssd_state(b, hb)
    hf_sc[...] = hf
    hb_sc[...] = hb
    for f, b in zip(fwd, bwd):
        _ssd_diag(f, yf_ref, hpg=hpg)
        _ssd_diag(b, yb_ref, hpg=hpg)


def _ssd(xbc, dt, a_pad, *, batch, seq, d_inner, cps):
    t = xbc.shape[0]
    L = SSM_CHUNK
    rows = cps * L
    nc = seq // rows
    gw = d_inner // SSM_N_GROUPS
    hpg = gw // SSM_HEAD_DIM
    b0 = d_inner // SSM_D_STATE
    c0 = b0 + SSM_N_GROUPS
    tri = jnp.asarray(np.tril(np.ones((L, L))), BF16)
    head_of_lane = np.arange(gw) // SSM_HEAD_DIM
    ef = jnp.asarray(np.arange(LANES)[:, None] == head_of_lane[None, :], BF16)
    eb = jnp.asarray(np.arange(LANES)[:, None] == (head_of_lane[None, :] + hpg), BF16)

    def fwd(col):
        return lambda b, g, c: (b * nc + c, col(g))

    def bwd(col):
        return lambda b, g, c: (b * nc + (nc - 1 - c), col(g))

    def specs(order):
        return [pl.BlockSpec((rows, gw), order(lambda g: g)),
                pl.BlockSpec((rows, SSM_D_STATE), order(lambda g: b0 + g)),
                pl.BlockSpec((rows, SSM_D_STATE), order(lambda g: c0 + g)),
                pl.BlockSpec((rows, LANES), order(lambda g: g))]

    in_specs = specs(fwd) + specs(bwd) + [
        pl.BlockSpec((1, LANES), lambda b, g, c: (0, g)),
        _resident((L, L)), _resident((LANES, gw)), _resident((LANES, gw))]
    return pl.pallas_call(
        functools.partial(_ssd_kernel, hpg=hpg, cps=cps),
        grid=(batch, SSM_N_GROUPS, nc),
        in_specs=in_specs,
        out_specs=[pl.BlockSpec((rows, gw), fwd(lambda g: g)), pl.BlockSpec((rows, gw), bwd(lambda g: g))],
        out_shape=[jax.ShapeDtypeStruct((t, d_inner), BF16), jax.ShapeDtypeStruct((t, d_inner), BF16)],
        scratch_shapes=[pltpu.VMEM((SSM_D_STATE, gw), F32), pltpu.VMEM((SSM_D_STATE, gw), F32)],
        compiler_params=_params("parallel", "parallel", "arbitrary"),
        name="ssd_scan",
    )(xbc, xbc, xbc, dt, xbc, xbc, xbc, dt, a_pad, tri, ef, eb)


def _ssd_out_kernel(yf_ref, yb_ref, xs_ref, z_ref, dsk_ref, ng_ref, w_ref, x_ref, gp_ref, gn_ref,
                    xo_ref, ho_ref, acc, ss, *, d_inner):
    kk = pl.program_id(1)

    @pl.when(kk == 0)
    def _():
        acc[...] = jnp.zeros(acc.shape, F32)
        ss[...] = jnp.zeros(ss.shape, F32)

    tk = w_ref.shape[0]
    sq = None
    part = None
    for c0 in range(0, tk, OUT_SUB):
        cols = slice(c0, c0 + OUT_SUB)
        y = (yf_ref[:, cols].astype(F32) + yb_ref[:, cols].astype(F32)
             + xs_ref[:, cols].astype(F32) * dsk_ref[:, cols])
        y = y * _silu(z_ref[:, cols].astype(F32))
        y2 = y * y
        for l0 in range(0, OUT_SUB, LANES):
            sq = y2[:, l0:l0 + LANES] if sq is None else sq + y2[:, l0:l0 + LANES]
        d = _dot((y * ng_ref[:, cols]).astype(BF16), w_ref[cols, :])
        part = d if part is None else part + d
    acc[...] += part
    ss[...] += sq

    @pl.when(kk == pl.num_programs(1) - 1)
    def _():
        ssum = jnp.sum(ss[...], axis=-1, keepdims=True)
        m = acc[...] * lax.rsqrt(ssum * (1.0 / d_inner) + EPS)
        xn = x_ref[...] + _rms(m, gp_ref[...])
        xo_ref[...] = xn
        ho_ref[...] = _rms(xn, gn_ref[...]).astype(BF16)


def _ssd_out(yf, yb, xbc, z, d_exp, norm_g, w, x, g_post, g_next, *, tm, tk):
    t, d = x.shape
    d_inner = yf.shape[1]
    lhs = pl.BlockSpec((tm, tk), lambda i, k: (i, k))
    vec = pl.BlockSpec((1, tk), lambda i, k: (0, k))
    row = pl.BlockSpec((tm, d), lambda i, k: (i, 0))
    w_spec = _resident((tk, d)) if tk == d_inner else pl.BlockSpec((tk, d), lambda i, k: (k, 0))
    return pl.pallas_call(
        functools.partial(_ssd_out_kernel, d_inner=d_inner),
        grid=(t // tm, d_inner // tk),
        in_specs=[lhs, lhs, lhs, lhs, vec, vec, w_spec, row, _resident((1, d)), _resident((1, d))],
        out_specs=[row, row],
        out_shape=[jax.ShapeDtypeStruct((t, d), F32), jax.ShapeDtypeStruct((t, d), BF16)],
        scratch_shapes=[pltpu.VMEM((tm, d), F32), pltpu.VMEM((tm, LANES), F32)],
        compiler_params=_params("parallel", "arbitrary"),
        name="ssd_out",
    )(yf, yb, xbc, z, d_exp, norm_g, w, x, g_post, g_next)


def _rope_tables(seq):
    rows = seq // GRID_W
    row = jnp.repeat(jnp.arange(rows, dtype=F32), GRID_W)
    col = jnp.tile(jnp.arange(GRID_W, dtype=F32), rows)
    axis_dim = HEAD_DIM // 2
    inv_freq = ROPE_THETA ** (-jnp.arange(0, axis_dim, 2, dtype=F32) / axis_dim)
    ang = jnp.concatenate([row[:, None] * inv_freq, col[:, None] * inv_freq], axis=-1)
    cos, sin = jnp.cos(ang), jnp.sin(ang)
    return jnp.concatenate([cos, cos], axis=-1), jnp.concatenate([-sin, sin], axis=-1)


def _pick(n, pref):
    t = min(pref, n)
    while n % t:
        t -= LANES
    return t


def kernel(x_prompt, x_sample, mix_pre_g, mix_post_g, ffn_pre_g, ffn_post_g, fa_w_in, fa_q_gain, fa_k_gain, fa_w_out, ssd_w_in, ssd_conv_w, ssd_conv_b, ssd_dt_bias, ssd_a_log, ssd_d, ssd_norm_g, ssd_w_out, ffn_w_up, ffn_conv_w, ffn_conv_b, ffn_w_down):
    depth = mix_pre_g.shape[0]
    d = x_prompt.shape[2]
    aw = N_Q_HEADS * HEAD_DIM
    kvw = N_KV_HEADS * HEAD_DIM
    fw = fa_w_in.shape[2] - aw - 2 * kvw
    assert fa_w_out.shape[1] == fw + aw and fw % FNET_GROUP_DIM == 0
    d_ff = ffn_w_down.shape[1]
    d_inner = ssd_w_out.shape[1]
    n_heads = d_inner // SSM_HEAD_DIM
    hpg = n_heads // SSM_N_GROUPS
    bc_w = SSM_N_GROUPS * SSM_D_STATE
    conv_dim = d_inner + 2 * bc_w
    assert hpg % 2 == 0 and 2 * hpg <= LANES and depth % 2 == 0
    tm = TOKEN_TILE
    vec = lambda v: v.reshape(1, -1).astype(F32)

    def regroup(p, pad):
        p = p.astype(F32).reshape(2, SSM_N_GROUPS, hpg).transpose(1, 0, 2).reshape(SSM_N_GROUPS, 2 * hpg)
        if pad:
            p = jnp.pad(p, ((0, 0), (0, LANES - 2 * hpg)))
        return p.reshape(1, -1)

    layers = []
    for i in range(depth):
        j = i // 2
        lw = dict(ffn_pre=vec(ffn_pre_g[i]), ffn_post=vec(ffn_post_g[i]), mix_pre=vec(mix_pre_g[i]),
                  mix_post=vec(mix_post_g[i]), w_up=ffn_w_up[i].astype(BF16), w_down=ffn_w_down[i].astype(BF16),
                  ffn_cw=ffn_conv_w[i].astype(F32), ffn_cb=vec(ffn_conv_b[i]))
        if i % 2 == 0:
            wo = fa_w_out[j].astype(BF16)
            lw.update(w_in=fa_w_in[j].astype(BF16), qg=vec(fa_q_gain[j]), kg=vec(fa_k_gain[j]),
                      wo_f=wo[:fw], wo_a=wo[fw:])
        else:
            w_dt = ssd_w_in[j][:, d_inner + conv_dim:].reshape(d, 2, SSM_N_GROUPS, hpg)
            w_dt = w_dt.transpose(0, 2, 1, 3).reshape(d, 2 * n_heads)
            w_bf = ssd_w_in[j].astype(BF16)
            xw, zw = conv_dim // IN1_STEPS, d_inner // IN1_STEPS
            w_steps = jnp.concatenate(
                [w_bf[:, d_inner:d_inner + conv_dim].reshape(d, IN1_STEPS, xw),
                 w_bf[:, :d_inner].reshape(d, IN1_STEPS, zw)], axis=2).reshape(d, conv_dim + d_inner)
            lw.update(w_in=w_steps, w_dt=w_dt.astype(BF16), dt_bias=regroup(ssd_dt_bias[j], pad=False),
                      a_pad=regroup(-jnp.exp(ssd_a_log[j].astype(F32)), pad=True), cw=ssd_conv_w[j].astype(F32),
                      cb=vec(ssd_conv_b[j]), d_exp=jnp.repeat(ssd_d[j].astype(F32), SSM_HEAD_DIM).reshape(1, d_inner),
                      norm_g=vec(ssd_norm_g[j]), w_out=ssd_w_out[j].astype(BF16))
        layers.append(lw)

    def trunk(xin):
        batch, seq, _ = xin.shape
        assert seq % tm == 0
        t = batch * seq
        x = xin.reshape(t, d)
        cosf, sinf = _rope_tables(seq)
        h = None
        for i, lw in enumerate(layers):
            if i % 2 == 0:
                u, q, k, v = _inproj0(x, lw["mix_pre"], lw["w_in"], lw["qg"], lw["kg"], cosf, sinf,
                                      seq=seq, fw=fw, aw=aw, kvw=kvw, tm=tm)
                a = _attention(q.reshape(batch, seq, aw), k.reshape(batch, seq, kvw), v.reshape(batch, seq, kvw),
                               tq=_pick(seq, ATTN_TILE), tk=_pick(seq, ATTN_TILE)).reshape(t, aw)
                f = _fourier_mix(u, batch=batch, seq=seq, tn=_pick(128 * fw, FOURIER_LANES))
                x, h = _proj_res(f, a, lw["wo_f"], lw["wo_a"], x, lw["mix_post"], lw["ffn_pre"], tm=tm)
            else:
                z, xbc = _inproj1(h, lw["w_in"], lw["cw"], lw["cb"], seq=seq, tm=tm,
                                  steps=IN1_STEPS, sub=IN1_SUB, d_inner=d_inner, conv_dim=conv_dim)
                dt = _dt_proj(h, lw["w_dt"], lw["dt_bias"], tm=tm, groups=SSM_N_GROUPS, per_group=2 * hpg)
                yf, yb = _ssd(xbc, dt, lw["a_pad"], batch=batch, seq=seq, d_inner=d_inner,
                              cps=SSD_CHUNKS_PER_STEP)
                x, h = _ssd_out(yf, yb, xbc, z, lw["d_exp"], lw["norm_g"], lw["w_out"], x,
                                lw["mix_post"], lw["ffn_pre"], tm=SSD_OUT_TILE, tk=d_inner)
            g_next = layers[i + 1]["mix_pre"] if i + 1 < depth else None
            x, h = _ffn(h, lw["w_up"], lw["ffn_cw"], lw["ffn_cb"], lw["w_down"], x, lw["ffn_post"], g_next,
                        seq=seq, tm=tm, tn=_pick(d_ff, FFN_CHUNK))
        return x.reshape(batch, seq, d)

    return (trunk(x_prompt), trunk(x_sample))
```

```python
import functools
import math

import jax
import jax.numpy as jnp
import numpy as np
from jax import lax
from jax.experimental import pallas as pl
from jax.experimental.pallas import tpu as pltpu

F32 = jnp.float32
BF16 = jnp.bfloat16

EPS = 1e-6
GRID_W = 64
HEAD_DIM = 128
N_Q_HEADS = 8
N_KV_HEADS = 2
FNET_GROUP_DIM = 128
ROPE_THETA = 10000.0
SSM_HEAD_DIM = 64
SSM_N_GROUPS = 8
SSM_D_STATE = 128
SSM_CHUNK = 128
LANES = 128
HALO = 16
OUT_SUB = 256
IN1_STEPS = 4
IN1_SUB = 512
TOKEN_TILE = 512
SSD_OUT_TILE = 256
ATTN_TILE = 1024
FFN_CHUNK = 512
SSD_CHUNKS_PER_STEP = 8
VMEM_LIMIT = 56 * 1024 * 1024

_NT = (((1,), (1,)), ((), ()))


def _params(*sem):
    return pltpu.CompilerParams(dimension_semantics=sem, vmem_limit_bytes=VMEM_LIMIT)


def _resident(shape):
    nd = len(shape)
    return pl.BlockSpec(shape, lambda *_: (0,) * nd, pipeline_mode=pl.Buffered(1))


def _rms(x, g):
    return x * lax.rsqrt(jnp.mean(x * x, axis=-1, keepdims=True) + EPS) * g


def _dot(a, b):
    return jnp.dot(a, b, preferred_element_type=F32)


def _inproj0_kernel(x_ref, g_ref, w_ref, qg_ref, kg_ref, cos_ref, sin_ref,
                    u_ref, q_ref, k_ref, v_ref, *, fw, aw, kvw, scale):
    h = _rms(x_ref[...], g_ref[...]).astype(BF16)
    cosf = cos_ref[...]
    sinf = sin_ref[...]

    def norm_rope(t, gain, mult):
        t = _rms(t, gain)
        t = t * cosf + pltpu.roll(t, HEAD_DIM // 2, 1) * sinf
        return (t * mult).astype(BF16)

    q = _dot(h, w_ref[:, fw:fw + aw])
    k = _dot(h, w_ref[:, fw + aw:fw + aw + kvw])
    v = _dot(h, w_ref[:, fw + aw + kvw:])
    u = _dot(h, w_ref[:, :fw])
    for hh in range(aw // HEAD_DIM):
        sl = slice(hh * HEAD_DIM, (hh + 1) * HEAD_DIM)
        q_ref[:, sl] = norm_rope(q[:, sl], qg_ref[...], scale)
    for hh in range(kvw // HEAD_DIM):
        sl = slice(hh * HEAD_DIM, (hh + 1) * HEAD_DIM)
        k_ref[:, sl] = norm_rope(k[:, sl], kg_ref[...], 1.0)
    v_ref[...] = v.astype(BF16)
    u_ref[...] = u.astype(BF16)


def _inproj0(x, g, w, qg, kg, cosf, sinf, *, seq, fw, aw, kvw, tm):
    t, d = x.shape
    n = w.shape[1]
    tps = seq // tm
    kern = functools.partial(_inproj0_kernel, fw=fw, aw=aw, kvw=kvw, scale=math.log2(math.e) * HEAD_DIM ** -0.5)
    row = lambda width: pl.BlockSpec((tm, width), lambda i: (i, 0))
    return pl.pallas_call(
        kern,
        grid=(t // tm,),
        in_specs=[row(d), _resident((1, d)), _resident((d, n)),
                  _resident((1, HEAD_DIM)), _resident((1, HEAD_DIM)),
                  pl.BlockSpec((tm, HEAD_DIM), lambda i: (i % tps, 0)),
                  pl.BlockSpec((tm, HEAD_DIM), lambda i: (i % tps, 0))],
        out_specs=[row(fw), row(aw), row(kvw), row(kvw)],
        out_shape=[jax.ShapeDtypeStruct((t, fw), BF16), jax.ShapeDtypeStruct((t, aw), BF16),
                   jax.ShapeDtypeStruct((t, kvw), BF16), jax.ShapeDtypeStruct((t, kvw), BF16)],
        compiler_params=_params("parallel"),
        name="inproj0",
    )(x, g, w, qg, kg, cosf, sinf)


def _attn_kernel(q_ref, k_ref, v_ref, o_ref, m_sc, acc_sc, *, group):
    ki = pl.program_id(3)

    @pl.when(ki == 0)
    def _():
        m_sc[...] = jnp.full(m_sc.shape, -jnp.inf, F32)
        acc_sc[...] = jnp.zeros(acc_sc.shape, F32)

    k = k_ref[0]
    v = v_ref[0]
    v_ext = jnp.concatenate([v, jnp.ones_like(v)], axis=1)
    heads = range(group)
    s = [lax.dot_general(q_ref[0, :, j * HEAD_DIM:(j + 1) * HEAD_DIM], k, _NT,
                         preferred_element_type=F32).astype(BF16) for j in heads]
    p, alpha = [], []
    for j in heads:
        m_prev = m_sc[j]
        m_new = jnp.maximum(m_prev, jnp.max(s[j], axis=-1, keepdims=True).astype(F32))
        alpha.append(jnp.exp2(m_prev - m_new))
        p.append(jnp.exp2(s[j] - m_new.astype(BF16)))
        m_sc[j] = m_new
    for j in heads:
        acc_sc[j] = alpha[j] * acc_sc[j] + _dot(p[j], v_ext)

    @pl.when(ki == pl.num_programs(3) - 1)
    def _():
        for j in heads:
            acc = acc_sc[j]
            o_ref[0, :, j * HEAD_DIM:(j + 1) * HEAD_DIM] = (acc[:, :HEAD_DIM] / acc[:, HEAD_DIM:]).astype(BF16)


def _attention(q, k, v, *, tq, tk):
    b, s, aw = q.shape
    group = aw // HEAD_DIM // N_KV_HEADS
    gw = group * HEAD_DIM
    return pl.pallas_call(
        functools.partial(_attn_kernel, group=group),
        grid=(b, N_KV_HEADS, s // tq, s // tk),
        in_specs=[pl.BlockSpec((1, tq, gw), lambda bi, h, qi, ki: (bi, qi, h)),
                  pl.BlockSpec((1, tk, HEAD_DIM), lambda bi, h, qi, ki: (bi, ki, h)),
                  pl.BlockSpec((1, tk, HEAD_DIM), lambda bi, h, qi, ki: (bi, ki, h))],
        out_specs=pl.BlockSpec((1, tq, gw), lambda bi, h, qi, ki: (bi, qi, h)),
        out_shape=jax.ShapeDtypeStruct((b, s, aw), BF16),
        scratch_shapes=[pltpu.VMEM((group, tq, 1), F32), pltpu.VMEM((group, tq, 2 * HEAD_DIM), F32)],
        compiler_params=_params("parallel", "parallel", "parallel", "arbitrary"),
        name="attention",
    )(q, k, v)


def _fourier_tables(seq):
    n2 = 128
    n1 = seq // n2
    a1 = 2.0 * np.pi * np.outer(np.arange(n1), np.arange(n1)) / n1
    f1 = np.concatenate([np.cos(a1), -np.sin(a1)], axis=0)
    kk = np.arange(n1)[:, None, None] + n1 * np.arange(n2)[None, :, None]
    th = 2.0 * np.pi * (kk * np.arange(n2)[None, None, :] % seq) / seq
    mc, ms = np.cos(th), np.sin(th)
    m = np.concatenate([np.concatenate([mc, ms], axis=2),
                        np.concatenate([-ms, mc], axis=2)], axis=1)
    ac = 2.0 * np.pi * np.outer(np.arange(FNET_GROUP_DIM), np.arange(FNET_GROUP_DIM)) / FNET_GROUP_DIM
    norm = 1.0 / math.sqrt(seq * FNET_GROUP_DIM)
    return (jnp.asarray(f1, BF16), jnp.asarray(m, BF16),
            jnp.asarray(np.cos(ac) * norm, BF16), jnp.asarray(np.sin(ac) * norm, BF16))


def _fourier_group_kernel(u_ref, f1_ref, m_ref, cc_ref, sc_ref, o_ref, *, n1, n2):
    x = pltpu.einshape("abc->bac", u_ref[0])
    f1 = f1_ref[...]
    y = jnp.stack([_dot(f1, x[i]).astype(BF16) for i in range(n2)], axis=0)
    y = pltpu.einshape("abc->bac", y)
    gs = [_dot(m_ref[k1], jnp.concatenate([y[k1], y[n1 + k1]], axis=0)).astype(BF16)
          for k1 in range(n1)]
    outs = [(_dot(g[:n2], cc_ref[...]) + _dot(g[n2:], sc_ref[...])).astype(BF16) for g in gs]
    o = pltpu.einshape("abc->bac", jnp.stack(outs, axis=0))
    o_ref[0] = o.reshape(n1 * n2, FNET_GROUP_DIM)


def _fourier_mix(u, *, batch, seq):
    t, width = u.shape
    n2 = 128
    n1 = seq // n2
    groups = width // FNET_GROUP_DIM
    f1, m, cc, sc = _fourier_tables(seq)
    out = pl.pallas_call(
        functools.partial(_fourier_group_kernel, n1=n1, n2=n2),
        grid=(batch, groups),
        in_specs=[pl.BlockSpec((1, n1, n2, FNET_GROUP_DIM), lambda b, g: (b, 0, 0, g)),
                  _resident((2 * n1, n1)), _resident((n1, 2 * n2, 2 * n2)),
                  _resident((FNET_GROUP_DIM, FNET_GROUP_DIM)), _resident((FNET_GROUP_DIM, FNET_GROUP_DIM))],
        out_specs=pl.BlockSpec((1, seq, FNET_GROUP_DIM), lambda b, g: (b, 0, g)),
        out_shape=jax.ShapeDtypeStruct((batch, seq, width), BF16),
        compiler_params=_params("parallel", "parallel"),
        name="fourier",
    )(u.reshape(batch, n1, n2, width), f1, m, cc, sc)
    return out.reshape(t, width)


def _proj_res_kernel(f_ref, a_ref, wf_ref, wa_ref, x_ref, gp_ref, gn_ref, xo_ref, ho_ref):
    half = x_ref.shape[0] // 2
    halves = [slice(0, half), slice(half, 2 * half)]
    ms = [_dot(f_ref[r, :], wf_ref[...]) + _dot(a_ref[r, :], wa_ref[...]) for r in halves]
    for r, m in zip(halves, ms):
        xn = x_ref[r, :] + _rms(m, gp_ref[...])
        xo_ref[r, :] = xn
        ho_ref[r, :] = _rms(xn, gn_ref[...]).astype(BF16)


def _proj_res(f, a, wf, wa, x, g_post, g_next, *, tm):
    t, d = x.shape
    row = pl.BlockSpec((tm, d), lambda i: (i, 0))
    lhs = lambda arr: pl.BlockSpec((tm, arr.shape[1]), lambda i: (i, 0))
    return pl.pallas_call(
        _proj_res_kernel,
        grid=(t // tm,),
        in_specs=[lhs(f), lhs(a), _resident(wf.shape), _resident(wa.shape), row,
                  _resident((1, d)), _resident((1, d))],
        out_specs=[row, row],
        out_shape=[jax.ShapeDtypeStruct((t, d), F32), jax.ShapeDtypeStruct((t, d), BF16)],
        compiler_params=_params("parallel"),
        name="proj_res",
    )(f, a, wf, wa, x, g_post, g_next)


def _gelu_tanh(x):
    return 0.5 * x * (1.0 + jnp.tanh(math.sqrt(2.0 / math.pi) * (x + 0.044715 * x * x * x)))


def _silu(x):
    return x * (1.0 / (1.0 + jnp.exp(-x)))


def _assemble_halo(hext, h_ref, hp_ref, hn_ref, *, tm, tps):
    i = pl.program_id(0)
    first = (i % tps) == 0
    last = (i % tps) == tps - 1
    prev_blk = jnp.where(first, jnp.zeros_like(hp_ref[...]), hp_ref[...])
    next_blk = jnp.where(last, jnp.zeros_like(hn_ref[...]), hn_ref[...])
    row = lax.broadcasted_iota(jnp.int32, prev_blk.shape, 0)
    hext[0:tm] = h_ref[...]
    hext[tm:] = jnp.where(row < HALO // 2, next_blk, prev_blk)


def _conv_rows(u, cw, cb, *, tm):
    rows = tm + HALO
    c = cb + cw[0:1] * pltpu.roll(u, 1, 0) + cw[1:2] * u + cw[2:3] * pltpu.roll(u, rows - 1, 0)
    return c[0:tm]


def _halo_specs(t, tm, kdim):
    hb = tm // HALO
    last_blk = t // HALO - 1
    return [pl.BlockSpec((tm, kdim), lambda i, j: (i, 0)),
            pl.BlockSpec((HALO, kdim), lambda i, j: (jnp.maximum(i * hb - 1, 0), 0)),
            pl.BlockSpec((HALO, kdim), lambda i, j: (jnp.minimum((i + 1) * hb, last_blk), 0))]


def _inproj1_kernel(h_ref, hp_ref, hn_ref, w_ref, cw_ref, cb_ref, z_ref, xbc_ref, hext, *, tm, tps, sub):
    xw = xbc_ref.shape[1]
    zw = z_ref.shape[1]

    @pl.when(pl.program_id(1) == 0)
    def _():
        _assemble_halo(hext, h_ref, hp_ref, hn_ref, tm=tm, tps=tps)

    hx = hext[...]
    h = hext[0:tm, :]
    ux = [_dot(hx, w_ref[:, c0:c0 + sub]) for c0 in range(0, xw, sub)]
    uz = [_dot(h, w_ref[:, xw + c0:xw + c0 + sub]) for c0 in range(0, zw, sub)]
    for k, u in enumerate(ux):
        cols = slice(k * sub, (k + 1) * sub)
        y = _conv_rows(u, cw_ref[:, cols], cb_ref[:, cols], tm=tm)
        xbc_ref[:, cols] = _silu(y).astype(BF16)
    for k, u in enumerate(uz):
        z_ref[:, k * sub:(k + 1) * sub] = u.astype(BF16)


def _inproj1(h, w_steps, conv_w, conv_b, *, seq, tm, steps, sub, d_inner, conv_dim):
    t, kdim = h.shape
    xw, zw = conv_dim // steps, d_inner // steps
    in_specs = _halo_specs(t, tm, kdim) + [
        pl.BlockSpec((kdim, xw + zw), lambda i, j: (0, j)),
        pl.BlockSpec((3, xw), lambda i, j: (0, j)),
        pl.BlockSpec((1, xw), lambda i, j: (0, j))]
    return pl.pallas_call(
        functools.partial(_inproj1_kernel, tm=tm, tps=seq // tm, sub=sub),
        grid=(t // tm, steps),
        in_specs=in_specs,
        out_specs=[pl.BlockSpec((tm, zw), lambda i, j: (i, j)), pl.BlockSpec((tm, xw), lambda i, j: (i, j))],
        out_shape=[jax.ShapeDtypeStruct((t, d_inner), BF16), jax.ShapeDtypeStruct((t, conv_dim), BF16)],
        scratch_shapes=[pltpu.VMEM((tm + HALO, kdim), BF16)],
        compiler_params=_params("parallel", "arbitrary"),
        name="inproj1",
    )(h, h, h, w_steps, conv_w, conv_b)


def _ffn_kernel(*refs, tm, tps, nj, has_next):
    (h_ref, hp_ref, hn_ref, wg_ref, wv_ref, cwg_ref, cwv_ref, cbg_ref, cbv_ref,
     wd_ref, wdl_ref, x_ref, gp_ref) = refs[:13]
    pos = 13
    gn_ref = refs[pos] if has_next else None
    pos += int(has_next)
    xo_ref = refs[pos]
    ho_ref = refs[pos + 1] if has_next else None
    hext, act_a, act_b, acc = refs[-4:]
    j = pl.program_id(1)

    def step(act_prev, act_cur):
        hx = hext[...]
        ug = _dot(hx, wg_ref[...])
        uv = _dot(hx, wv_ref[...])
        if act_prev is None:
            acc[...] = jnp.zeros(acc.shape, F32)
        else:
            acc[...] += _dot(act_prev[...], wd_ref[...])
        gate = _conv_rows(ug, cwg_ref[...], cbg_ref[...], tm=tm)
        val = _conv_rows(uv, cwv_ref[...], cbv_ref[...], tm=tm)
        act_cur[...] = (_gelu_tanh(gate) * val).astype(BF16)

    @pl.when(j == 0)
    def _():
        _assemble_halo(hext, h_ref, hp_ref, hn_ref, tm=tm, tps=tps)
        step(None, act_a)

    @pl.when((j > 0) & (j % 2 == 0))
    def _():
        step(act_b, act_a)

    @pl.when(j % 2 == 1)
    def _():
        step(act_a, act_b)

    @pl.when(j == nj - 1)
    def _():
        act_last = act_a if (nj - 1) % 2 == 0 else act_b
        m = acc[...] + _dot(act_last[...], wdl_ref[...])
        xn = x_ref[...] + _rms(m, gp_ref[...])
        xo_ref[...] = xn
        if has_next:
            ho_ref[...] = _rms(xn, gn_ref[...]).astype(BF16)


def _ffn(h, w_up, conv_w, conv_b, w_down, x, g_post, g_next, *, seq, tm, tn):
    t, d = x.shape
    d_ff = w_down.shape[0]
    nj = d_ff // tn
    has_next = g_next is not None
    row = pl.BlockSpec((tm, d), lambda i, j: (i, 0))
    in_specs = _halo_specs(t, tm, d) + [
        pl.BlockSpec((d, tn), lambda i, j: (0, j)),
        pl.BlockSpec((d, tn), lambda i, j: (0, nj + j)),
        pl.BlockSpec((3, tn), lambda i, j: (0, j)),
        pl.BlockSpec((3, tn), lambda i, j: (0, nj + j)),
        pl.BlockSpec((1, tn), lambda i, j: (0, j)),
        pl.BlockSpec((1, tn), lambda i, j: (0, nj + j)),
        pl.BlockSpec((tn, d), lambda i, j: (jnp.maximum(j - 1, 0), 0)),
        pl.BlockSpec((tn, d), lambda i, j: (nj - 1, 0), pipeline_mode=pl.Buffered(1)),
        row, _resident((1, d))]
    args = [h, h, h, w_up, w_up, conv_w, conv_w, conv_b, conv_b, w_down, w_down, x, g_post]
    out_specs = [row]
    out_shape = [jax.ShapeDtypeStruct((t, d), F32)]
    if has_next:
        in_specs.append(_resident((1, d)))
        args.append(g_next)
        out_specs.append(row)
        out_shape.append(jax.ShapeDtypeStruct((t, d), BF16))
    res = pl.pallas_call(
        functools.partial(_ffn_kernel, tm=tm, tps=seq // tm, nj=nj, has_next=has_next),
        grid=(t // tm, nj),
        in_specs=in_specs, out_specs=out_specs, out_shape=out_shape,
        scratch_shapes=[pltpu.VMEM((tm + HALO, d), BF16), pltpu.VMEM((tm, tn), BF16),
                        pltpu.VMEM((tm, tn), BF16), pltpu.VMEM((tm, d), F32)],
        compiler_params=_params("parallel", "arbitrary"),
        name="ffn",
    )(*args)
    return res if has_next else (res[0], None)


def _dt_kernel(h_ref, w_ref, b_ref, o_ref, *, per_group):
    r = _dot(h_ref[...], w_ref[...]) + b_ref[...]
    sp = jnp.maximum(r, 0.0) + jnp.log(1.0 + jnp.exp(-jnp.abs(r)))
    for g in range(o_ref.shape[1] // LANES):
        o_ref[:, g * LANES:(g + 1) * LANES] = sp if g == 0 else pltpu.roll(sp, LANES - g * per_group, 1)


def _dt_proj(h, w, b, *, tm, groups, per_group):
    t, kdim = h.shape
    assert w.shape[1] == LANES == groups * per_group
    return pl.pallas_call(
        functools.partial(_dt_kernel, per_group=per_group),
        grid=(t // tm,),
        in_specs=[pl.BlockSpec((tm, kdim), lambda i: (i, 0)), _resident((kdim, LANES)), _resident((1, LANES))],
        out_specs=pl.BlockSpec((tm, groups * LANES), lambda i: (i, 0)),
        out_shape=jax.ShapeDtypeStruct((t, groups * LANES), F32),
        compiler_params=_params("parallel"),
        name="ssd_dt",
    )(h, w, b)


def _split3(x):
    hi = x.astype(BF16)
    r = x - hi.astype(F32)
    mid = r.astype(BF16)
    lo = (r - mid.astype(F32)).astype(BF16)
    return hi, mid, lo


def _cumsum_rows(tri, x):
    hi, mid, lo = _split3(x)
    return _dot(tri, hi) + _dot(tri, mid) + _dot(tri, lo)


def _expand(x, e):
    return _dot(x.astype(BF16), e)


class _Chunk:
    pass


def _ssd_positions(c, x_ref, b_ref, c_ref, dt_ref, a_ref, tri_ref):
    rows = slice(c.r0, c.r0 + SSM_CHUNK)
    c.x = x_ref[rows, :]
    c.bm = b_ref[rows, :]
    c.cm = c_ref[rows, :]
    c.dt = dt_ref[rows, :]
    c.dta = c.dt * a_ref[...]
    c.incl = _cumsum_rows(tri_ref[...], c.dta)


def _ssd_weights(c, e_ref):
    L = SSM_CHUNK
    total = c.incl[L - 1:L, :]
    if c.backward:
        c.pos = c.incl - c.dta
        w_state = c.dt * jnp.exp(c.pos)
        w_out = jnp.exp(total - c.pos)
    else:
        c.pos = c.incl
        w_state = c.dt * jnp.exp(total - c.pos)
        w_out = jnp.exp(c.pos)
    e = e_ref[...]
    c.w_out_x = _expand(w_out, e)
    c.carry = c.w_out_x[0:1] if c.backward else c.w_out_x[L - 1:L]
    c.xs = (c.x.astype(F32) * _expand(w_state, e)).astype(BF16)
    c.b_t = c.bm.astype(F32).T.astype(BF16)
    c.cb = lax.dot_general(c.cm, c.bm, _NT, preferred_element_type=F32)
    c.pos_t = c.pos.T
    c.dt_t = c.dt.T


def _ssd_state(c, h_in):
    c.y = _dot(c.cm, h_in.astype(BF16)) * c.w_out_x
    return h_in * c.carry + _dot(c.b_t, c.xs)


def _ssd_diag(c, y_ref, *, hpg):
    L = SSM_CHUNK
    rows = slice(c.r0, c.r0 + L)
    li = lax.broadcasted_iota(jnp.int32, (L, L), 0)
    si = lax.broadcasted_iota(jnp.int32, (L, L), 1)
    mask = (si >= li) if c.backward else (li >= si)
    lane = lax.broadcasted_iota(jnp.int32, (L, 2 * SSM_HEAD_DIM), 1)
    for jp in range(hpg // 2):
        sl = slice(jp * 2 * SSM_HEAD_DIM, (jp + 1) * 2 * SSM_HEAD_DIM)
        xp = c.x[:, sl]
        outs = []
        for j in (c.lane0 + 2 * jp, c.lane0 + 2 * jp + 1):
            col = c.pos[:, j:j + 1]
            row = c.pos_t[j:j + 1, :]
            d = (row - col) if c.backward else (col - row)
            w = c.cb * jnp.exp(jnp.where(mask, d, -1e30)) * c.dt_t[j:j + 1, :]
            outs.append(_dot(w.astype(BF16), xp))
        y_ref[rows, sl] = (c.y[:, sl] + jnp.where(lane < SSM_HEAD_DIM, outs[0], outs[1])).astype(y_ref.dtype)


def _ssd_kernel(xf, bf, cf, dtf, xb, bb, cb, dtb, a_ref, tri_ref, ef_ref, eb_ref,
                yf_ref, yb_ref, hf_sc, hb_sc, *, hpg, cps):
    @pl.when(pl.program_id(2) == 0)
    def _():
        hf_sc[...] = jnp.zeros(hf_sc.shape, F32)
        hb_sc[...] = jnp.zeros(hb_sc.shape, F32)

    fwd, bwd = [], []
    for k in range(cps):
        f, b = _Chunk(), _Chunk()
        f.r0, f.lane0, f.backward = k * SSM_CHUNK, 0, False
        b.r0, b.lane0, b.backward = (cps - 1 - k) * SSM_CHUNK, hpg, True
        fwd.append(f)
        bwd.append(b)
    for f, b in zip(fwd, bwd):
        _ssd_positions(f, xf, bf, cf, dtf, a_ref, tri_ref)
        _ssd_positions(b, xb, bb, cb, dtb, a_ref, tri_ref)
    for f, b in zip(fwd, bwd):
        _ssd_weights(f, ef_ref)
        _ssd_weights(b, eb_ref)
    hf = hf_sc[...]
    hb = hb_sc[...]
    for f, b in zip(fwd, bwd):
        hf = _ssd_state(f, hf)
        hb = _ssd_state(b, hb)
    hf_sc[...] = hf
    hb_sc[...] = hb
    for f, b in zip(fwd, bwd):
        _ssd_diag(f, yf_ref, hpg=hpg)
        _ssd_diag(b, yb_ref, hpg=hpg)


def _ssd(xbc, dt, a_pad, *, batch, seq, d_inner, cps):
    t = xbc.shape[0]
    L = SSM_CHUNK
    rows = cps * L
    nc = seq // rows
    gw = d_inner // SSM_N_GROUPS
    hpg = gw // SSM_HEAD_DIM
    b0 = d_inner // SSM_D_STATE
    c0 = b0 + SSM_N_GROUPS
    tri = jnp.asarray(np.tril(np.ones((L, L))), BF16)
    head_of_lane = np.arange(gw) // SSM_HEAD_DIM
    ef = jnp.asarray(np.arange(LANES)[:, None] == head_of_lane[None, :], BF16)
    eb = jnp.asarray(np.arange(LANES)[:, None] == (head_of_lane[None, :] + hpg), BF16)

    def fwd(col):
        return lambda b, g, c: (b * nc + c, col(g))

    def bwd(col):
        return lambda b, g, c: (b * nc + (nc - 1 - c), col(g))

    def specs(order):
        return [pl.BlockSpec((rows, gw), order(lambda g: g)),
                pl.BlockSpec((rows, SSM_D_STATE), order(lambda g: b0 + g)),
                pl.BlockSpec((rows, SSM_D_STATE), order(lambda g: c0 + g)),
                pl.BlockSpec((rows, LANES), order(lambda g: g))]

    in_specs = specs(fwd) + specs(bwd) + [
        pl.BlockSpec((1, LANES), lambda b, g, c: (0, g)),
        _resident((L, L)), _resident((LANES, gw)), _resident((LANES, gw))]
    return pl.pallas_call(
        functools.partial(_ssd_kernel, hpg=hpg, cps=cps),
        grid=(batch, SSM_N_GROUPS, nc),
        in_specs=in_specs,
        out_specs=[pl.BlockSpec((rows, gw), fwd(lambda g: g)), pl.BlockSpec((rows, gw), bwd(lambda g: g))],
        out_shape=[jax.ShapeDtypeStruct((t, d_inner), BF16), jax.ShapeDtypeStruct((t, d_inner), BF16)],
        scratch_shapes=[pltpu.VMEM((SSM_D_STATE, gw), F32), pltpu.VMEM((SSM_D_STATE, gw), F32)],
        compiler_params=_params("parallel", "parallel", "arbitrary"),
        name="ssd_scan",
    )(xbc, xbc, xbc, dt, xbc, xbc, xbc, dt, a_pad, tri, ef, eb)


def _ssd_out_kernel(yf_ref, yb_ref, xs_ref, z_ref, dsk_ref, ng_ref, w_ref, x_ref, gp_ref, gn_ref,
                    xo_ref, ho_ref, acc, ss, *, d_inner):
    kk = pl.program_id(1)

    @pl.when(kk == 0)
    def _():
        acc[...] = jnp.zeros(acc.shape, F32)
        ss[...] = jnp.zeros(ss.shape, F32)

    tk = w_ref.shape[0]
    sq = None
    part = None
    for c0 in range(0, tk, OUT_SUB):
        cols = slice(c0, c0 + OUT_SUB)
        y = (yf_ref[:, cols].astype(F32) + yb_ref[:, cols].astype(F32)
             + xs_ref[:, cols].astype(F32) * dsk_ref[:, cols])
        y = y * _silu(z_ref[:, cols].astype(F32))
        y2 = y * y
        for l0 in range(0, OUT_SUB, LANES):
            sq = y2[:, l0:l0 + LANES] if sq is None else sq + y2[:, l0:l0 + LANES]
        d = _dot((y * ng_ref[:, cols]).astype(BF16), w_ref[cols, :])
        part = d if part is None else part + d
    acc[...] += part
    ss[...] += sq

    @pl.when(kk == pl.num_programs(1) - 1)
    def _():
        ssum = jnp.sum(ss[...], axis=-1, keepdims=True)
        m = acc[...] * lax.rsqrt(ssum * (1.0 / d_inner) + EPS)
        xn = x_ref[...] + _rms(m, gp_ref[...])
        xo_ref[...] = xn
        ho_ref[...] = _rms(xn, gn_ref[...]).astype(BF16)


def _ssd_out(yf, yb, xbc, z, d_exp, norm_g, w, x, g_post, g_next, *, tm, tk):
    t, d = x.shape
    d_inner = yf.shape[1]
    lhs = pl.BlockSpec((tm, tk), lambda i, k: (i, k))
    vec = pl.BlockSpec((1, tk), lambda i, k: (0, k))
    row = pl.BlockSpec((tm, d), lambda i, k: (i, 0))
    w_spec = _resident((tk, d)) if tk == d_inner else pl.BlockSpec((tk, d), lambda i, k: (k, 0))
    return pl.pallas_call(
        functools.partial(_ssd_out_kernel, d_inner=d_inner),
        grid=(t // tm, d_inner // tk),
        in_specs=[lhs, lhs, lhs, lhs, vec, vec, w_spec, row, _resident((1, d)), _resident((1, d))],
        out_specs=[row, row],
        out_shape=[jax.ShapeDtypeStruct((t, d), F32), jax.ShapeDtypeStruct((t, d), BF16)],
        scratch_shapes=[pltpu.VMEM((tm, d), F32), pltpu.VMEM((tm, LANES), F32)],
        compiler_params=_params("parallel", "arbitrary"),
        name="ssd_out",
    )(yf, yb, xbc, z, d_exp, norm_g, w, x, g_post, g_next)


def _rope_tables(seq):
    rows = seq // GRID_W
    row = jnp.repeat(jnp.arange(rows, dtype=F32), GRID_W)
    col = jnp.tile(jnp.arange(GRID_W, dtype=F32), rows)
    axis_dim = HEAD_DIM // 2
    inv_freq = ROPE_THETA ** (-jnp.arange(0, axis_dim, 2, dtype=F32) / axis_dim)
    ang = jnp.concatenate([row[:, None] * inv_freq, col[:, None] * inv_freq], axis=-1)
    cos, sin = jnp.cos(ang), jnp.sin(ang)
    return jnp.concatenate([cos, cos], axis=-1), jnp.concatenate([-sin, sin], axis=-1)


def _pick(n, pref):
    t = min(pref, n)
    while n % t:
        t -= LANES
    return t


def kernel(x_prompt, x_sample, mix_pre_g, mix_post_g, ffn_pre_g, ffn_post_g, fa_w_in, fa_q_gain, fa_k_gain, fa_w_out, ssd_w_in, ssd_conv_w, ssd_conv_b, ssd_dt_bias, ssd_a_log, ssd_d, ssd_norm_g, ssd_w_out, ffn_w_up, ffn_conv_w, ffn_conv_b, ffn_w_down):
    depth = mix_pre_g.shape[0]
    d = x_prompt.shape[2]
    aw = N_Q_HEADS * HEAD_DIM
    kvw = N_KV_HEADS * HEAD_DIM
    fw = fa_w_in.shape[2] - aw - 2 * kvw
    assert fa_w_out.shape[1] == fw + aw and fw % FNET_GROUP_DIM == 0
    d_ff = ffn_w_down.shape[1]
    d_inner = ssd_w_out.shape[1]
    n_heads = d_inner // SSM_HEAD_DIM
    hpg = n_heads // SSM_N_GROUPS
    bc_w = SSM_N_GROUPS * SSM_D_STATE
    conv_dim = d_inner + 2 * bc_w
    assert hpg % 2 == 0 and 2 * hpg <= LANES and depth % 2 == 0
    tm = TOKEN_TILE
    vec = lambda v: v.reshape(1, -1).astype(F32)

    def regroup(p, pad):
        p = p.astype(F32).reshape(2, SSM_N_GROUPS, hpg).transpose(1, 0, 2).reshape(SSM_N_GROUPS, 2 * hpg)
        if pad:
            p = jnp.pad(p, ((0, 0), (0, LANES - 2 * hpg)))
        return p.reshape(1, -1)

    layers = []
    for i in range(depth):
        j = i // 2
        lw = dict(ffn_pre=vec(ffn_pre_g[i]), ffn_post=vec(ffn_post_g[i]), mix_pre=vec(mix_pre_g[i]),
                  mix_post=vec(mix_post_g[i]), w_up=ffn_w_up[i].astype(BF16), w_down=ffn_w_down[i].astype(BF16),
                  ffn_cw=ffn_conv_w[i].astype(F32), ffn_cb=vec(ffn_conv_b[i]))
        if i % 2 == 0:
            wo = fa_w_out[j].astype(BF16)
            lw.update(w_in=fa_w_in[j].astype(BF16), qg=vec(fa_q_gain[j]), kg=vec(fa_k_gain[j]),
                      wo_f=wo[:fw], wo_a=wo[fw:])
        else:
            w_dt = ssd_w_in[j][:, d_inner + conv_dim:].reshape(d, 2, SSM_N_GROUPS, hpg)
            w_dt = w_dt.transpose(0, 2, 1, 3).reshape(d, 2 * n_heads)
            w_bf = ssd_w_in[j].astype(BF16)
            xw, zw = conv_dim // IN1_STEPS, d_inner // IN1_STEPS
            w_steps = jnp.concatenate(
                [w_bf[:, d_inner:d_inner + conv_dim].reshape(d, IN1_STEPS, xw),
                 w_bf[:, :d_inner].reshape(d, IN1_STEPS, zw)], axis=2).reshape(d, conv_dim + d_inner)
            lw.update(w_in=w_steps, w_dt=w_dt.astype(BF16), dt_bias=regroup(ssd_dt_bias[j], pad=False),
                      a_pad=regroup(-jnp.exp(ssd_a_log[j].astype(F32)), pad=True), cw=ssd_conv_w[j].astype(F32),
                      cb=vec(ssd_conv_b[j]), d_exp=jnp.repeat(ssd_d[j].astype(F32), SSM_HEAD_DIM).reshape(1, d_inner),
                      norm_g=vec(ssd_norm_g[j]), w_out=ssd_w_out[j].astype(BF16))
        layers.append(lw)

    def trunk(xin):
        batch, seq, _ = xin.shape
        assert seq % tm == 0
        t = batch * seq
        x = xin.reshape(t, d)
        cosf, sinf = _rope_tables(seq)
        h = None
        for i, lw in enumerate(layers):
            if i % 2 == 0:
                u, q, k, v = _inproj0(x, lw["mix_pre"], lw["w_in"], lw["qg"], lw["kg"], cosf, sinf,
                                      seq=seq, fw=fw, aw=aw, kvw=kvw, tm=tm)
                a = _attention(q.reshape(batch, seq, aw), k.reshape(batch, seq, kvw), v.reshape(batch, seq, kvw),
                               tq=_pick(seq, ATTN_TILE), tk=_pick(seq, ATTN_TILE)).reshape(t, aw)
                f = _fourier_mix(u, batch=batch, seq=seq)
                x, h = _proj_res(f, a, lw["wo_f"], lw["wo_a"], x, lw["mix_post"], lw["ffn_pre"], tm=tm)
            else:
                z, xbc = _inproj1(h, lw["w_in"], lw["cw"], lw["cb"], seq=seq, tm=tm,
                                  steps=IN1_STEPS, sub=IN1_SUB, d_inner=d_inner, conv_dim=conv_dim)
                dt = _dt_proj(h, lw["w_dt"], lw["dt_bias"], tm=tm, groups=SSM_N_GROUPS, per_group=2 * hpg)
                yf, yb = _ssd(xbc, dt, lw["a_pad"], batch=batch, seq=seq, d_inner=d_inner,
                              cps=SSD_CHUNKS_PER_STEP)
                x, h = _ssd_out(yf, yb, xbc, z, lw["d_exp"], lw["norm_g"], lw["w_out"], x,
                                lw["mix_post"], lw["ffn_pre"], tm=SSD_OUT_TILE, tk=d_inner)
            g_next = layers[i + 1]["mix_pre"] if i + 1 < depth else None
            x, h = _ffn(h, lw["w_up"], lw["ffn_cw"], lw["ffn_cb"], lw["w_down"], x, lw["ffn_post"], g_next,
                        seq=seq, tm=tm, tn=_pick(d_ff, FFN_CHUNK))
        return x.reshape(batch, seq, d)

    return (trunk(x_prompt), trunk(x_sample))
```

```python
import functools
import math

import jax
import jax.numpy as jnp
import numpy as np
from jax import lax
from jax.experimental import pallas as pl
from jax.experimental.pallas import tpu as pltpu

F32 = jnp.float32
BF16 = jnp.bfloat16

EPS = 1e-6
GRID_W = 64
HEAD_DIM = 128
N_Q_HEADS = 8
N_KV_HEADS = 2
FNET_GROUP_DIM = 128
ROPE_THETA = 10000.0
SSM_HEAD_DIM = 64
SSM_N_GROUPS = 8
SSM_D_STATE = 128
SSM_CHUNK = 128
LANES = 128
HALO = 16
OUT_SUB = 256
IN1_STEPS = 4
IN1_SUB = 512
TOKEN_TILE = 512
SSD_OUT_TILE = 256
ATTN_Q_TILE = 1024
ATTN_K_TILE = 2048
FFN_CHUNK = 512
SSD_CHUNKS_PER_STEP = 16
VMEM_LIMIT = 56 * 1024 * 1024

_NT = (((1,), (1,)), ((), ()))


def _params(*sem):
    return pltpu.CompilerParams(dimension_semantics=sem, vmem_limit_bytes=VMEM_LIMIT)


def _resident(shape):
    nd = len(shape)
    return pl.BlockSpec(shape, lambda *_: (0,) * nd, pipeline_mode=pl.Buffered(1))


def _rms(x, g):
    return x * lax.rsqrt(jnp.mean(x * x, axis=-1, keepdims=True) + EPS) * g


def _dot(a, b):
    return jnp.dot(a, b, preferred_element_type=F32)


def _inproj0_kernel(x_ref, g_ref, w_ref, qg_ref, kg_ref, cos_ref, sin_ref,
                    u_ref, q_ref, k_ref, v_ref, *, fw, aw, kvw, scale):
    h = _rms(x_ref[...], g_ref[...]).astype(BF16)
    cosf = cos_ref[...]
    sinf = sin_ref[...]

    def norm_rope(t, gain, mult):
        t = _rms(t, gain)
        t = t * cosf + pltpu.roll(t, HEAD_DIM // 2, 1) * sinf
        return (t * mult).astype(BF16)

    q = _dot(h, w_ref[:, fw:fw + aw])
    k = _dot(h, w_ref[:, fw + aw:fw + aw + kvw])
    v = _dot(h, w_ref[:, fw + aw + kvw:])
    u = _dot(h, w_ref[:, :fw])
    for hh in range(aw // HEAD_DIM):
        sl = slice(hh * HEAD_DIM, (hh + 1) * HEAD_DIM)
        q_ref[:, sl] = norm_rope(q[:, sl], qg_ref[...], scale)
    for hh in range(kvw // HEAD_DIM):
        sl = slice(hh * HEAD_DIM, (hh + 1) * HEAD_DIM)
        k_ref[:, sl] = norm_rope(k[:, sl], kg_ref[...], 1.0)
    v_ref[...] = v.astype(BF16)
    u_ref[...] = u.astype(BF16)


def _inproj0(x, g, w, qg, kg, cosf, sinf, *, seq, fw, aw, kvw, tm):
    t, d = x.shape
    n = w.shape[1]
    tps = seq // tm
    kern = functools.partial(_inproj0_kernel, fw=fw, aw=aw, kvw=kvw, scale=math.log2(math.e) * HEAD_DIM ** -0.5)
    row = lambda width: pl.BlockSpec((tm, width), lambda i: (i, 0))
    return pl.pallas_call(
        kern,
        grid=(t // tm,),
        in_specs=[row(d), _resident((1, d)), _resident((d, n)),
                  _resident((1, HEAD_DIM)), _resident((1, HEAD_DIM)),
                  pl.BlockSpec((tm, HEAD_DIM), lambda i: (i % tps, 0)),
                  pl.BlockSpec((tm, HEAD_DIM), lambda i: (i % tps, 0))],
        out_specs=[row(fw), row(aw), row(kvw), row(kvw)],
        out_shape=[jax.ShapeDtypeStruct((t, fw), BF16), jax.ShapeDtypeStruct((t, aw), BF16),
                   jax.ShapeDtypeStruct((t, kvw), BF16), jax.ShapeDtypeStruct((t, kvw), BF16)],
        compiler_params=_params("parallel"),
        name="inproj0",
    )(x, g, w, qg, kg, cosf, sinf)


def _attn_kernel(q_ref, k_ref, v_ref, o_ref, m_sc, acc_sc, *, group):
    ki = pl.program_id(3)

    @pl.when(ki == 0)
    def _():
        m_sc[...] = jnp.full(m_sc.shape, -jnp.inf, F32)
        acc_sc[...] = jnp.zeros(acc_sc.shape, F32)

    k = k_ref[0]
    v = v_ref[0]
    v_ext = jnp.concatenate([v, jnp.ones_like(v)], axis=1)
    heads = range(group)
    s = [lax.dot_general(q_ref[0, :, j * HEAD_DIM:(j + 1) * HEAD_DIM], k, _NT,
                         preferred_element_type=F32).astype(BF16) for j in heads]
    p, alpha = [], []
    for j in heads:
        m_prev = m_sc[j]
        m_new = jnp.maximum(m_prev, jnp.max(s[j], axis=-1, keepdims=True).astype(F32))
        alpha.append(jnp.exp2(m_prev - m_new))
        p.append(jnp.exp2(s[j] - m_new.astype(BF16)))
        m_sc[j] = m_new
    for j in heads:
        acc_sc[j] = alpha[j] * acc_sc[j] + _dot(p[j], v_ext)

    @pl.when(ki == pl.num_programs(3) - 1)
    def _():
        for j in heads:
            acc = acc_sc[j]
            o_ref[0, :, j * HEAD_DIM:(j + 1) * HEAD_DIM] = (acc[:, :HEAD_DIM] / acc[:, HEAD_DIM:]).astype(BF16)


def _attention(q, k, v, *, tq, tk):
    b, s, aw = q.shape
    group = aw // HEAD_DIM // N_KV_HEADS
    gw = group * HEAD_DIM
    return pl.pallas_call(
        functools.partial(_attn_kernel, group=group),
        grid=(b, N_KV_HEADS, s // tq, s // tk),
        in_specs=[pl.BlockSpec((1, tq, gw), lambda bi, h, qi, ki: (bi, qi, h)),
                  pl.BlockSpec((1, tk, HEAD_DIM), lambda bi, h, qi, ki: (bi, ki, h)),
                  pl.BlockSpec((1, tk, HEAD_DIM), lambda bi, h, qi, ki: (bi, ki, h))],
        out_specs=pl.BlockSpec((1, tq, gw), lambda bi, h, qi, ki: (bi, qi, h)),
        out_shape=jax.ShapeDtypeStruct((b, s, aw), BF16),
        scratch_shapes=[pltpu.VMEM((group, tq, 1), F32), pltpu.VMEM((group, tq, 2 * HEAD_DIM), F32)],
        compiler_params=_params("parallel", "parallel", "parallel", "arbitrary"),
        name="attention",
    )(q, k, v)


def _fourier_tables(seq):
    n2 = 128
    n1 = seq // n2
    a1 = 2.0 * np.pi * np.outer(np.arange(n1), np.arange(n1)) / n1
    f1 = np.concatenate([np.cos(a1), -np.sin(a1)], axis=0)
    kk = np.arange(n1)[:, None, None] + n1 * np.arange(n2)[None, :, None]
    th = 2.0 * np.pi * (kk * np.arange(n2)[None, None, :] % seq) / seq
    mc, ms = np.cos(th), np.sin(th)
    m = np.concatenate([np.concatenate([mc, ms], axis=2),
                        np.concatenate([-ms, mc], axis=2)], axis=1)
    ac = 2.0 * np.pi * np.outer(np.arange(FNET_GROUP_DIM), np.arange(FNET_GROUP_DIM)) / FNET_GROUP_DIM
    norm = 1.0 / math.sqrt(seq * FNET_GROUP_DIM)
    return (jnp.asarray(f1, BF16), jnp.asarray(m, BF16),
            jnp.asarray(np.cos(ac) * norm, BF16), jnp.asarray(np.sin(ac) * norm, BF16))


def _fourier_group_kernel(u_ref, f1_ref, m_ref, cc_ref, sc_ref, o_ref, *, n1, n2):
    x = pltpu.einshape("abc->bac", u_ref[0])
    f1 = f1_ref[...]
    y = jnp.stack([_dot(f1, x[i]).astype(BF16) for i in range(n2)], axis=0)
    y = pltpu.einshape("abc->bac", y)
    gs = [_dot(m_ref[k1], jnp.concatenate([y[k1], y[n1 + k1]], axis=0)).astype(BF16)
          for k1 in range(n1)]
    outs = [(_dot(g[:n2], cc_ref[...]) + _dot(g[n2:], sc_ref[...])).astype(BF16) for g in gs]
    o = pltpu.einshape("abc->bac", jnp.stack(outs, axis=0))
    o_ref[0] = o.reshape(n1 * n2, FNET_GROUP_DIM)


def _fourier_mix(u, *, batch, seq):
    t, width = u.shape
    n2 = 128
    n1 = seq // n2
    groups = width // FNET_GROUP_DIM
    f1, m, cc, sc = _fourier_tables(seq)
    out = pl.pallas_call(
        functools.partial(_fourier_group_kernel, n1=n1, n2=n2),
        grid=(batch, groups),
        in_specs=[pl.BlockSpec((1, n1, n2, FNET_GROUP_DIM), lambda b, g: (b, 0, 0, g)),
                  _resident((2 * n1, n1)), _resident((n1, 2 * n2, 2 * n2)),
                  _resident((FNET_GROUP_DIM, FNET_GROUP_DIM)), _resident((FNET_GROUP_DIM, FNET_GROUP_DIM))],
        out_specs=pl.BlockSpec((1, seq, FNET_GROUP_DIM), lambda b, g: (b, 0, g)),
        out_shape=jax.ShapeDtypeStruct((batch, seq, width), BF16),
        compiler_params=_params("parallel", "parallel"),
        name="fourier",
    )(u.reshape(batch, n1, n2, width), f1, m, cc, sc)
    return out.reshape(t, width)


def _proj_res_kernel(f_ref, a_ref, wf_ref, wa_ref, x_ref, gp_ref, gn_ref, xo_ref, ho_ref):
    half = x_ref.shape[0] // 2
    halves = [slice(0, half), slice(half, 2 * half)]
    ms = [_dot(f_ref[r, :], wf_ref[...]) + _dot(a_ref[r, :], wa_ref[...]) for r in halves]
    for r, m in zip(halves, ms):
        xn = x_ref[r, :] + _rms(m, gp_ref[...])
        xo_ref[r, :] = xn
        ho_ref[r, :] = _rms(xn, gn_ref[...]).astype(BF16)


def _proj_res(f, a, wf, wa, x, g_post, g_next, *, tm):
    t, d = x.shape
    row = pl.BlockSpec((tm, d), lambda i: (i, 0))
    lhs = lambda arr: pl.BlockSpec((tm, arr.shape[1]), lambda i: (i, 0))
    return pl.pallas_call(
        _proj_res_kernel,
        grid=(t // tm,),
        in_specs=[lhs(f), lhs(a), _resident(wf.shape), _resident(wa.shape), row,
                  _resident((1, d)), _resident((1, d))],
        out_specs=[row, row],
        out_shape=[jax.ShapeDtypeStruct((t, d), F32), jax.ShapeDtypeStruct((t, d), BF16)],
        compiler_params=_params("parallel"),
        name="proj_res",
    )(f, a, wf, wa, x, g_post, g_next)


def _gelu_tanh(x):
    return 0.5 * x * (1.0 + jnp.tanh(math.sqrt(2.0 / math.pi) * (x + 0.044715 * x * x * x)))


def _silu(x):
    return x * (1.0 / (1.0 + jnp.exp(-x)))


def _assemble_halo(hext, h_ref, hp_ref, hn_ref, *, tm, tps):
    i = pl.program_id(0)
    first = (i % tps) == 0
    last = (i % tps) == tps - 1
    prev_blk = jnp.where(first, jnp.zeros_like(hp_ref[...]), hp_ref[...])
    next_blk = jnp.where(last, jnp.zeros_like(hn_ref[...]), hn_ref[...])
    row = lax.broadcasted_iota(jnp.int32, prev_blk.shape, 0)
    hext[0:tm] = h_ref[...]
    hext[tm:] = jnp.where(row < HALO // 2, next_blk, prev_blk)


def _conv_rows(u, cw, cb, *, tm):
    rows = tm + HALO
    c = cb + cw[0:1] * pltpu.roll(u, 1, 0) + cw[1:2] * u + cw[2:3] * pltpu.roll(u, rows - 1, 0)
    return c[0:tm]


def _halo_specs(t, tm, kdim):
    hb = tm // HALO
    last_blk = t // HALO - 1
    return [pl.BlockSpec((tm, kdim), lambda i, j: (i, 0)),
            pl.BlockSpec((HALO, kdim), lambda i, j: (jnp.maximum(i * hb - 1, 0), 0)),
            pl.BlockSpec((HALO, kdim), lambda i, j: (jnp.minimum((i + 1) * hb, last_blk), 0))]


def _inproj1_kernel(h_ref, hp_ref, hn_ref, w_ref, cw_ref, cb_ref, z_ref, xbc_ref, hext, *, tm, tps, sub):
    xw = xbc_ref.shape[1]
    zw = z_ref.shape[1]

    @pl.when(pl.program_id(1) == 0)
    def _():
        _assemble_halo(hext, h_ref, hp_ref, hn_ref, tm=tm, tps=tps)

    hx = hext[...]
    h = hext[0:tm, :]
    ux = [_dot(hx, w_ref[:, c0:c0 + sub]) for c0 in range(0, xw, sub)]
    uz = [_dot(h, w_ref[:, xw + c0:xw + c0 + sub]) for c0 in range(0, zw, sub)]
    for k, u in enumerate(ux):
        cols = slice(k * sub, (k + 1) * sub)
        y = _conv_rows(u, cw_ref[:, cols], cb_ref[:, cols], tm=tm)
        xbc_ref[:, cols] = _silu(y).astype(BF16)
    for k, u in enumerate(uz):
        z_ref[:, k * sub:(k + 1) * sub] = u.astype(BF16)


def _inproj1(h, w_steps, conv_w, conv_b, *, seq, tm, steps, sub, d_inner, conv_dim):
    t, kdim = h.shape
    xw, zw = conv_dim // steps, d_inner // steps
    in_specs = _halo_specs(t, tm, kdim) + [
        pl.BlockSpec((kdim, xw + zw), lambda i, j: (0, j)),
        pl.BlockSpec((3, xw), lambda i, j: (0, j)),
        pl.BlockSpec((1, xw), lambda i, j: (0, j))]
    return pl.pallas_call(
        functools.partial(_inproj1_kernel, tm=tm, tps=seq // tm, sub=sub),
        grid=(t // tm, steps),
        in_specs=in_specs,
        out_specs=[pl.BlockSpec((tm, zw), lambda i, j: (i, j)), pl.BlockSpec((tm, xw), lambda i, j: (i, j))],
        out_shape=[jax.ShapeDtypeStruct((t, d_inner), BF16), jax.ShapeDtypeStruct((t, conv_dim), BF16)],
        scratch_shapes=[pltpu.VMEM((tm + HALO, kdim), BF16)],
        compiler_params=_params("parallel", "arbitrary"),
        name="inproj1",
    )(h, h, h, w_steps, conv_w, conv_b)


def _ffn_kernel(*refs, tm, tps, nj, has_next):
    (h_ref, hp_ref, hn_ref, wg_ref, wv_ref, cwg_ref, cwv_ref, cbg_ref, cbv_ref,
     wd_ref, wdl_ref, x_ref, gp_ref) = refs[:13]
    pos = 13
    gn_ref = refs[pos] if has_next else None
    pos += int(has_next)
    xo_ref = refs[pos]
    ho_ref = refs[pos + 1] if has_next else None
    hext, act_a, act_b, acc = refs[-4:]
    j = pl.program_id(1)

    def step(act_prev, act_cur):
        hx = hext[...]
        ug = _dot(hx, wg_ref[...])
        uv = _dot(hx, wv_ref[...])
        if act_prev is None:
            acc[...] = jnp.zeros(acc.shape, F32)
        else:
            acc[...] += _dot(act_prev[...], wd_ref[...])
        gate = _conv_rows(ug, cwg_ref[...], cbg_ref[...], tm=tm)
        val = _conv_rows(uv, cwv_ref[...], cbv_ref[...], tm=tm)
        act_cur[...] = (_gelu_tanh(gate) * val).astype(BF16)

    @pl.when(j == 0)
    def _():
        _assemble_halo(hext, h_ref, hp_ref, hn_ref, tm=tm, tps=tps)
        step(None, act_a)

    @pl.when((j > 0) & (j % 2 == 0))
    def _():
        step(act_b, act_a)

    @pl.when(j % 2 == 1)
    def _():
        step(act_a, act_b)

    @pl.when(j == nj - 1)
    def _():
        act_last = act_a if (nj - 1) % 2 == 0 else act_b
        m = acc[...] + _dot(act_last[...], wdl_ref[...])
        xn = x_ref[...] + _rms(m, gp_ref[...])
        xo_ref[...] = xn
        if has_next:
            ho_ref[...] = _rms(xn, gn_ref[...]).astype(BF16)


def _ffn(h, w_up, conv_w, conv_b, w_down, x, g_post, g_next, *, seq, tm, tn):
    t, d = x.shape
    d_ff = w_down.shape[0]
    nj = d_ff // tn
    has_next = g_next is not None
    row = pl.BlockSpec((tm, d), lambda i, j: (i, 0))
    in_specs = _halo_specs(t, tm, d) + [
        pl.BlockSpec((d, tn), lambda i, j: (0, j)),
        pl.BlockSpec((d, tn), lambda i, j: (0, nj + j)),
        pl.BlockSpec((3, tn), lambda i, j: (0, j)),
        pl.BlockSpec((3, tn), lambda i, j: (0, nj + j)),
        pl.BlockSpec((1, tn), lambda i, j: (0, j)),
        pl.BlockSpec((1, tn), lambda i, j: (0, nj + j)),
        pl.BlockSpec((tn, d), lambda i, j: (jnp.maximum(j - 1, 0), 0)),
        pl.BlockSpec((tn, d), lambda i, j: (nj - 1, 0), pipeline_mode=pl.Buffered(1)),
        row, _resident((1, d))]
    args = [h, h, h, w_up, w_up, conv_w, conv_w, conv_b, conv_b, w_down, w_down, x, g_post]
    out_specs = [row]
    out_shape = [jax.ShapeDtypeStruct((t, d), F32)]
    if has_next:
        in_specs.append(_resident((1, d)))
        args.append(g_next)
        out_specs.append(row)
        out_shape.append(jax.ShapeDtypeStruct((t, d), BF16))
    res = pl.pallas_call(
        functools.partial(_ffn_kernel, tm=tm, tps=seq // tm, nj=nj, has_next=has_next),
        grid=(t // tm, nj),
        in_specs=in_specs, out_specs=out_specs, out_shape=out_shape,
        scratch_shapes=[pltpu.VMEM((tm + HALO, d), BF16), pltpu.VMEM((tm, tn), BF16),
                        pltpu.VMEM((tm, tn), BF16), pltpu.VMEM((tm, d), F32)],
        compiler_params=_params("parallel", "arbitrary"),
        name="ffn",
    )(*args)
    return res if has_next else (res[0], None)


def _dt_kernel(h_ref, w_ref, b_ref, o_ref, *, per_group):
    r = _dot(h_ref[...], w_ref[...]) + b_ref[...]
    sp = jnp.maximum(r, 0.0) + jnp.log(1.0 + jnp.exp(-jnp.abs(r)))
    for g in range(o_ref.shape[1] // LANES):
        o_ref[:, g * LANES:(g + 1) * LANES] = sp if g == 0 else pltpu.roll(sp, LANES - g * per_group, 1)


def _dt_proj(h, w, b, *, tm, groups, per_group):
    t, kdim = h.shape
    assert w.shape[1] == LANES == groups * per_group
    return pl.pallas_call(
        functools.partial(_dt_kernel, per_group=per_group),
        grid=(t // tm,),
        in_specs=[pl.BlockSpec((tm, kdim), lambda i: (i, 0)), _resident((kdim, LANES)), _resident((1, LANES))],
        out_specs=pl.BlockSpec((tm, groups * LANES), lambda i: (i, 0)),
        out_shape=jax.ShapeDtypeStruct((t, groups * LANES), F32),
        compiler_params=_params("parallel"),
        name="ssd_dt",
    )(h, w, b)


def _split3(x):
    hi = x.astype(BF16)
    r = x - hi.astype(F32)
    mid = r.astype(BF16)
    lo = (r - mid.astype(F32)).astype(BF16)
    return hi, mid, lo


def _cumsum_rows(tri, x):
    hi, mid, lo = _split3(x)
    return _dot(tri, hi) + _dot(tri, mid) + _dot(tri, lo)


def _expand(x, e):
    return _dot(x.astype(BF16), e)


class _Chunk:
    pass


def _ssd_positions(c, x_ref, b_ref, c_ref, dt_ref, a_ref, tri_ref):
    rows = slice(c.r0, c.r0 + SSM_CHUNK)
    c.x = x_ref[rows, :]
    c.bm = b_ref[rows, :]
    c.cm = c_ref[rows, :]
    c.dt = dt_ref[rows, :]
    c.dta = c.dt * a_ref[...]
    c.incl = _cumsum_rows(tri_ref[...], c.dta)


def _ssd_weights(c, e_ref):
    L = SSM_CHUNK
    total = c.incl[L - 1:L, :]
    if c.backward:
        c.pos = c.incl - c.dta
        w_state = c.dt * jnp.exp(c.pos)
        w_out = jnp.exp(total - c.pos)
    else:
        c.pos = c.incl
        w_state = c.dt * jnp.exp(total - c.pos)
        w_out = jnp.exp(c.pos)
    e = e_ref[...]
    c.w_out_x = _expand(w_out, e)
    c.carry = c.w_out_x[0:1] if c.backward else c.w_out_x[L - 1:L]
    c.xs = (c.x.astype(F32) * _expand(w_state, e)).astype(BF16)
    c.b_t = c.bm.astype(F32).T.astype(BF16)
    c.cb = lax.dot_general(c.cm, c.bm, _NT, preferred_element_type=F32)
    c.pos_t = c.pos.T
    c.dt_t = c.dt.T


def _ssd_state(c, h_in):
    c.y = _dot(c.cm, h_in.astype(BF16)) * c.w_out_x
    return h_in * c.carry + _dot(c.b_t, c.xs)


def _ssd_diag(c, y_ref, *, hpg):
    L = SSM_CHUNK
    rows = slice(c.r0, c.r0 + L)
    li = lax.broadcasted_iota(jnp.int32, (L, L), 0)
    si = lax.broadcasted_iota(jnp.int32, (L, L), 1)
    mask = (si >= li) if c.backward else (li >= si)
    lane = lax.broadcasted_iota(jnp.int32, (L, 2 * SSM_HEAD_DIM), 1)
    for jp in range(hpg // 2):
        sl = slice(jp * 2 * SSM_HEAD_DIM, (jp + 1) * 2 * SSM_HEAD_DIM)
        xp = c.x[:, sl]
        outs = []
        for j in (c.lane0 + 2 * jp, c.lane0 + 2 * jp + 1):
            col = c.pos[:, j:j + 1]
            row = c.pos_t[j:j + 1, :]
            d = (row - col) if c.backward else (col - row)
            w = c.cb * jnp.exp(jnp.where(mask, d, -1e30)) * c.dt_t[j:j + 1, :]
            outs.append(_dot(w.astype(BF16), xp))
        y_ref[rows, sl] = (c.y[:, sl] + jnp.where(lane < SSM_HEAD_DIM, outs[0], outs[1])).astype(y_ref.dtype)


def _ssd_kernel(xf, bf, cf, dtf, xb, bb, cb, dtb, a_ref, tri_ref, ef_ref, eb_ref,
                yf_ref, yb_ref, hf_sc, hb_sc, *, hpg, cps):
    @pl.when(pl.program_id(2) == 0)
    def _():
        hf_sc[...] = jnp.zeros(hf_sc.shape, F32)
        hb_sc[...] = jnp.zeros(hb_sc.shape, F32)

    fwd, bwd = [], []
    for k in range(cps):
        f, b = _Chunk(), _Chunk()
        f.r0, f.lane0, f.backward = k * SSM_CHUNK, 0, False
        b.r0, b.lane0, b.backward = (cps - 1 - k) * SSM_CHUNK, hpg, True
        fwd.append(f)
        bwd.append(b)
    for f, b in zip(fwd, bwd):
        _ssd_positions(f, xf, bf, cf, dtf, a_ref, tri_ref)
        _ssd_positions(b, xb, bb, cb, dtb, a_ref, tri_ref)
    for f, b in zip(fwd, bwd):
        _ssd_weights(f, ef_ref)
        _ssd_weights(b, eb_ref)
    hf = hf_sc[...]
    hb = hb_sc[...]
    for f, b in zip(fwd, bwd):
        hf = _ssd_state(f, hf)
        hb = _ssd_state(b, hb)
    hf_sc[...] = hf
    hb_sc[...] = hb
    for f, b in zip(fwd, bwd):
        _ssd_diag(f, yf_ref, hpg=hpg)
        _ssd_diag(b, yb_ref, hpg=hpg)


def _ssd(xbc, dt, a_pad, *, batch, seq, d_inner, cps):
    t = xbc.shape[0]
    L = SSM_CHUNK
    rows = cps * L
    nc = seq // rows
    gw = d_inner // SSM_N_GROUPS
    hpg = gw // SSM_HEAD_DIM
    b0 = d_inner // SSM_D_STATE
    c0 = b0 + SSM_N_GROUPS
    tri = jnp.asarray(np.tril(np.ones((L, L))), BF16)
    head_of_lane = np.arange(gw) // SSM_HEAD_DIM
    ef = jnp.asarray(np.arange(LANES)[:, None] == head_of_lane[None, :], BF16)
    eb = jnp.asarray(np.arange(LANES)[:, None] == (head_of_lane[None, :] + hpg), BF16)

    def fwd(col):
        return lambda b, g, c: (b * nc + c, col(g))

    def bwd(col):
        return lambda b, g, c: (b * nc + (nc - 1 - c), col(g))

    def specs(order):
        return [pl.BlockSpec((rows, gw), order(lambda g: g)),
                pl.BlockSpec((rows, SSM_D_STATE), order(lambda g: b0 + g)),
                pl.BlockSpec((rows, SSM_D_STATE), order(lambda g: c0 + g)),
                pl.BlockSpec((rows, LANES), order(lambda g: g))]

    in_specs = specs(fwd) + specs(bwd) + [
        pl.BlockSpec((1, LANES), lambda b, g, c: (0, g)),
        _resident((L, L)), _resident((LANES, gw)), _resident((LANES, gw))]
    return pl.pallas_call(
        functools.partial(_ssd_kernel, hpg=hpg, cps=cps),
        grid=(batch, SSM_N_GROUPS, nc),
        in_specs=in_specs,
        out_specs=[pl.BlockSpec((rows, gw), fwd(lambda g: g)), pl.BlockSpec((rows, gw), bwd(lambda g: g))],
        out_shape=[jax.ShapeDtypeStruct((t, d_inner), BF16), jax.ShapeDtypeStruct((t, d_inner), BF16)],
        scratch_shapes=[pltpu.VMEM((SSM_D_STATE, gw), F32), pltpu.VMEM((SSM_D_STATE, gw), F32)],
        compiler_params=_params("parallel", "parallel", "arbitrary"),
        name="ssd_scan",
    )(xbc, xbc, xbc, dt, xbc, xbc, xbc, dt, a_pad, tri, ef, eb)


def _ssd_out_kernel(yf_ref, yb_ref, xs_ref, z_ref, dsk_ref, ng_ref, w_ref, x_ref, gp_ref, gn_ref,
                    xo_ref, ho_ref, acc, ss, *, d_inner):
    kk = pl.program_id(1)

    @pl.when(kk == 0)
    def _():
        acc[...] = jnp.zeros(acc.shape, F32)
        ss[...] = jnp.zeros(ss.shape, F32)

    tk = w_ref.shape[0]
    sq = None
    part = None
    for c0 in range(0, tk, OUT_SUB):
        cols = slice(c0, c0 + OUT_SUB)
        y = (yf_ref[:, cols].astype(F32) + yb_ref[:, cols].astype(F32)
             + xs_ref[:, cols].astype(F32) * dsk_ref[:, cols])
        y = y * _silu(z_ref[:, cols].astype(F32))
        y2 = y * y
        for l0 in range(0, OUT_SUB, LANES):
            sq = y2[:, l0:l0 + LANES] if sq is None else sq + y2[:, l0:l0 + LANES]
        d = _dot((y * ng_ref[:, cols]).astype(BF16), w_ref[cols, :])
        part = d if part is None else part + d
    acc[...] += part
    ss[...] += sq

    @pl.when(kk == pl.num_programs(1) - 1)
    def _():
        ssum = jnp.sum(ss[...], axis=-1, keepdims=True)
        m = acc[...] * lax.rsqrt(ssum * (1.0 / d_inner) + EPS)
        xn = x_ref[...] + _rms(m, gp_ref[...])
        xo_ref[...] = xn
        ho_ref[...] = _rms(xn, gn_ref[...]).astype(BF16)


def _ssd_out(yf, yb, xbc, z, d_exp, norm_g, w, x, g_post, g_next, *, tm, tk):
    t, d = x.shape
    d_inner = yf.shape[1]
    lhs = pl.BlockSpec((tm, tk), lambda i, k: (i, k))
    vec = pl.BlockSpec((1, tk), lambda i, k: (0, k))
    row = pl.BlockSpec((tm, d), lambda i, k: (i, 0))
    w_spec = _resident((tk, d)) if tk == d_inner else pl.BlockSpec((tk, d), lambda i, k: (k, 0))
    return pl.pallas_call(
        functools.partial(_ssd_out_kernel, d_inner=d_inner),
        grid=(t // tm, d_inner // tk),
        in_specs=[lhs, lhs, lhs, lhs, vec, vec, w_spec, row, _resident((1, d)), _resident((1, d))],
        out_specs=[row, row],
        out_shape=[jax.ShapeDtypeStruct((t, d), F32), jax.ShapeDtypeStruct((t, d), BF16)],
        scratch_shapes=[pltpu.VMEM((tm, d), F32), pltpu.VMEM((tm, LANES), F32)],
        compiler_params=_params("parallel", "arbitrary"),
        name="ssd_out",
    )(yf, yb, xbc, z, d_exp, norm_g, w, x, g_post, g_next)


def _rope_tables(seq):
    rows = seq // GRID_W
    row = jnp.repeat(jnp.arange(rows, dtype=F32), GRID_W)
    col = jnp.tile(jnp.arange(GRID_W, dtype=F32), rows)
    axis_dim = HEAD_DIM // 2
    inv_freq = ROPE_THETA ** (-jnp.arange(0, axis_dim, 2, dtype=F32) / axis_dim)
    ang = jnp.concatenate([row[:, None] * inv_freq, col[:, None] * inv_freq], axis=-1)
    cos, sin = jnp.cos(ang), jnp.sin(ang)
    return jnp.concatenate([cos, cos], axis=-1), jnp.concatenate([-sin, sin], axis=-1)


def _pick(n, pref):
    t = min(pref, n)
    while n % t:
        t -= LANES
    return t


def kernel(x_prompt, x_sample, mix_pre_g, mix_post_g, ffn_pre_g, ffn_post_g, fa_w_in, fa_q_gain, fa_k_gain, fa_w_out, ssd_w_in, ssd_conv_w, ssd_conv_b, ssd_dt_bias, ssd_a_log, ssd_d, ssd_norm_g, ssd_w_out, ffn_w_up, ffn_conv_w, ffn_conv_b, ffn_w_down):
    depth = mix_pre_g.shape[0]
    d = x_prompt.shape[2]
    aw = N_Q_HEADS * HEAD_DIM
    kvw = N_KV_HEADS * HEAD_DIM
    fw = fa_w_in.shape[2] - aw - 2 * kvw
    assert fa_w_out.shape[1] == fw + aw and fw % FNET_GROUP_DIM == 0
    d_ff = ffn_w_down.shape[1]
    d_inner = ssd_w_out.shape[1]
    n_heads = d_inner // SSM_HEAD_DIM
    hpg = n_heads // SSM_N_GROUPS
    bc_w = SSM_N_GROUPS * SSM_D_STATE
    conv_dim = d_inner + 2 * bc_w
    assert hpg % 2 == 0 and 2 * hpg <= LANES and depth % 2 == 0
    tm = TOKEN_TILE
    vec = lambda v: v.reshape(1, -1).astype(F32)

    def regroup(p, pad):
        p = p.astype(F32).reshape(2, SSM_N_GROUPS, hpg).transpose(1, 0, 2).reshape(SSM_N_GROUPS, 2 * hpg)
        if pad:
            p = jnp.pad(p, ((0, 0), (0, LANES - 2 * hpg)))
        return p.reshape(1, -1)

    layers = []
    for i in range(depth):
        j = i // 2
        lw = dict(ffn_pre=vec(ffn_pre_g[i]), ffn_post=vec(ffn_post_g[i]), mix_pre=vec(mix_pre_g[i]),
                  mix_post=vec(mix_post_g[i]), w_up=ffn_w_up[i].astype(BF16), w_down=ffn_w_down[i].astype(BF16),
                  ffn_cw=ffn_conv_w[i].astype(F32), ffn_cb=vec(ffn_conv_b[i]))
        if i % 2 == 0:
            wo = fa_w_out[j].astype(BF16)
            lw.update(w_in=fa_w_in[j].astype(BF16), qg=vec(fa_q_gain[j]), kg=vec(fa_k_gain[j]),
                      wo_f=wo[:fw], wo_a=wo[fw:])
        else:
            w_dt = ssd_w_in[j][:, d_inner + conv_dim:].reshape(d, 2, SSM_N_GROUPS, hpg)
            w_dt = w_dt.transpose(0, 2, 1, 3).reshape(d, 2 * n_heads)
            w_bf = ssd_w_in[j].astype(BF16)
            xw, zw = conv_dim // IN1_STEPS, d_inner // IN1_STEPS
            w_steps = jnp.concatenate(
                [w_bf[:, d_inner:d_inner + conv_dim].reshape(d, IN1_STEPS, xw),
                 w_bf[:, :d_inner].reshape(d, IN1_STEPS, zw)], axis=2).reshape(d, conv_dim + d_inner)
            lw.update(w_in=w_steps, w_dt=w_dt.astype(BF16), dt_bias=regroup(ssd_dt_bias[j], pad=False),
                      a_pad=regroup(-jnp.exp(ssd_a_log[j].astype(F32)), pad=True), cw=ssd_conv_w[j].astype(F32),
                      cb=vec(ssd_conv_b[j]), d_exp=jnp.repeat(ssd_d[j].astype(F32), SSM_HEAD_DIM).reshape(1, d_inner),
                      norm_g=vec(ssd_norm_g[j]), w_out=ssd_w_out[j].astype(BF16))
        layers.append(lw)

    def trunk(xin):
        batch, seq, _ = xin.shape
        assert seq % tm == 0
        t = batch * seq
        x = xin.reshape(t, d)
        cosf, sinf = _rope_tables(seq)
        h = None
        for i, lw in enumerate(layers):
            if i % 2 == 0:
                u, q, k, v = _inproj0(x, lw["mix_pre"], lw["w_in"], lw["qg"], lw["kg"], cosf, sinf,
                                      seq=seq, fw=fw, aw=aw, kvw=kvw, tm=tm)
                a = _attention(q.reshape(batch, seq, aw), k.reshape(batch, seq, kvw), v.reshape(batch, seq, kvw),
                               tq=_pick(seq, ATTN_Q_TILE), tk=_pick(seq, ATTN_K_TILE)).reshape(t, aw)
                f = _fourier_mix(u, batch=batch, seq=seq)
                x, h = _proj_res(f, a, lw["wo_f"], lw["wo_a"], x, lw["mix_post"], lw["ffn_pre"], tm=tm)
            else:
                z, xbc = _inproj1(h, lw["w_in"], lw["cw"], lw["cb"], seq=seq, tm=tm,
                                  steps=IN1_STEPS, sub=IN1_SUB, d_inner=d_inner, conv_dim=conv_dim)
                dt = _dt_proj(h, lw["w_dt"], lw["dt_bias"], tm=tm, groups=SSM_N_GROUPS, per_group=2 * hpg)
                yf, yb = _ssd(xbc, dt, lw["a_pad"], batch=batch, seq=seq, d_inner=d_inner,
                              cps=SSD_CHUNKS_PER_STEP)
                x, h = _ssd_out(yf, yb, xbc, z, lw["d_exp"], lw["norm_g"], lw["w_out"], x,
                                lw["mix_post"], lw["ffn_pre"], tm=SSD_OUT_TILE, tk=d_inner)
            g_next = layers[i + 1]["mix_pre"] if i + 1 < depth else None
            x, h = _ffn(h, lw["w_up"], lw["ffn_cw"], lw["ffn_cb"], lw["w_down"], x, lw["ffn_post"], g_next,
                        seq=seq, tm=tm, tn=_pick(d_ff, FFN_CHUNK))
        return x.reshape(batch, seq, d)

    return (trunk(x_prompt), trunk(x_sample))
```

```python
import functools
import math

import jax
import jax.numpy as jnp
import numpy as np
from jax import lax
from jax.experimental import pallas as pl
from jax.experimental.pallas import tpu as pltpu

F32 = jnp.float32
BF16 = jnp.bfloat16

EPS = 1e-6
GRID_W = 64
HEAD_DIM = 128
N_Q_HEADS = 8
N_KV_HEADS = 2
FNET_GROUP_DIM = 128
ROPE_THETA = 10000.0
SSM_HEAD_DIM = 64
SSM_N_GROUPS = 8
SSM_D_STATE = 128
SSM_CHUNK = 128
LANES = 128
HALO = 16
OUT_SUB = 256
IN1_STEPS = 4
IN1_SUB = 512
TOKEN_TILE = 512
IN1_TILE = 1024
SSD_OUT_TILE = 256
ATTN_Q_TILE = 1024
ATTN_K_TILE = 2048
FFN_CHUNK = 512
SSD_CHUNKS_PER_STEP = 16
VMEM_LIMIT = 56 * 1024 * 1024

_NT = (((1,), (1,)), ((), ()))


def _params(*sem):
    return pltpu.CompilerParams(dimension_semantics=sem, vmem_limit_bytes=VMEM_LIMIT)


def _resident(shape):
    nd = len(shape)
    return pl.BlockSpec(shape, lambda *_: (0,) * nd, pipeline_mode=pl.Buffered(1))


def _rms(x, g):
    return x * lax.rsqrt(jnp.mean(x * x, axis=-1, keepdims=True) + EPS) * g


def _dot(a, b):
    return jnp.dot(a, b, preferred_element_type=F32)


def _inproj0_kernel(x_ref, g_ref, w_ref, qg_ref, kg_ref, cos_ref, sin_ref,
                    u_ref, q_ref, k_ref, v_ref, *, fw, aw, kvw, scale):
    h = _rms(x_ref[...], g_ref[...]).astype(BF16)
    cosf = cos_ref[...]
    sinf = sin_ref[...]

    def norm_rope(t, gain, mult):
        t = _rms(t, gain)
        t = t * cosf + pltpu.roll(t, HEAD_DIM // 2, 1) * sinf
        return (t * mult).astype(BF16)

    q = _dot(h, w_ref[:, fw:fw + aw])
    k = _dot(h, w_ref[:, fw + aw:fw + aw + kvw])
    v = _dot(h, w_ref[:, fw + aw + kvw:])
    u = _dot(h, w_ref[:, :fw])
    for hh in range(aw // HEAD_DIM):
        sl = slice(hh * HEAD_DIM, (hh + 1) * HEAD_DIM)
        q_ref[:, sl] = norm_rope(q[:, sl], qg_ref[...], scale)
    for hh in range(kvw // HEAD_DIM):
        sl = slice(hh * HEAD_DIM, (hh + 1) * HEAD_DIM)
        k_ref[:, sl] = norm_rope(k[:, sl], kg_ref[...], 1.0)
    v_ref[...] = v.astype(BF16)
    u_ref[...] = u.astype(BF16)


def _inproj0(x, g, w, qg, kg, cosf, sinf, *, seq, fw, aw, kvw, tm):
    t, d = x.shape
    n = w.shape[1]
    tps = seq // tm
    kern = functools.partial(_inproj0_kernel, fw=fw, aw=aw, kvw=kvw, scale=math.log2(math.e) * HEAD_DIM ** -0.5)
    row = lambda width: pl.BlockSpec((tm, width), lambda i: (i, 0))
    return pl.pallas_call(
        kern,
        grid=(t // tm,),
        in_specs=[row(d), _resident((1, d)), _resident((d, n)),
                  _resident((1, HEAD_DIM)), _resident((1, HEAD_DIM)),
                  pl.BlockSpec((tm, HEAD_DIM), lambda i: (i % tps, 0)),
                  pl.BlockSpec((tm, HEAD_DIM), lambda i: (i % tps, 0))],
        out_specs=[row(fw), row(aw), row(kvw), row(kvw)],
        out_shape=[jax.ShapeDtypeStruct((t, fw), BF16), jax.ShapeDtypeStruct((t, aw), BF16),
                   jax.ShapeDtypeStruct((t, kvw), BF16), jax.ShapeDtypeStruct((t, kvw), BF16)],
        compiler_params=_params("parallel"),
        name="inproj0",
    )(x, g, w, qg, kg, cosf, sinf)


def _attn_kernel(q_ref, k_ref, v_ref, o_ref, m_sc, acc_sc, *, group):
    ki = pl.program_id(3)

    @pl.when(ki == 0)
    def _():
        m_sc[...] = jnp.full(m_sc.shape, -jnp.inf, F32)
        acc_sc[...] = jnp.zeros(acc_sc.shape, F32)

    k = k_ref[0]
    v = v_ref[0]
    v_ext = jnp.concatenate([v, jnp.ones_like(v)], axis=1)
    heads = range(group)
    s = [lax.dot_general(q_ref[0, :, j * HEAD_DIM:(j + 1) * HEAD_DIM], k, _NT,
                         preferred_element_type=F32).astype(BF16) for j in heads]
    p, alpha = [], []
    for j in heads:
        m_prev = m_sc[j]
        m_new = jnp.maximum(m_prev, jnp.max(s[j], axis=-1, keepdims=True).astype(F32))
        alpha.append(jnp.exp2(m_prev - m_new))
        p.append(jnp.exp2(s[j] - m_new.astype(BF16)))
        m_sc[j] = m_new
    for j in heads:
        acc_sc[j] = alpha[j] * acc_sc[j] + _dot(p[j], v_ext)

    @pl.when(ki == pl.num_programs(3) - 1)
    def _():
        for j in heads:
            acc = acc_sc[j]
            o_ref[0, :, j * HEAD_DIM:(j + 1) * HEAD_DIM] = (acc[:, :HEAD_DIM] / acc[:, HEAD_DIM:]).astype(BF16)


def _attention(q, k, v, *, tq, tk):
    b, s, aw = q.shape
    group = aw // HEAD_DIM // N_KV_HEADS
    gw = group * HEAD_DIM
    return pl.pallas_call(
        functools.partial(_attn_kernel, group=group),
        grid=(b, N_KV_HEADS, s // tq, s // tk),
        in_specs=[pl.BlockSpec((1, tq, gw), lambda bi, h, qi, ki: (bi, qi, h)),
                  pl.BlockSpec((1, tk, HEAD_DIM), lambda bi, h, qi, ki: (bi, ki, h)),
                  pl.BlockSpec((1, tk, HEAD_DIM), lambda bi, h, qi, ki: (bi, ki, h))],
        out_specs=pl.BlockSpec((1, tq, gw), lambda bi, h, qi, ki: (bi, qi, h)),
        out_shape=jax.ShapeDtypeStruct((b, s, aw), BF16),
        scratch_shapes=[pltpu.VMEM((group, tq, 1), F32), pltpu.VMEM((group, tq, 2 * HEAD_DIM), F32)],
        compiler_params=_params("parallel", "parallel", "parallel", "arbitrary"),
        name="attention",
    )(q, k, v)


def _fourier_tables(seq):
    n2 = 128
    n1 = seq // n2
    a1 = 2.0 * np.pi * np.outer(np.arange(n1), np.arange(n1)) / n1
    f1 = np.concatenate([np.cos(a1), -np.sin(a1)], axis=0)
    kk = np.arange(n1)[:, None, None] + n1 * np.arange(n2)[None, :, None]
    th = 2.0 * np.pi * (kk * np.arange(n2)[None, None, :] % seq) / seq
    mc, ms = np.cos(th), np.sin(th)
    m = np.concatenate([np.concatenate([mc, ms], axis=2),
                        np.concatenate([-ms, mc], axis=2)], axis=1)
    ac = 2.0 * np.pi * np.outer(np.arange(FNET_GROUP_DIM), np.arange(FNET_GROUP_DIM)) / FNET_GROUP_DIM
    norm = 1.0 / math.sqrt(seq * FNET_GROUP_DIM)
    return (jnp.asarray(f1, BF16), jnp.asarray(m, BF16),
            jnp.asarray(np.cos(ac) * norm, BF16), jnp.asarray(np.sin(ac) * norm, BF16))


def _fourier_group_kernel(u_ref, f1_ref, m_ref, cc_ref, sc_ref, o_ref, *, n1, n2):
    x = pltpu.einshape("abc->bac", u_ref[0])
    f1 = f1_ref[...]
    y = jnp.stack([_dot(f1, x[i]).astype(BF16) for i in range(n2)], axis=0)
    y = pltpu.einshape("abc->bac", y)
    gs = [_dot(m_ref[k1], jnp.concatenate([y[k1], y[n1 + k1]], axis=0)).astype(BF16)
          for k1 in range(n1)]
    outs = [(_dot(g[:n2], cc_ref[...]) + _dot(g[n2:], sc_ref[...])).astype(BF16) for g in gs]
    o = pltpu.einshape("abc->bac", jnp.stack(outs, axis=0))
    o_ref[0] = o.reshape(n1 * n2, FNET_GROUP_DIM)


def _fourier_mix(u, *, batch, seq):
    t, width = u.shape
    n2 = 128
    n1 = seq // n2
    groups = width // FNET_GROUP_DIM
    f1, m, cc, sc = _fourier_tables(seq)
    out = pl.pallas_call(
        functools.partial(_fourier_group_kernel, n1=n1, n2=n2),
        grid=(batch, groups),
        in_specs=[pl.BlockSpec((1, n1, n2, FNET_GROUP_DIM), lambda b, g: (b, 0, 0, g)),
                  _resident((2 * n1, n1)), _resident((n1, 2 * n2, 2 * n2)),
                  _resident((FNET_GROUP_DIM, FNET_GROUP_DIM)), _resident((FNET_GROUP_DIM, FNET_GROUP_DIM))],
        out_specs=pl.BlockSpec((1, seq, FNET_GROUP_DIM), lambda b, g: (b, 0, g)),
        out_shape=jax.ShapeDtypeStruct((batch, seq, width), BF16),
        compiler_params=_params("parallel", "parallel"),
        name="fourier",
    )(u.reshape(batch, n1, n2, width), f1, m, cc, sc)
    return out.reshape(t, width)


def _proj_res_kernel(f_ref, a_ref, wf_ref, wa_ref, x_ref, gp_ref, gn_ref, xo_ref, ho_ref):
    half = x_ref.shape[0] // 2
    halves = [slice(0, half), slice(half, 2 * half)]
    ms = [_dot(f_ref[r, :], wf_ref[...]) + _dot(a_ref[r, :], wa_ref[...]) for r in halves]
    for r, m in zip(halves, ms):
        xn = x_ref[r, :] + _rms(m, gp_ref[...])
        xo_ref[r, :] = xn
        ho_ref[r, :] = _rms(xn, gn_ref[...]).astype(BF16)


def _proj_res(f, a, wf, wa, x, g_post, g_next, *, tm):
    t, d = x.shape
    row = pl.BlockSpec((tm, d), lambda i: (i, 0))
    lhs = lambda arr: pl.BlockSpec((tm, arr.shape[1]), lambda i: (i, 0))
    return pl.pallas_call(
        _proj_res_kernel,
        grid=(t // tm,),
        in_specs=[lhs(f), lhs(a), _resident(wf.shape), _resident(wa.shape), row,
                  _resident((1, d)), _resident((1, d))],
        out_specs=[row, row],
        out_shape=[jax.ShapeDtypeStruct((t, d), F32), jax.ShapeDtypeStruct((t, d), BF16)],
        compiler_params=_params("parallel"),
        name="proj_res",
    )(f, a, wf, wa, x, g_post, g_next)


def _gelu_tanh(x):
    return 0.5 * x * (1.0 + jnp.tanh(math.sqrt(2.0 / math.pi) * (x + 0.044715 * x * x * x)))


def _silu(x):
    return x * (1.0 / (1.0 + jnp.exp(-x)))


def _assemble_halo(hext, h_ref, hp_ref, hn_ref, *, tm, tps):
    i = pl.program_id(0)
    first = (i % tps) == 0
    last = (i % tps) == tps - 1
    prev_blk = jnp.where(first, jnp.zeros_like(hp_ref[...]), hp_ref[...])
    next_blk = jnp.where(last, jnp.zeros_like(hn_ref[...]), hn_ref[...])
    row = lax.broadcasted_iota(jnp.int32, prev_blk.shape, 0)
    hext[0:tm] = h_ref[...]
    hext[tm:] = jnp.where(row < HALO // 2, next_blk, prev_blk)


def _conv_rows(u, cw, cb, *, tm):
    rows = tm + HALO
    c = cb + cw[0:1] * pltpu.roll(u, 1, 0) + cw[1:2] * u + cw[2:3] * pltpu.roll(u, rows - 1, 0)
    return c[0:tm]


def _halo_specs(t, tm, kdim):
    hb = tm // HALO
    last_blk = t // HALO - 1
    return [pl.BlockSpec((tm, kdim), lambda i, j: (i, 0)),
            pl.BlockSpec((HALO, kdim), lambda i, j: (jnp.maximum(i * hb - 1, 0), 0)),
            pl.BlockSpec((HALO, kdim), lambda i, j: (jnp.minimum((i + 1) * hb, last_blk), 0))]


def _inproj1_kernel(h_ref, hp_ref, hn_ref, w_ref, cw_ref, cb_ref, z_ref, xbc_ref, hext, *, tm, tps, sub):
    xw = xbc_ref.shape[1]
    zw = z_ref.shape[1]

    @pl.when(pl.program_id(1) == 0)
    def _():
        _assemble_halo(hext, h_ref, hp_ref, hn_ref, tm=tm, tps=tps)

    hx = hext[...]
    h = hext[0:tm, :]
    ux = [_dot(hx, w_ref[:, c0:c0 + sub]) for c0 in range(0, xw, sub)]
    uz = [_dot(h, w_ref[:, xw + c0:xw + c0 + sub]) for c0 in range(0, zw, sub)]
    for k, u in enumerate(ux):
        cols = slice(k * sub, (k + 1) * sub)
        y = _conv_rows(u, cw_ref[:, cols], cb_ref[:, cols], tm=tm)
        xbc_ref[:, cols] = _silu(y).astype(BF16)
    for k, u in enumerate(uz):
        z_ref[:, k * sub:(k + 1) * sub] = u.astype(BF16)


def _inproj1(h, w_steps, conv_w, conv_b, *, seq, tm, steps, sub, d_inner, conv_dim):
    t, kdim = h.shape
    xw, zw = conv_dim // steps, d_inner // steps
    in_specs = _halo_specs(t, tm, kdim) + [
        pl.BlockSpec((kdim, xw + zw), lambda i, j: (0, j)),
        pl.BlockSpec((3, xw), lambda i, j: (0, j)),
        pl.BlockSpec((1, xw), lambda i, j: (0, j))]
    return pl.pallas_call(
        functools.partial(_inproj1_kernel, tm=tm, tps=seq // tm, sub=sub),
        grid=(t // tm, steps),
        in_specs=in_specs,
        out_specs=[pl.BlockSpec((tm, zw), lambda i, j: (i, j)), pl.BlockSpec((tm, xw), lambda i, j: (i, j))],
        out_shape=[jax.ShapeDtypeStruct((t, d_inner), BF16), jax.ShapeDtypeStruct((t, conv_dim), BF16)],
        scratch_shapes=[pltpu.VMEM((tm + HALO, kdim), BF16)],
        compiler_params=_params("parallel", "arbitrary"),
        name="inproj1",
    )(h, h, h, w_steps, conv_w, conv_b)


def _ffn_kernel(*refs, tm, tps, nj, has_next):
    (h_ref, hp_ref, hn_ref, wg_ref, wv_ref, cwg_ref, cwv_ref, cbg_ref, cbv_ref,
     wd_ref, wdl_ref, x_ref, gp_ref) = refs[:13]
    pos = 13
    gn_ref = refs[pos] if has_next else None
    pos += int(has_next)
    xo_ref = refs[pos]
    ho_ref = refs[pos + 1] if has_next else None
    hext, act_a, act_b, acc = refs[-4:]
    j = pl.program_id(1)

    def step(act_prev, act_cur):
        hx = hext[...]
        ug = _dot(hx, wg_ref[...])
        uv = _dot(hx, wv_ref[...])
        if act_prev is None:
            acc[...] = jnp.zeros(acc.shape, F32)
        else:
            acc[...] += _dot(act_prev[...], wd_ref[...])
        gate = _conv_rows(ug, cwg_ref[...], cbg_ref[...], tm=tm)
        val = _conv_rows(uv, cwv_ref[...], cbv_ref[...], tm=tm)
        act_cur[...] = (_gelu_tanh(gate) * val).astype(BF16)

    @pl.when(j == 0)
    def _():
        _assemble_halo(hext, h_ref, hp_ref, hn_ref, tm=tm, tps=tps)
        step(None, act_a)

    @pl.when((j > 0) & (j % 2 == 0))
    def _():
        step(act_b, act_a)

    @pl.when(j % 2 == 1)
    def _():
        step(act_a, act_b)

    @pl.when(j == nj - 1)
    def _():
        act_last = act_a if (nj - 1) % 2 == 0 else act_b
        m = acc[...] + _dot(act_last[...], wdl_ref[...])
        xn = x_ref[...] + _rms(m, gp_ref[...])
        xo_ref[...] = xn
        if has_next:
            ho_ref[...] = _rms(xn, gn_ref[...]).astype(BF16)


def _ffn(h, w_up, conv_w, conv_b, w_down, x, g_post, g_next, *, seq, tm, tn):
    t, d = x.shape
    d_ff = w_down.shape[0]
    nj = d_ff // tn
    has_next = g_next is not None
    row = pl.BlockSpec((tm, d), lambda i, j: (i, 0))
    in_specs = _halo_specs(t, tm, d) + [
        pl.BlockSpec((d, tn), lambda i, j: (0, j)),
        pl.BlockSpec((d, tn), lambda i, j: (0, nj + j)),
        pl.BlockSpec((3, tn), lambda i, j: (0, j)),
        pl.BlockSpec((3, tn), lambda i, j: (0, nj + j)),
        pl.BlockSpec((1, tn), lambda i, j: (0, j)),
        pl.BlockSpec((1, tn), lambda i, j: (0, nj + j)),
        pl.BlockSpec((tn, d), lambda i, j: (jnp.maximum(j - 1, 0), 0)),
        pl.BlockSpec((tn, d), lambda i, j: (nj - 1, 0), pipeline_mode=pl.Buffered(1)),
        row, _resident((1, d))]
    args = [h, h, h, w_up, w_up, conv_w, conv_w, conv_b, conv_b, w_down, w_down, x, g_post]
    out_specs = [row]
    out_shape = [jax.ShapeDtypeStruct((t, d), F32)]
    if has_next:
        in_specs.append(_resident((1, d)))
        args.append(g_next)
        out_specs.append(row)
        out_shape.append(jax.ShapeDtypeStruct((t, d), BF16))
    res = pl.pallas_call(
        functools.partial(_ffn_kernel, tm=tm, tps=seq // tm, nj=nj, has_next=has_next),
        grid=(t // tm, nj),
        in_specs=in_specs, out_specs=out_specs, out_shape=out_shape,
        scratch_shapes=[pltpu.VMEM((tm + HALO, d), BF16), pltpu.VMEM((tm, tn), BF16),
                        pltpu.VMEM((tm, tn), BF16), pltpu.VMEM((tm, d), F32)],
        compiler_params=_params("parallel", "arbitrary"),
        name="ffn",
    )(*args)
    return res if has_next else (res[0], None)


def _dt_kernel(h_ref, w_ref, b_ref, o_ref, *, per_group):
    r = _dot(h_ref[...], w_ref[...]) + b_ref[...]
    sp = jnp.maximum(r, 0.0) + jnp.log(1.0 + jnp.exp(-jnp.abs(r)))
    for g in range(o_ref.shape[1] // LANES):
        o_ref[:, g * LANES:(g + 1) * LANES] = sp if g == 0 else pltpu.roll(sp, LANES - g * per_group, 1)


def _dt_proj(h, w, b, *, tm, groups, per_group):
    t, kdim = h.shape
    assert w.shape[1] == LANES == groups * per_group
    return pl.pallas_call(
        functools.partial(_dt_kernel, per_group=per_group),
        grid=(t // tm,),
        in_specs=[pl.BlockSpec((tm, kdim), lambda i: (i, 0)), _resident((kdim, LANES)), _resident((1, LANES))],
        out_specs=pl.BlockSpec((tm, groups * LANES), lambda i: (i, 0)),
        out_shape=jax.ShapeDtypeStruct((t, groups * LANES), F32),
        compiler_params=_params("parallel"),
        name="ssd_dt",
    )(h, w, b)


def _split3(x):
    hi = x.astype(BF16)
    r = x - hi.astype(F32)
    mid = r.astype(BF16)
    lo = (r - mid.astype(F32)).astype(BF16)
    return hi, mid, lo


def _cumsum_rows(tri, x):
    hi, mid, lo = _split3(x)
    return _dot(tri, hi) + _dot(tri, mid) + _dot(tri, lo)


def _expand(x, e):
    return _dot(x.astype(BF16), e)


class _Chunk:
    pass


def _ssd_positions(c, x_ref, b_ref, c_ref, dt_ref, a_ref, tri_ref):
    rows = slice(c.r0, c.r0 + SSM_CHUNK)
    c.x = x_ref[rows, :]
    c.bm = b_ref[rows, :]
    c.cm = c_ref[rows, :]
    c.dt = dt_ref[rows, :]
    c.dta = c.dt * a_ref[...]
    c.incl = _cumsum_rows(tri_ref[...], c.dta)


def _ssd_weights(c, e_ref):
    L = SSM_CHUNK
    total = c.incl[L - 1:L, :]
    if c.backward:
        c.pos = c.incl - c.dta
        w_state = c.dt * jnp.exp(c.pos)
        w_out = jnp.exp(total - c.pos)
    else:
        c.pos = c.incl
        w_state = c.dt * jnp.exp(total - c.pos)
        w_out = jnp.exp(c.pos)
    e = e_ref[...]
    c.w_out_x = _expand(w_out, e)
    c.carry = c.w_out_x[0:1] if c.backward else c.w_out_x[L - 1:L]
    c.xs = (c.x.astype(F32) * _expand(w_state, e)).astype(BF16)
    c.b_t = c.bm.astype(F32).T.astype(BF16)
    c.cb = lax.dot_general(c.cm, c.bm, _NT, preferred_element_type=F32)
    c.pos_t = c.pos.T
    c.dt_t = c.dt.T


def _ssd_state(c, h_in):
    c.y = _dot(c.cm, h_in.astype(BF16)) * c.w_out_x
    return h_in * c.carry + _dot(c.b_t, c.xs)


def _ssd_diag(c, y_ref, *, hpg):
    L = SSM_CHUNK
    rows = slice(c.r0, c.r0 + L)
    li = lax.broadcasted_iota(jnp.int32, (L, L), 0)
    si = lax.broadcasted_iota(jnp.int32, (L, L), 1)
    mask = (si >= li) if c.backward else (li >= si)
    lane = lax.broadcasted_iota(jnp.int32, (L, 2 * SSM_HEAD_DIM), 1)
    for jp in range(hpg // 2):
        sl = slice(jp * 2 * SSM_HEAD_DIM, (jp + 1) * 2 * SSM_HEAD_DIM)
        xp = c.x[:, sl]
        outs = []
        for j in (c.lane0 + 2 * jp, c.lane0 + 2 * jp + 1):
            col = c.pos[:, j:j + 1]
            row = c.pos_t[j:j + 1, :]
            d = (row - col) if c.backward else (col - row)
            w = c.cb * jnp.exp(jnp.where(mask, d, -1e30)) * c.dt_t[j:j + 1, :]
            outs.append(_dot(w.astype(BF16), xp))
        y_ref[rows, sl] = (c.y[:, sl] + jnp.where(lane < SSM_HEAD_DIM, outs[0], outs[1])).astype(y_ref.dtype)


def _ssd_kernel(xf, bf, cf, dtf, xb, bb, cb, dtb, a_ref, tri_ref, ef_ref, eb_ref,
                yf_ref, yb_ref, hf_sc, hb_sc, *, hpg, cps):
    @pl.when(pl.program_id(2) == 0)
    def _():
        hf_sc[...] = jnp.zeros(hf_sc.shape, F32)
        hb_sc[...] = jnp.zeros(hb_sc.shape, F32)

    fwd, bwd = [], []
    for k in range(cps):
        f, b = _Chunk(), _Chunk()
        f.r0, f.lane0, f.backward = k * SSM_CHUNK, 0, False
        b.r0, b.lane0, b.backward = (cps - 1 - k) * SSM_CHUNK, hpg, True
        fwd.append(f)
        bwd.append(b)
    for f, b in zip(fwd, bwd):
        _ssd_positions(f, xf, bf, cf, dtf, a_ref, tri_ref)
        _ssd_positions(b, xb, bb, cb, dtb, a_ref, tri_ref)
    for f, b in zip(fwd, bwd):
        _ssd_weights(f, ef_ref)
        _ssd_weights(b, eb_ref)
    hf = hf_sc[...]
    hb = hb_sc[...]
    for f, b in zip(fwd, bwd):
        hf = _ssd_state(f, hf)
        hb = _ssd_state(b, hb)
    hf_sc[...] = hf
    hb_sc[...] = hb
    for f, b in zip(fwd, bwd):
        _ssd_diag(f, yf_ref, hpg=hpg)
        _ssd_diag(b, yb_ref, hpg=hpg)


def _ssd(xbc, dt, a_pad, *, batch, seq, d_inner, cps):
    t = xbc.shape[0]
    L = SSM_CHUNK
    rows = cps * L
    assert seq % rows == 0
    nc = seq // rows
    gw = d_inner // SSM_N_GROUPS
    hpg = gw // SSM_HEAD_DIM
    b0 = d_inner // SSM_D_STATE
    c0 = b0 + SSM_N_GROUPS
    tri = jnp.asarray(np.tril(np.ones((L, L))), BF16)
    head_of_lane = np.arange(gw) // SSM_HEAD_DIM
    ef = jnp.asarray(np.arange(LANES)[:, None] == head_of_lane[None, :], BF16)
    eb = jnp.asarray(np.arange(LANES)[:, None] == (head_of_lane[None, :] + hpg), BF16)

    def fwd(col):
        return lambda b, g, c: (b * nc + c, col(g))

    def bwd(col):
        return lambda b, g, c: (b * nc + (nc - 1 - c), col(g))

    def specs(order):
        return [pl.BlockSpec((rows, gw), order(lambda g: g)),
                pl.BlockSpec((rows, SSM_D_STATE), order(lambda g: b0 + g)),
                pl.BlockSpec((rows, SSM_D_STATE), order(lambda g: c0 + g)),
                pl.BlockSpec((rows, LANES), order(lambda g: g))]

    in_specs = specs(fwd) + specs(bwd) + [
        pl.BlockSpec((1, LANES), lambda b, g, c: (0, g)),
        _resident((L, L)), _resident((LANES, gw)), _resident((LANES, gw))]
    return pl.pallas_call(
        functools.partial(_ssd_kernel, hpg=hpg, cps=cps),
        grid=(batch, SSM_N_GROUPS, nc),
        in_specs=in_specs,
        out_specs=[pl.BlockSpec((rows, gw), fwd(lambda g: g)), pl.BlockSpec((rows, gw), bwd(lambda g: g))],
        out_shape=[jax.ShapeDtypeStruct((t, d_inner), BF16), jax.ShapeDtypeStruct((t, d_inner), BF16)],
        scratch_shapes=[pltpu.VMEM((SSM_D_STATE, gw), F32), pltpu.VMEM((SSM_D_STATE, gw), F32)],
        compiler_params=_params("parallel", "parallel", "arbitrary"),
        name="ssd_scan",
    )(xbc, xbc, xbc, dt, xbc, xbc, xbc, dt, a_pad, tri, ef, eb)


def _ssd_out_kernel(yf_ref, yb_ref, xs_ref, z_ref, dsk_ref, ng_ref, w_ref, x_ref, gp_ref, gn_ref,
                    xo_ref, ho_ref, acc, ss, *, d_inner):
    kk = pl.program_id(1)

    @pl.when(kk == 0)
    def _():
        acc[...] = jnp.zeros(acc.shape, F32)
        ss[...] = jnp.zeros(ss.shape, F32)

    tk = w_ref.shape[0]
    sq = None
    part = None
    for c0 in range(0, tk, OUT_SUB):
        cols = slice(c0, c0 + OUT_SUB)
        y = (yf_ref[:, cols].astype(F32) + yb_ref[:, cols].astype(F32)
             + xs_ref[:, cols].astype(F32) * dsk_ref[:, cols])
        y = y * _silu(z_ref[:, cols].astype(F32))
        y2 = y * y
        for l0 in range(0, OUT_SUB, LANES):
            sq = y2[:, l0:l0 + LANES] if sq is None else sq + y2[:, l0:l0 + LANES]
        d = _dot((y * ng_ref[:, cols]).astype(BF16), w_ref[cols, :])
        part = d if part is None else part + d
    acc[...] += part
    ss[...] += sq

    @pl.when(kk == pl.num_programs(1) - 1)
    def _():
        ssum = jnp.sum(ss[...], axis=-1, keepdims=True)
        m = acc[...] * lax.rsqrt(ssum * (1.0 / d_inner) + EPS)
        xn = x_ref[...] + _rms(m, gp_ref[...])
        xo_ref[...] = xn
        ho_ref[...] = _rms(xn, gn_ref[...]).astype(BF16)


def _ssd_out(yf, yb, xbc, z, d_exp, norm_g, w, x, g_post, g_next, *, tm, tk):
    t, d = x.shape
    d_inner = yf.shape[1]
    lhs = pl.BlockSpec((tm, tk), lambda i, k: (i, k))
    vec = pl.BlockSpec((1, tk), lambda i, k: (0, k))
    row = pl.BlockSpec((tm, d), lambda i, k: (i, 0))
    w_spec = _resident((tk, d)) if tk == d_inner else pl.BlockSpec((tk, d), lambda i, k: (k, 0))
    return pl.pallas_call(
        functools.partial(_ssd_out_kernel, d_inner=d_inner),
        grid=(t // tm, d_inner // tk),
        in_specs=[lhs, lhs, lhs, lhs, vec, vec, w_spec, row, _resident((1, d)), _resident((1, d))],
        out_specs=[row, row],
        out_shape=[jax.ShapeDtypeStruct((t, d), F32), jax.ShapeDtypeStruct((t, d), BF16)],
        scratch_shapes=[pltpu.VMEM((tm, d), F32), pltpu.VMEM((tm, LANES), F32)],
        compiler_params=_params("parallel", "arbitrary"),
        name="ssd_out",
    )(yf, yb, xbc, z, d_exp, norm_g, w, x, g_post, g_next)


def _rope_tables(seq):
    rows = seq // GRID_W
    row = jnp.repeat(jnp.arange(rows, dtype=F32), GRID_W)
    col = jnp.tile(jnp.arange(GRID_W, dtype=F32), rows)
    axis_dim = HEAD_DIM // 2
    inv_freq = ROPE_THETA ** (-jnp.arange(0, axis_dim, 2, dtype=F32) / axis_dim)
    ang = jnp.concatenate([row[:, None] * inv_freq, col[:, None] * inv_freq], axis=-1)
    cos, sin = jnp.cos(ang), jnp.sin(ang)
    return jnp.concatenate([cos, cos], axis=-1), jnp.concatenate([-sin, sin], axis=-1)


def _pick(n, pref):
    t = min(pref, n)
    while n % t:
        t -= LANES
    return t


def kernel(x_prompt, x_sample, mix_pre_g, mix_post_g, ffn_pre_g, ffn_post_g, fa_w_in, fa_q_gain, fa_k_gain, fa_w_out, ssd_w_in, ssd_conv_w, ssd_conv_b, ssd_dt_bias, ssd_a_log, ssd_d, ssd_norm_g, ssd_w_out, ffn_w_up, ffn_conv_w, ffn_conv_b, ffn_w_down):
    depth = mix_pre_g.shape[0]
    d = x_prompt.shape[2]
    aw = N_Q_HEADS * HEAD_DIM
    kvw = N_KV_HEADS * HEAD_DIM
    fw = fa_w_in.shape[2] - aw - 2 * kvw
    assert fa_w_out.shape[1] == fw + aw and fw % FNET_GROUP_DIM == 0
    d_ff = ffn_w_down.shape[1]
    d_inner = ssd_w_out.shape[1]
    n_heads = d_inner // SSM_HEAD_DIM
    hpg = n_heads // SSM_N_GROUPS
    bc_w = SSM_N_GROUPS * SSM_D_STATE
    conv_dim = d_inner + 2 * bc_w
    assert hpg % 2 == 0 and 2 * hpg <= LANES and depth % 2 == 0
    tm = TOKEN_TILE
    vec = lambda v: v.reshape(1, -1).astype(F32)

    def regroup(p, pad):
        p = p.astype(F32).reshape(2, SSM_N_GROUPS, hpg).transpose(1, 0, 2).reshape(SSM_N_GROUPS, 2 * hpg)
        if pad:
            p = jnp.pad(p, ((0, 0), (0, LANES - 2 * hpg)))
        return p.reshape(1, -1)

    layers = []
    for i in range(depth):
        j = i // 2
        lw = dict(ffn_pre=vec(ffn_pre_g[i]), ffn_post=vec(ffn_post_g[i]), mix_pre=vec(mix_pre_g[i]),
                  mix_post=vec(mix_post_g[i]), w_up=ffn_w_up[i].astype(BF16), w_down=ffn_w_down[i].astype(BF16),
                  ffn_cw=ffn_conv_w[i].astype(F32), ffn_cb=vec(ffn_conv_b[i]))
        if i % 2 == 0:
            wo = fa_w_out[j].astype(BF16)
            lw.update(w_in=fa_w_in[j].astype(BF16), qg=vec(fa_q_gain[j]), kg=vec(fa_k_gain[j]),
                      wo_f=wo[:fw], wo_a=wo[fw:])
        else:
            w_dt = ssd_w_in[j][:, d_inner + conv_dim:].reshape(d, 2, SSM_N_GROUPS, hpg)
            w_dt = w_dt.transpose(0, 2, 1, 3).reshape(d, 2 * n_heads)
            w_bf = ssd_w_in[j].astype(BF16)
            xw, zw = conv_dim // IN1_STEPS, d_inner // IN1_STEPS
            w_steps = jnp.concatenate(
                [w_bf[:, d_inner:d_inner + conv_dim].reshape(d, IN1_STEPS, xw),
                 w_bf[:, :d_inner].reshape(d, IN1_STEPS, zw)], axis=2).reshape(d, conv_dim + d_inner)
            lw.update(w_in=w_steps, w_dt=w_dt.astype(BF16), dt_bias=regroup(ssd_dt_bias[j], pad=False),
                      a_pad=regroup(-jnp.exp(ssd_a_log[j].astype(F32)), pad=True), cw=ssd_conv_w[j].astype(F32),
                      cb=vec(ssd_conv_b[j]), d_exp=jnp.repeat(ssd_d[j].astype(F32), SSM_HEAD_DIM).reshape(1, d_inner),
                      norm_g=vec(ssd_norm_g[j]), w_out=ssd_w_out[j].astype(BF16))
        layers.append(lw)

    def trunk(xin):
        batch, seq, _ = xin.shape
        assert seq % tm == 0
        t = batch * seq
        x = xin.reshape(t, d)
        cosf, sinf = _rope_tables(seq)
        h = None
        for i, lw in enumerate(layers):
            if i % 2 == 0:
                u, q, k, v = _inproj0(x, lw["mix_pre"], lw["w_in"], lw["qg"], lw["kg"], cosf, sinf,
                                      seq=seq, fw=fw, aw=aw, kvw=kvw, tm=tm)
                a = _attention(q.reshape(batch, seq, aw), k.reshape(batch, seq, kvw), v.reshape(batch, seq, kvw),
                               tq=_pick(seq, ATTN_Q_TILE), tk=_pick(seq, ATTN_K_TILE)).reshape(t, aw)
                f = _fourier_mix(u, batch=batch, seq=seq)
                x, h = _proj_res(f, a, lw["wo_f"], lw["wo_a"], x, lw["mix_post"], lw["ffn_pre"], tm=tm)
            else:
                z, xbc = _inproj1(h, lw["w_in"], lw["cw"], lw["cb"], seq=seq, tm=_pick(seq, IN1_TILE),
                                  steps=IN1_STEPS, sub=IN1_SUB, d_inner=d_inner, conv_dim=conv_dim)
                dt = _dt_proj(h, lw["w_dt"], lw["dt_bias"], tm=tm, groups=SSM_N_GROUPS, per_group=2 * hpg)
                yf, yb = _ssd(xbc, dt, lw["a_pad"], batch=batch, seq=seq, d_inner=d_inner,
                              cps=SSD_CHUNKS_PER_STEP)
                x, h = _ssd_out(yf, yb, xbc, z, lw["d_exp"], lw["norm_g"], lw["w_out"], x,
                                lw["mix_post"], lw["ffn_pre"], tm=SSD_OUT_TILE, tk=d_inner)
            g_next = layers[i + 1]["mix_pre"] if i + 1 < depth else None
            x, h = _ffn(h, lw["w_up"], lw["ffn_cw"], lw["ffn_cb"], lw["w_down"], x, lw["ffn_post"], g_next,
                        seq=seq, tm=tm, tn=_pick(d_ff, FFN_CHUNK))
        return x.reshape(batch, seq, d)

    return (trunk(x_prompt), trunk(x_sample))
```

```python
import functools
import math

import jax
import jax.numpy as jnp
import numpy as np
from jax import lax
from jax.experimental import pallas as pl
from jax.experimental.pallas import tpu as pltpu

F32 = jnp.float32
BF16 = jnp.bfloat16

EPS = 1e-6
GRID_W = 64
HEAD_DIM = 128
N_Q_HEADS = 8
N_KV_HEADS = 2
FNET_GROUP_DIM = 128
ROPE_THETA = 10000.0
SSM_HEAD_DIM = 64
SSM_N_GROUPS = 8
SSM_D_STATE = 128
SSM_CHUNK = 128
LANES = 128
HALO = 16
OUT_SUB = 256
IN1_STEPS = 4
IN1_SUB = 512
TOKEN_TILE = 512
IN1_TILE = 1024
IN0_TILE = 1024
SSD_OUT_TILE = 256
ATTN_Q_TILE = 1024
ATTN_K_TILE = 2048
FFN_CHUNK = 512
SSD_CHUNKS_PER_STEP = 16
VMEM_LIMIT = 56 * 1024 * 1024

_NT = (((1,), (1,)), ((), ()))


def _params(*sem):
    return pltpu.CompilerParams(dimension_semantics=sem, vmem_limit_bytes=VMEM_LIMIT)


def _resident(shape):
    nd = len(shape)
    return pl.BlockSpec(shape, lambda *_: (0,) * nd, pipeline_mode=pl.Buffered(1))


def _rms(x, g):
    return x * lax.rsqrt(jnp.mean(x * x, axis=-1, keepdims=True) + EPS) * g


def _dot(a, b):
    return jnp.dot(a, b, preferred_element_type=F32)


def _inproj0_kernel(x_ref, g_ref, w_ref, qg_ref, kg_ref, cos_ref, sin_ref,
                    u_ref, q_ref, k_ref, v_ref, *, fw, aw, kvw, scale):
    h = _rms(x_ref[...], g_ref[...]).astype(BF16)
    cosf = cos_ref[...]
    sinf = sin_ref[...]

    def norm_rope(t, gain, mult):
        t = _rms(t, gain)
        t = t * cosf + pltpu.roll(t, HEAD_DIM // 2, 1) * sinf
        return (t * mult).astype(BF16)

    q = _dot(h, w_ref[:, fw:fw + aw])
    k = _dot(h, w_ref[:, fw + aw:fw + aw + kvw])
    v = _dot(h, w_ref[:, fw + aw + kvw:])
    u = _dot(h, w_ref[:, :fw])
    for hh in range(aw // HEAD_DIM):
        sl = slice(hh * HEAD_DIM, (hh + 1) * HEAD_DIM)
        q_ref[:, sl] = norm_rope(q[:, sl], qg_ref[...], scale)
    for hh in range(kvw // HEAD_DIM):
        sl = slice(hh * HEAD_DIM, (hh + 1) * HEAD_DIM)
        k_ref[:, sl] = norm_rope(k[:, sl], kg_ref[...], 1.0)
    v_ref[...] = v.astype(BF16)
    u_ref[...] = u.astype(BF16)


def _inproj0(x, g, w, qg, kg, cosf, sinf, *, seq, fw, aw, kvw, tm):
    t, d = x.shape
    n = w.shape[1]
    tps = seq // tm
    kern = functools.partial(_inproj0_kernel, fw=fw, aw=aw, kvw=kvw, scale=math.log2(math.e) * HEAD_DIM ** -0.5)
    row = lambda width: pl.BlockSpec((tm, width), lambda i: (i, 0))
    return pl.pallas_call(
        kern,
        grid=(t // tm,),
        in_specs=[row(d), _resident((1, d)), _resident((d, n)),
                  _resident((1, HEAD_DIM)), _resident((1, HEAD_DIM)),
                  pl.BlockSpec((tm, HEAD_DIM), lambda i: (i % tps, 0)),
                  pl.BlockSpec((tm, HEAD_DIM), lambda i: (i % tps, 0))],
        out_specs=[row(fw), row(aw), row(kvw), row(kvw)],
        out_shape=[jax.ShapeDtypeStruct((t, fw), BF16), jax.ShapeDtypeStruct((t, aw), BF16),
                   jax.ShapeDtypeStruct((t, kvw), BF16), jax.ShapeDtypeStruct((t, kvw), BF16)],
        compiler_params=_params("parallel"),
        name="inproj0",
    )(x, g, w, qg, kg, cosf, sinf)


def _attn_kernel(q_ref, k_ref, v_ref, o_ref, m_sc, acc_sc, *, group):
    ki = pl.program_id(3)

    @pl.when(ki == 0)
    def _():
        m_sc[...] = jnp.full(m_sc.shape, -jnp.inf, F32)
        acc_sc[...] = jnp.zeros(acc_sc.shape, F32)

    k = k_ref[0]
    v = v_ref[0]
    v_ext = jnp.concatenate([v, jnp.ones_like(v)], axis=1)
    heads = range(group)
    s = [lax.dot_general(q_ref[0, :, j * HEAD_DIM:(j + 1) * HEAD_DIM], k, _NT,
                         preferred_element_type=F32).astype(BF16) for j in heads]
    p, alpha = [], []
    for j in heads:
        m_prev = m_sc[j]
        m_new = jnp.maximum(m_prev, jnp.max(s[j], axis=-1, keepdims=True).astype(F32))
        alpha.append(jnp.exp2(m_prev - m_new))
        p.append(jnp.exp2(s[j] - m_new.astype(BF16)))
        m_sc[j] = m_new
    for j in heads:
        acc_sc[j] = alpha[j] * acc_sc[j] + _dot(p[j], v_ext)

    @pl.when(ki == pl.num_programs(3) - 1)
    def _():
        for j in heads:
            acc = acc_sc[j]
            o_ref[0, :, j * HEAD_DIM:(j + 1) * HEAD_DIM] = (acc[:, :HEAD_DIM] / acc[:, HEAD_DIM:]).astype(BF16)


def _attention(q, k, v, *, tq, tk):
    b, s, aw = q.shape
    group = aw // HEAD_DIM // N_KV_HEADS
    gw = group * HEAD_DIM
    return pl.pallas_call(
        functools.partial(_attn_kernel, group=group),
        grid=(b, N_KV_HEADS, s // tq, s // tk),
        in_specs=[pl.BlockSpec((1, tq, gw), lambda bi, h, qi, ki: (bi, qi, h)),
                  pl.BlockSpec((1, tk, HEAD_DIM), lambda bi, h, qi, ki: (bi, ki, h)),
                  pl.BlockSpec((1, tk, HEAD_DIM), lambda bi, h, qi, ki: (bi, ki, h))],
        out_specs=pl.BlockSpec((1, tq, gw), lambda bi, h, qi, ki: (bi, qi, h)),
        out_shape=jax.ShapeDtypeStruct((b, s, aw), BF16),
        scratch_shapes=[pltpu.VMEM((group, tq, 1), F32), pltpu.VMEM((group, tq, 2 * HEAD_DIM), F32)],
        compiler_params=_params("parallel", "parallel", "parallel", "arbitrary"),
        name="attention",
    )(q, k, v)


def _fourier_tables(seq):
    n2 = 128
    n1 = seq // n2
    a1 = 2.0 * np.pi * np.outer(np.arange(n1), np.arange(n1)) / n1
    f1 = np.concatenate([np.cos(a1), -np.sin(a1)], axis=0)
    kk = np.arange(n1)[:, None, None] + n1 * np.arange(n2)[None, :, None]
    th = 2.0 * np.pi * (kk * np.arange(n2)[None, None, :] % seq) / seq
    mc, ms = np.cos(th), np.sin(th)
    m = np.concatenate([np.concatenate([mc, ms], axis=2),
                        np.concatenate([-ms, mc], axis=2)], axis=1)
    ac = 2.0 * np.pi * np.outer(np.arange(FNET_GROUP_DIM), np.arange(FNET_GROUP_DIM)) / FNET_GROUP_DIM
    norm = 1.0 / math.sqrt(seq * FNET_GROUP_DIM)
    return (jnp.asarray(f1, BF16), jnp.asarray(m, BF16),
            jnp.asarray(np.cos(ac) * norm, BF16), jnp.asarray(np.sin(ac) * norm, BF16))


def _fourier_group_kernel(u_ref, f1_ref, m_ref, cc_ref, sc_ref, o_ref, *, n1, n2):
    x = pltpu.einshape("abc->bac", u_ref[0])
    f1 = f1_ref[...]
    y = jnp.stack([_dot(f1, x[i]).astype(BF16) for i in range(n2)], axis=0)
    y = pltpu.einshape("abc->bac", y)
    gs = [_dot(m_ref[k1], jnp.concatenate([y[k1], y[n1 + k1]], axis=0)).astype(BF16)
          for k1 in range(n1)]
    outs = [(_dot(g[:n2], cc_ref[...]) + _dot(g[n2:], sc_ref[...])).astype(BF16) for g in gs]
    o = pltpu.einshape("abc->bac", jnp.stack(outs, axis=0))
    o_ref[0] = o.reshape(n1 * n2, FNET_GROUP_DIM)


def _fourier_mix(u, *, batch, seq):
    t, width = u.shape
    n2 = 128
    n1 = seq // n2
    groups = width // FNET_GROUP_DIM
    f1, m, cc, sc = _fourier_tables(seq)
    out = pl.pallas_call(
        functools.partial(_fourier_group_kernel, n1=n1, n2=n2),
        grid=(batch, groups),
        in_specs=[pl.BlockSpec((1, n1, n2, FNET_GROUP_DIM), lambda b, g: (b, 0, 0, g)),
                  _resident((2 * n1, n1)), _resident((n1, 2 * n2, 2 * n2)),
                  _resident((FNET_GROUP_DIM, FNET_GROUP_DIM)), _resident((FNET_GROUP_DIM, FNET_GROUP_DIM))],
        out_specs=pl.BlockSpec((1, seq, FNET_GROUP_DIM), lambda b, g: (b, 0, g)),
        out_shape=jax.ShapeDtypeStruct((batch, seq, width), BF16),
        compiler_params=_params("parallel", "parallel"),
        name="fourier",
    )(u.reshape(batch, n1, n2, width), f1, m, cc, sc)
    return out.reshape(t, width)


def _proj_res_kernel(f_ref, a_ref, wf_ref, wa_ref, x_ref, gp_ref, gn_ref, xo_ref, ho_ref):
    half = x_ref.shape[0] // 2
    halves = [slice(0, half), slice(half, 2 * half)]
    ms = [_dot(f_ref[r, :], wf_ref[...]) + _dot(a_ref[r, :], wa_ref[...]) for r in halves]
    for r, m in zip(halves, ms):
        xn = x_ref[r, :] + _rms(m, gp_ref[...])
        xo_ref[r, :] = xn
        ho_ref[r, :] = _rms(xn, gn_ref[...]).astype(BF16)


def _proj_res(f, a, wf, wa, x, g_post, g_next, *, tm):
    t, d = x.shape
    row = pl.BlockSpec((tm, d), lambda i: (i, 0))
    lhs = lambda arr: pl.BlockSpec((tm, arr.shape[1]), lambda i: (i, 0))
    return pl.pallas_call(
        _proj_res_kernel,
        grid=(t // tm,),
        in_specs=[lhs(f), lhs(a), _resident(wf.shape), _resident(wa.shape), row,
                  _resident((1, d)), _resident((1, d))],
        out_specs=[row, row],
        out_shape=[jax.ShapeDtypeStruct((t, d), F32), jax.ShapeDtypeStruct((t, d), BF16)],
        compiler_params=_params("parallel"),
        name="proj_res",
    )(f, a, wf, wa, x, g_post, g_next)


def _gelu_tanh(x):
    return 0.5 * x * (1.0 + jnp.tanh(math.sqrt(2.0 / math.pi) * (x + 0.044715 * x * x * x)))


def _silu(x):
    return x * (1.0 / (1.0 + jnp.exp(-x)))


def _assemble_halo(hext, h_ref, hp_ref, hn_ref, *, tm, tps):
    i = pl.program_id(0)
    first = (i % tps) == 0
    last = (i % tps) == tps - 1
    prev_blk = jnp.where(first, jnp.zeros_like(hp_ref[...]), hp_ref[...])
    next_blk = jnp.where(last, jnp.zeros_like(hn_ref[...]), hn_ref[...])
    row = lax.broadcasted_iota(jnp.int32, prev_blk.shape, 0)
    hext[0:tm] = h_ref[...]
    hext[tm:] = jnp.where(row < HALO // 2, next_blk, prev_blk)


def _conv_rows(u, cw, cb, *, tm):
    rows = tm + HALO
    c = cb + cw[0:1] * pltpu.roll(u, 1, 0) + cw[1:2] * u + cw[2:3] * pltpu.roll(u, rows - 1, 0)
    return c[0:tm]


def _halo_specs(t, tm, kdim):
    hb = tm // HALO
    last_blk = t // HALO - 1
    return [pl.BlockSpec((tm, kdim), lambda i, j: (i, 0)),
            pl.BlockSpec((HALO, kdim), lambda i, j: (jnp.maximum(i * hb - 1, 0), 0)),
            pl.BlockSpec((HALO, kdim), lambda i, j: (jnp.minimum((i + 1) * hb, last_blk), 0))]


def _inproj1_kernel(h_ref, hp_ref, hn_ref, w_ref, cw_ref, cb_ref, z_ref, xbc_ref, hext, *, tm, tps, sub):
    xw = xbc_ref.shape[1]
    zw = z_ref.shape[1]

    @pl.when(pl.program_id(1) == 0)
    def _():
        _assemble_halo(hext, h_ref, hp_ref, hn_ref, tm=tm, tps=tps)

    hx = hext[...]
    h = hext[0:tm, :]
    ux = [_dot(hx, w_ref[:, c0:c0 + sub]) for c0 in range(0, xw, sub)]
    uz = [_dot(h, w_ref[:, xw + c0:xw + c0 + sub]) for c0 in range(0, zw, sub)]
    for k, u in enumerate(ux):
        cols = slice(k * sub, (k + 1) * sub)
        y = _conv_rows(u, cw_ref[:, cols], cb_ref[:, cols], tm=tm)
        xbc_ref[:, cols] = _silu(y).astype(BF16)
    for k, u in enumerate(uz):
        z_ref[:, k * sub:(k + 1) * sub] = u.astype(BF16)


def _inproj1(h, w_steps, conv_w, conv_b, *, seq, tm, steps, sub, d_inner, conv_dim):
    t, kdim = h.shape
    xw, zw = conv_dim // steps, d_inner // steps
    in_specs = _halo_specs(t, tm, kdim) + [
        pl.BlockSpec((kdim, xw + zw), lambda i, j: (0, j)),
        pl.BlockSpec((3, xw), lambda i, j: (0, j)),
        pl.BlockSpec((1, xw), lambda i, j: (0, j))]
    return pl.pallas_call(
        functools.partial(_inproj1_kernel, tm=tm, tps=seq // tm, sub=sub),
        grid=(t // tm, steps),
        in_specs=in_specs,
        out_specs=[pl.BlockSpec((tm, zw), lambda i, j: (i, j)), pl.BlockSpec((tm, xw), lambda i, j: (i, j))],
        out_shape=[jax.ShapeDtypeStruct((t, d_inner), BF16), jax.ShapeDtypeStruct((t, conv_dim), BF16)],
        scratch_shapes=[pltpu.VMEM((tm + HALO, kdim), BF16)],
        compiler_params=_params("parallel", "arbitrary"),
        name="inproj1",
    )(h, h, h, w_steps, conv_w, conv_b)


def _ffn_kernel(*refs, tm, tps, nj, has_next):
    (h_ref, hp_ref, hn_ref, wg_ref, wv_ref, cwg_ref, cwv_ref, cbg_ref, cbv_ref,
     wd_ref, wdl_ref, x_ref, gp_ref) = refs[:13]
    pos = 13
    gn_ref = refs[pos] if has_next else None
    pos += int(has_next)
    xo_ref = refs[pos]
    ho_ref = refs[pos + 1] if has_next else None
    hext, act_a, act_b, acc = refs[-4:]
    j = pl.program_id(1)

    def step(act_prev, act_cur):
        hx = hext[...]
        ug = _dot(hx, wg_ref[...])
        uv = _dot(hx, wv_ref[...])
        if act_prev is None:
            acc[...] = jnp.zeros(acc.shape, F32)
        else:
            acc[...] += _dot(act_prev[...], wd_ref[...])
        gate = _conv_rows(ug, cwg_ref[...], cbg_ref[...], tm=tm)
        val = _conv_rows(uv, cwv_ref[...], cbv_ref[...], tm=tm)
        act_cur[...] = (_gelu_tanh(gate) * val).astype(BF16)

    @pl.when(j == 0)
    def _():
        _assemble_halo(hext, h_ref, hp_ref, hn_ref, tm=tm, tps=tps)
        step(None, act_a)

    @pl.when((j > 0) & (j % 2 == 0))
    def _():
        step(act_b, act_a)

    @pl.when(j % 2 == 1)
    def _():
        step(act_a, act_b)

    @pl.when(j == nj - 1)
    def _():
        act_last = act_a if (nj - 1) % 2 == 0 else act_b
        m = acc[...] + _dot(act_last[...], wdl_ref[...])
        xn = x_ref[...] + _rms(m, gp_ref[...])
        xo_ref[...] = xn
        if has_next:
            ho_ref[...] = _rms(xn, gn_ref[...]).astype(BF16)


def _ffn(h, w_up, conv_w, conv_b, w_down, x, g_post, g_next, *, seq, tm, tn):
    t, d = x.shape
    d_ff = w_down.shape[0]
    nj = d_ff // tn
    has_next = g_next is not None
    row = pl.BlockSpec((tm, d), lambda i, j: (i, 0))
    in_specs = _halo_specs(t, tm, d) + [
        pl.BlockSpec((d, tn), lambda i, j: (0, j)),
        pl.BlockSpec((d, tn), lambda i, j: (0, nj + j)),
        pl.BlockSpec((3, tn), lambda i, j: (0, j)),
        pl.BlockSpec((3, tn), lambda i, j: (0, nj + j)),
        pl.BlockSpec((1, tn), lambda i, j: (0, j)),
        pl.BlockSpec((1, tn), lambda i, j: (0, nj + j)),
        pl.BlockSpec((tn, d), lambda i, j: (jnp.maximum(j - 1, 0), 0)),
        pl.BlockSpec((tn, d), lambda i, j: (nj - 1, 0), pipeline_mode=pl.Buffered(1)),
        row, _resident((1, d))]
    args = [h, h, h, w_up, w_up, conv_w, conv_w, conv_b, conv_b, w_down, w_down, x, g_post]
    out_specs = [row]
    out_shape = [jax.ShapeDtypeStruct((t, d), F32)]
    if has_next:
        in_specs.append(_resident((1, d)))
        args.append(g_next)
        out_specs.append(row)
        out_shape.append(jax.ShapeDtypeStruct((t, d), BF16))
    res = pl.pallas_call(
        functools.partial(_ffn_kernel, tm=tm, tps=seq // tm, nj=nj, has_next=has_next),
        grid=(t // tm, nj),
        in_specs=in_specs, out_specs=out_specs, out_shape=out_shape,
        scratch_shapes=[pltpu.VMEM((tm + HALO, d), BF16), pltpu.VMEM((tm, tn), BF16),
                        pltpu.VMEM((tm, tn), BF16), pltpu.VMEM((tm, d), F32)],
        compiler_params=_params("parallel", "arbitrary"),
        name="ffn",
    )(*args)
    return res if has_next else (res[0], None)


def _dt_kernel(h_ref, w_ref, b_ref, o_ref, *, per_group):
    r = _dot(h_ref[...], w_ref[...]) + b_ref[...]
    sp = jnp.maximum(r, 0.0) + jnp.log(1.0 + jnp.exp(-jnp.abs(r)))
    for g in range(o_ref.shape[1] // LANES):
        o_ref[:, g * LANES:(g + 1) * LANES] = sp if g == 0 else pltpu.roll(sp, LANES - g * per_group, 1)


def _dt_proj(h, w, b, *, tm, groups, per_group):
    t, kdim = h.shape
    assert w.shape[1] == LANES == groups * per_group
    return pl.pallas_call(
        functools.partial(_dt_kernel, per_group=per_group),
        grid=(t // tm,),
        in_specs=[pl.BlockSpec((tm, kdim), lambda i: (i, 0)), _resident((kdim, LANES)), _resident((1, LANES))],
        out_specs=pl.BlockSpec((tm, groups * LANES), lambda i: (i, 0)),
        out_shape=jax.ShapeDtypeStruct((t, groups * LANES), F32),
        compiler_params=_params("parallel"),
        name="ssd_dt",
    )(h, w, b)


def _split3(x):
    hi = x.astype(BF16)
    r = x - hi.astype(F32)
    mid = r.astype(BF16)
    lo = (r - mid.astype(F32)).astype(BF16)
    return hi, mid, lo


def _cumsum_rows(tri, x):
    hi, mid, lo = _split3(x)
    return _dot(tri, hi) + _dot(tri, mid) + _dot(tri, lo)


def _expand(x, e):
    return _dot(x.astype(BF16), e)


class _Chunk:
    pass


def _ssd_positions(c, x_ref, b_ref, c_ref, dt_ref, a_ref, tri_ref):
    rows = slice(c.r0, c.r0 + SSM_CHUNK)
    c.x = x_ref[rows, :]
    c.bm = b_ref[rows, :]
    c.cm = c_ref[rows, :]
    c.dt = dt_ref[rows, :]
    c.dta = c.dt * a_ref[...]
    c.incl = _cumsum_rows(tri_ref[...], c.dta)


def _ssd_weights(c, e_ref):
    L = SSM_CHUNK
    total = c.incl[L - 1:L, :]
    if c.backward:
        c.pos = c.incl - c.dta
        w_state = c.dt * jnp.exp(c.pos)
        w_out = jnp.exp(total - c.pos)
    else:
        c.pos = c.incl
        w_state = c.dt * jnp.exp(total - c.pos)
        w_out = jnp.exp(c.pos)
    e = e_ref[...]
    c.w_out_x = _expand(w_out, e)
    c.carry = c.w_out_x[0:1] if c.backward else c.w_out_x[L - 1:L]
    c.xs = (c.x.astype(F32) * _expand(w_state, e)).astype(BF16)
    c.b_t = c.bm.astype(F32).T.astype(BF16)
    c.cb = lax.dot_general(c.cm, c.bm, _NT, preferred_element_type=F32)
    c.pos_t = c.pos.T
    c.dt_t = c.dt.T


def _ssd_state(c, h_in):
    c.y = _dot(c.cm, h_in.astype(BF16)) * c.w_out_x
    return h_in * c.carry + _dot(c.b_t, c.xs)


def _ssd_diag(c, y_ref, *, hpg):
    L = SSM_CHUNK
    rows = slice(c.r0, c.r0 + L)
    li = lax.broadcasted_iota(jnp.int32, (L, L), 0)
    si = lax.broadcasted_iota(jnp.int32, (L, L), 1)
    mask = (si >= li) if c.backward else (li >= si)
    lane = lax.broadcasted_iota(jnp.int32, (L, 2 * SSM_HEAD_DIM), 1)
    for jp in range(hpg // 2):
        sl = slice(jp * 2 * SSM_HEAD_DIM, (jp + 1) * 2 * SSM_HEAD_DIM)
        xp = c.x[:, sl]
        outs = []
        for j in (c.lane0 + 2 * jp, c.lane0 + 2 * jp + 1):
            col = c.pos[:, j:j + 1]
            row = c.pos_t[j:j + 1, :]
            d = (row - col) if c.backward else (col - row)
            w = c.cb * jnp.exp(jnp.where(mask, d, -1e30)) * c.dt_t[j:j + 1, :]
            outs.append(_dot(w.astype(BF16), xp))
        y_ref[rows, sl] = (c.y[:, sl] + jnp.where(lane < SSM_HEAD_DIM, outs[0], outs[1])).astype(y_ref.dtype)


def _ssd_kernel(xf, bf, cf, dtf, xb, bb, cb, dtb, a_ref, tri_ref, ef_ref, eb_ref,
                yf_ref, yb_ref, hf_sc, hb_sc, *, hpg, cps):
    @pl.when(pl.program_id(2) == 0)
    def _():
        hf_sc[...] = jnp.zeros(hf_sc.shape, F32)
        hb_sc[...] = jnp.zeros(hb_sc.shape, F32)

    fwd, bwd = [], []
    for k in range(cps):
        f, b = _Chunk(), _Chunk()
        f.r0, f.lane0, f.backward = k * SSM_CHUNK, 0, False
        b.r0, b.lane0, b.backward = (cps - 1 - k) * SSM_CHUNK, hpg, True
        fwd.append(f)
        bwd.append(b)
    for f, b in zip(fwd, bwd):
        _ssd_positions(f, xf, bf, cf, dtf, a_ref, tri_ref)
        _ssd_positions(b, xb, bb, cb, dtb, a_ref, tri_ref)
    for f, b in zip(fwd, bwd):
        _ssd_weights(f, ef_ref)
        _ssd_weights(b, eb_ref)
    hf = hf_sc[...]
    hb = hb_sc[...]
    for f, b in zip(fwd, bwd):
        hf = _ssd_state(f, hf)
        hb = _ssd_state(b, hb)
    hf_sc[...] = hf
    hb_sc[...] = hb
    for f, b in zip(fwd, bwd):
        _ssd_diag(f, yf_ref, hpg=hpg)
        _ssd_diag(b, yb_ref, hpg=hpg)


def _ssd(xbc, dt, a_pad, *, batch, seq, d_inner, cps):
    t = xbc.shape[0]
    L = SSM_CHUNK
    rows = cps * L
    assert seq % rows == 0
    nc = seq // rows
    gw = d_inner // SSM_N_GROUPS
    hpg = gw // SSM_HEAD_DIM
    b0 = d_inner // SSM_D_STATE
    c0 = b0 + SSM_N_GROUPS
    tri = jnp.asarray(np.tril(np.ones((L, L))), BF16)
    head_of_lane = np.arange(gw) // SSM_HEAD_DIM
    ef = jnp.asarray(np.arange(LANES)[:, None] == head_of_lane[None, :], BF16)
    eb = jnp.asarray(np.arange(LANES)[:, None] == (head_of_lane[None, :] + hpg), BF16)

    def fwd(col):
        return lambda b, g, c: (b * nc + c, col(g))

    def bwd(col):
        return lambda b, g, c: (b * nc + (nc - 1 - c), col(g))

    def specs(order):
        return [pl.BlockSpec((rows, gw), order(lambda g: g)),
                pl.BlockSpec((rows, SSM_D_STATE), order(lambda g: b0 + g)),
                pl.BlockSpec((rows, SSM_D_STATE), order(lambda g: c0 + g)),
                pl.BlockSpec((rows, LANES), order(lambda g: g))]

    in_specs = specs(fwd) + specs(bwd) + [
        pl.BlockSpec((1, LANES), lambda b, g, c: (0, g)),
        _resident((L, L)), _resident((LANES, gw)), _resident((LANES, gw))]
    return pl.pallas_call(
        functools.partial(_ssd_kernel, hpg=hpg, cps=cps),
        grid=(batch, SSM_N_GROUPS, nc),
        in_specs=in_specs,
        out_specs=[pl.BlockSpec((rows, gw), fwd(lambda g: g)), pl.BlockSpec((rows, gw), bwd(lambda g: g))],
        out_shape=[jax.ShapeDtypeStruct((t, d_inner), BF16), jax.ShapeDtypeStruct((t, d_inner), BF16)],
        scratch_shapes=[pltpu.VMEM((SSM_D_STATE, gw), F32), pltpu.VMEM((SSM_D_STATE, gw), F32)],
        compiler_params=_params("parallel", "parallel", "arbitrary"),
        name="ssd_scan",
    )(xbc, xbc, xbc, dt, xbc, xbc, xbc, dt, a_pad, tri, ef, eb)


def _ssd_out_kernel(yf_ref, yb_ref, xs_ref, z_ref, dsk_ref, ng_ref, w_ref, x_ref, gp_ref, gn_ref,
                    xo_ref, ho_ref, acc, ss, *, d_inner):
    kk = pl.program_id(1)

    @pl.when(kk == 0)
    def _():
        acc[...] = jnp.zeros(acc.shape, F32)
        ss[...] = jnp.zeros(ss.shape, F32)

    tk = w_ref.shape[0]
    sq = None
    part = None
    for c0 in range(0, tk, OUT_SUB):
        cols = slice(c0, c0 + OUT_SUB)
        y = (yf_ref[:, cols].astype(F32) + yb_ref[:, cols].astype(F32)
             + xs_ref[:, cols].astype(F32) * dsk_ref[:, cols])
        y = y * _silu(z_ref[:, cols].astype(F32))
        y2 = y * y
        for l0 in range(0, OUT_SUB, LANES):
            sq = y2[:, l0:l0 + LANES] if sq is None else sq + y2[:, l0:l0 + LANES]
        d = _dot((y * ng_ref[:, cols]).astype(BF16), w_ref[cols, :])
        part = d if part is None else part + d
    acc[...] += part
    ss[...] += sq

    @pl.when(kk == pl.num_programs(1) - 1)
    def _():
        ssum = jnp.sum(ss[...], axis=-1, keepdims=True)
        m = acc[...] * lax.rsqrt(ssum * (1.0 / d_inner) + EPS)
        xn = x_ref[...] + _rms(m, gp_ref[...])
        xo_ref[...] = xn
        ho_ref[...] = _rms(xn, gn_ref[...]).astype(BF16)


def _ssd_out(yf, yb, xbc, z, d_exp, norm_g, w, x, g_post, g_next, *, tm, tk):
    t, d = x.shape
    d_inner = yf.shape[1]
    lhs = pl.BlockSpec((tm, tk), lambda i, k: (i, k))
    vec = pl.BlockSpec((1, tk), lambda i, k: (0, k))
    row = pl.BlockSpec((tm, d), lambda i, k: (i, 0))
    w_spec = _resident((tk, d)) if tk == d_inner else pl.BlockSpec((tk, d), lambda i, k: (k, 0))
    return pl.pallas_call(
        functools.partial(_ssd_out_kernel, d_inner=d_inner),
        grid=(t // tm, d_inner // tk),
        in_specs=[lhs, lhs, lhs, lhs, vec, vec, w_spec, row, _resident((1, d)), _resident((1, d))],
        out_specs=[row, row],
        out_shape=[jax.ShapeDtypeStruct((t, d), F32), jax.ShapeDtypeStruct((t, d), BF16)],
        scratch_shapes=[pltpu.VMEM((tm, d), F32), pltpu.VMEM((tm, LANES), F32)],
        compiler_params=_params("parallel", "arbitrary"),
        name="ssd_out",
    )(yf, yb, xbc, z, d_exp, norm_g, w, x, g_post, g_next)


def _rope_tables(seq):
    rows = seq // GRID_W
    row = jnp.repeat(jnp.arange(rows, dtype=F32), GRID_W)
    col = jnp.tile(jnp.arange(GRID_W, dtype=F32), rows)
    axis_dim = HEAD_DIM // 2
    inv_freq = ROPE_THETA ** (-jnp.arange(0, axis_dim, 2, dtype=F32) / axis_dim)
    ang = jnp.concatenate([row[:, None] * inv_freq, col[:, None] * inv_freq], axis=-1)
    cos, sin = jnp.cos(ang), jnp.sin(ang)
    return jnp.concatenate([cos, cos], axis=-1), jnp.concatenate([-sin, sin], axis=-1)


def _pick(n, pref):
    t = min(pref, n)
    while n % t:
        t -= LANES
    return t


def kernel(x_prompt, x_sample, mix_pre_g, mix_post_g, ffn_pre_g, ffn_post_g, fa_w_in, fa_q_gain, fa_k_gain, fa_w_out, ssd_w_in, ssd_conv_w, ssd_conv_b, ssd_dt_bias, ssd_a_log, ssd_d, ssd_norm_g, ssd_w_out, ffn_w_up, ffn_conv_w, ffn_conv_b, ffn_w_down):
    depth = mix_pre_g.shape[0]
    d = x_prompt.shape[2]
    aw = N_Q_HEADS * HEAD_DIM
    kvw = N_KV_HEADS * HEAD_DIM
    fw = fa_w_in.shape[2] - aw - 2 * kvw
    assert fa_w_out.shape[1] == fw + aw and fw % FNET_GROUP_DIM == 0
    d_ff = ffn_w_down.shape[1]
    d_inner = ssd_w_out.shape[1]
    n_heads = d_inner // SSM_HEAD_DIM
    hpg = n_heads // SSM_N_GROUPS
    bc_w = SSM_N_GROUPS * SSM_D_STATE
    conv_dim = d_inner + 2 * bc_w
    assert hpg % 2 == 0 and 2 * hpg <= LANES and depth % 2 == 0
    tm = TOKEN_TILE
    vec = lambda v: v.reshape(1, -1).astype(F32)

    def regroup(p, pad):
        p = p.astype(F32).reshape(2, SSM_N_GROUPS, hpg).transpose(1, 0, 2).reshape(SSM_N_GROUPS, 2 * hpg)
        if pad:
            p = jnp.pad(p, ((0, 0), (0, LANES - 2 * hpg)))
        return p.reshape(1, -1)

    layers = []
    for i in range(depth):
        j = i // 2
        lw = dict(ffn_pre=vec(ffn_pre_g[i]), ffn_post=vec(ffn_post_g[i]), mix_pre=vec(mix_pre_g[i]),
                  mix_post=vec(mix_post_g[i]), w_up=ffn_w_up[i].astype(BF16), w_down=ffn_w_down[i].astype(BF16),
                  ffn_cw=ffn_conv_w[i].astype(F32), ffn_cb=vec(ffn_conv_b[i]))
        if i % 2 == 0:
            wo = fa_w_out[j].astype(BF16)
            lw.update(w_in=fa_w_in[j].astype(BF16), qg=vec(fa_q_gain[j]), kg=vec(fa_k_gain[j]),
                      wo_f=wo[:fw], wo_a=wo[fw:])
        else:
            w_dt = ssd_w_in[j][:, d_inner + conv_dim:].reshape(d, 2, SSM_N_GROUPS, hpg)
            w_dt = w_dt.transpose(0, 2, 1, 3).reshape(d, 2 * n_heads)
            w_bf = ssd_w_in[j].astype(BF16)
            xw, zw = conv_dim // IN1_STEPS, d_inner // IN1_STEPS
            w_steps = jnp.concatenate(
                [w_bf[:, d_inner:d_inner + conv_dim].reshape(d, IN1_STEPS, xw),
                 w_bf[:, :d_inner].reshape(d, IN1_STEPS, zw)], axis=2).reshape(d, conv_dim + d_inner)
            lw.update(w_in=w_steps, w_dt=w_dt.astype(BF16), dt_bias=regroup(ssd_dt_bias[j], pad=False),
                      a_pad=regroup(-jnp.exp(ssd_a_log[j].astype(F32)), pad=True), cw=ssd_conv_w[j].astype(F32),
                      cb=vec(ssd_conv_b[j]), d_exp=jnp.repeat(ssd_d[j].astype(F32), SSM_HEAD_DIM).reshape(1, d_inner),
                      norm_g=vec(ssd_norm_g[j]), w_out=ssd_w_out[j].astype(BF16))
        layers.append(lw)

    def trunk(xin):
        batch, seq, _ = xin.shape
        assert seq % tm == 0
        t = batch * seq
        x = xin.reshape(t, d)
        cosf, sinf = _rope_tables(seq)
        h = None
        for i, lw in enumerate(layers):
            if i % 2 == 0:
                u, q, k, v = _inproj0(x, lw["mix_pre"], lw["w_in"], lw["qg"], lw["kg"], cosf, sinf,
                                      seq=seq, fw=fw, aw=aw, kvw=kvw, tm=_pick(seq, IN0_TILE))
                a = _attention(q.reshape(batch, seq, aw), k.reshape(batch, seq, kvw), v.reshape(batch, seq, kvw),
                               tq=_pick(seq, ATTN_Q_TILE), tk=_pick(seq, ATTN_K_TILE)).reshape(t, aw)
                f = _fourier_mix(u, batch=batch, seq=seq)
                x, h = _proj_res(f, a, lw["wo_f"], lw["wo_a"], x, lw["mix_post"], lw["ffn_pre"], tm=tm)
            else:
                z, xbc = _inproj1(h, lw["w_in"], lw["cw"], lw["cb"], seq=seq, tm=_pick(seq, IN1_TILE),
                                  steps=IN1_STEPS, sub=IN1_SUB, d_inner=d_inner, conv_dim=conv_dim)
                dt = _dt_proj(h, lw["w_dt"], lw["dt_bias"], tm=tm, groups=SSM_N_GROUPS, per_group=2 * hpg)
                yf, yb = _ssd(xbc, dt, lw["a_pad"], batch=batch, seq=seq, d_inner=d_inner,
                              cps=SSD_CHUNKS_PER_STEP)
                x, h = _ssd_out(yf, yb, xbc, z, lw["d_exp"], lw["norm_g"], lw["w_out"], x,
                                lw["mix_post"], lw["ffn_pre"], tm=SSD_OUT_TILE, tk=d_inner)
            g_next = layers[i + 1]["mix_pre"] if i + 1 < depth else None
            x, h = _ffn(h, lw["w_up"], lw["ffn_cw"], lw["ffn_cb"], lw["w_down"], x, lw["ffn_post"], g_next,
                        seq=seq, tm=tm, tn=_pick(d_ff, FFN_CHUNK))
        return x.reshape(batch, seq, d)

    return (trunk(x_prompt), trunk(x_sample))
```
